```python
import math
import jax
import jax.numpy as jnp
from jax import lax
import numpy as np

D_MODEL = 1024
BATCH = 2
SEQ = 8192
DEPTH = 2
DEC_BATCH = 32
DEC_SEQ = 64
PAST_LEN = 2048

CHUNK = 64
A_HEADS = 6
A_HEAD_DIM = 64
A_BAND_CHUNKS = 8
A_PAST = A_BAND_CHUNKS * CHUNK
A_REL_CLIP = 128
B_HEADS = 4
B_KEY_DIM = 64
B_VAL_DIM = 96
B_GATE_RANK = 16
B_GATE_TAU = 16.0
C_HEADS = 4
C_Q_RANK = 192
C_KV_RANK = 128
C_NOPE_DIM = 64
C_ROPE_DIM = 32
C_V_DIM = 64
ROPE_THETA = 10000.0
N_EXPERTS = 32
TOP_K = 4
D_EXPERT = 1024
SWIGLU_LIMIT = 7.0
SWIGLU_ALPHA = 1.702
MOE_BLOCK = 128
ATTN_Q_BLOCK = 128
EPS = 1e-6
NEG = -1e30

A_W = A_HEADS * A_HEAD_DIM
B_W = B_HEADS * B_VAL_DIM
C_W = C_HEADS * C_V_DIM
MIX_W = A_W + B_W + C_W
B_KW = B_HEADS * B_KEY_DIM
IN_SPLITS = (A_W, A_W, A_W, B_KW, B_KW, B_W, B_GATE_RANK, B_W, C_Q_RANK, C_KV_RANK, C_ROPE_DIM)
IN_W = 3 * A_W + 2 * B_KW + 2 * B_W + B_GATE_RANK + C_Q_RANK + C_KV_RANK + C_ROPE_DIM

kernel_name = 'hybrid_chunk_streaming_encoder_step'

f32 = jnp.float32


def rmsnorm(x, g):
    xf = x.astype(f32)
    y = xf * lax.rsqrt(jnp.mean(xf * xf, axis=-1, keepdims=True) + EPS)
    return (y * g.astype(f32)).astype(x.dtype)


def rope(x, pos):
    half = C_ROPE_DIM // 2
    inv = ROPE_THETA ** (-jnp.arange(half, dtype=f32) / half)
    ang = pos.astype(f32)[:, None] * inv[None, :]
    shp = (ang.shape[0],) + (1,) * (x.ndim - 3) + (half,)
    cos = jnp.cos(ang).reshape(shp)
    sin = jnp.sin(ang).reshape(shp)
    xf = x.astype(f32)
    x1, x2 = xf[..., :half], xf[..., half:]
    return jnp.concatenate([x1 * cos - x2 * sin, x2 * cos + x1 * sin], axis=-1).astype(x.dtype)


def split_cols(h):
    out = []
    o = 0
    for w in IN_SPLITS:
        out.append(h[..., o:o + w])
        o += w
    return out


def project(xn, lp, pos):
    b_, t = xn.shape[0], xn.shape[1]
    h = jnp.einsum('btd,de->bte', xn, lp['w_in'])
    aq, ak, av, bq, bk, bv, blr, bg, cq, ckv, ckr = split_cols(h)
    aq = rmsnorm(aq.reshape(b_, t, A_HEADS, A_HEAD_DIM), lp['a_q_g'])
    ak = rmsnorm(ak.reshape(b_, t, A_HEADS, A_HEAD_DIM), lp['a_k_g'])
    av = av.reshape(b_, t, A_HEADS, A_HEAD_DIM)
    bq = bq.reshape(b_, t, B_HEADS, B_KEY_DIM)
    bk = bk.reshape(b_, t, B_HEADS, B_KEY_DIM)
    bv = bv.reshape(b_, t, B_HEADS, B_VAL_DIM)
    z = jnp.einsum('btr,re->bte', blr, lp['b_gate_w2']).astype(f32) + lp['b_gate_b'].astype(f32)
    b_log_a = (jax.nn.log_sigmoid(z) / B_GATE_TAU).reshape(b_, t, B_HEADS, B_KEY_DIM)
    cq = rmsnorm(cq, lp['c_qa_g'])
    cq = jnp.einsum('btr,re->bte', cq, lp['c_w_uq']).reshape(b_, t, C_HEADS, C_NOPE_DIM + C_ROPE_DIM)
    c_q = jnp.concatenate([rmsnorm(cq[..., :C_NOPE_DIM], lp['c_qn_g']),
                           rope(rmsnorm(cq[..., C_NOPE_DIM:], lp['c_qr_g']), pos)], axis=-1)
    ckv = rmsnorm(ckv, lp['c_kva_g'])
    ckr = rope(rmsnorm(ckr, lp['c_kr_g']), pos)
    return aq, ak, av, bq, bk, bv, b_log_a, bg, c_q, ckv, ckr


def rel_bias_lookup(rel_bias, rel):
    idx = jnp.clip(rel, -A_REL_CLIP, A_REL_CLIP) + A_REL_CLIP
    return rel_bias[:, idx].astype(f32)


def band_attn_prompt(q, k, v, rel_bias):
    b_, t, h_, dh = q.shape
    nc = t // CHUNK
    band = (A_BAND_CHUNKS + 1) * CHUNK
    qc = q.reshape(b_, nc, CHUNK, h_, dh)
    pad = ((0, 0), (A_BAND_CHUNKS, 0), (0, 0), (0, 0), (0, 0))
    kc = jnp.pad(k.reshape(b_, nc, CHUNK, h_, dh), pad)
    vc = jnp.pad(v.reshape(b_, nc, CHUNK, h_, dh), pad)
    idx = jnp.arange(nc)[:, None] + jnp.arange(A_BAND_CHUNKS + 1)[None, :]
    kb = kc[:, idx].reshape(b_, nc, band, h_, dh)
    vb = vc[:, idx].reshape(b_, nc, band, h_, dh)
    s = jnp.einsum('bcqhd,bckhd->bchqk', qc, kb).astype(f32) * (A_HEAD_DIM ** -0.5)
    qpos = A_PAST + jnp.arange(CHUNK)
    kpos = jnp.arange(band)
    s = s + rel_bias_lookup(rel_bias, qpos[:, None] - kpos[None, :])[None, None]
    valid = (jnp.arange(nc)[:, None] * CHUNK - A_PAST + kpos[None, :]) >= 0
    s = jnp.where(valid[None, :, None, None, :], s, NEG)
    p = jax.nn.softmax(s, axis=-1).astype(v.dtype)
    o = jnp.einsum('bchqk,bckhd->bcqhd', p, vb)
    return o.reshape(b_, t, h_, dh)


def band_attn_sample(q, k_new, v_new, k_cache, v_cache, rel_bias):
    s_len = q.shape[1]
    l_past = k_cache.shape[1]
    k = jnp.concatenate([k_cache.astype(k_new.dtype), k_new], axis=1)
    v = jnp.concatenate([v_cache.astype(v_new.dtype), v_new], axis=1)
    qpos = PAST_LEN + jnp.arange(s_len)
    kpos = PAST_LEN - l_past + jnp.arange(l_past + s_len)
    s = jnp.einsum('bqhd,bkhd->bhqk', q, k).astype(f32) * (A_HEAD_DIM ** -0.5)
    s = s + rel_bias_lookup(rel_bias, qpos[:, None] - kpos[None, :])[None]
    p = jax.nn.softmax(s, axis=-1).astype(v.dtype)
    return jnp.einsum('bhqk,bkhd->bqhd', p, v)


def gla_blockwise(q, k, v, log_a, s0, block):
    b_, t, h_, dk = q.shape
    dv = v.shape[-1]
    nb = t // block
    q_ = q.astype(f32).reshape(b_, nb, block, h_, dk) * (dk ** -0.5)
    k_ = k.astype(f32).reshape(b_, nb, block, h_, dk)
    v_ = v.astype(f32).reshape(b_, nb, block, h_, dv)
    la = log_a.astype(f32).reshape(b_, nb, block, h_, dk)
    bcum = jnp.cumsum(la, axis=2)
    b_last = bcum[:, :, -1:]
    qt = q_ * jnp.exp(bcum)
    kt = k_ * jnp.exp(-bcum)
    causal = jnp.tril(jnp.ones((block, block), dtype=bool))
    att = jnp.where(causal, jnp.einsum('bnlhd,bnmhd->bnhlm', qt, kt), 0.0)
    o_intra = jnp.einsum('bnhlm,bnmhe->bnlhe', att, v_)
    ds = jnp.einsum('bnlhd,bnlhe->bnhde', k_ * jnp.exp(b_last - bcum), v_)
    decay = jnp.exp(b_last[:, :, 0])

    def step(s, inp):
        dec, d = inp
        return dec[..., None] * s + d, s

    s_fin, s_prev = lax.scan(step, s0.astype(f32), (jnp.moveaxis(decay, 1, 0), jnp.moveaxis(ds, 1, 0)))
    s_prev = jnp.moveaxis(s_prev, 0, 1)
    o_inter = jnp.einsum('bnlhd,bnhde->bnlhe', qt, s_prev)
    return (o_intra + o_inter).reshape(b_, t, h_, dv), s_fin


def gla_output(o, g, gain, dtype):
    b_, t = o.shape[0], o.shape[1]
    gg = jax.nn.silu(g.astype(f32)).reshape(b_, t, B_HEADS, B_VAL_DIM)
    return (rmsnorm(o, gain) * gg).astype(dtype)


def mla_keys(ckv, ckr, lp):
    b_, t = ckv.shape[0], ckv.shape[1]
    kv = jnp.einsum('btr,re->bte', ckv, lp['c_w_ukv']).reshape(b_, t, C_HEADS, C_NOPE_DIM + C_V_DIM)
    kn = rmsnorm(kv[..., :C_NOPE_DIM], lp['c_kn_g'])
    kr = jnp.broadcast_to(ckr[:, :, None, :].astype(kn.dtype), (b_, t, C_HEADS, C_ROPE_DIM))
    return jnp.concatenate([kn, kr], axis=-1), kv[..., C_NOPE_DIM:]


def mla_attn_prompt(q, k, v):
    b_, t, h_, dq = q.shape
    nqb = t // ATTN_Q_BLOCK
    qb = jnp.moveaxis(q.reshape(b_, nqb, ATTN_Q_BLOCK, h_, dq), 1, 0)
    kpos = jnp.arange(t)
    scale = (C_NOPE_DIM + C_ROPE_DIM) ** -0.5

    def one(args):
        qblk, i = args
        qpos = i * ATTN_Q_BLOCK + jnp.arange(ATTN_Q_BLOCK)
        limit = (qpos // CHUNK + 1) * CHUNK
        s = jnp.einsum('bqhd,bkhd->bhqk', qblk, k).astype(f32) * scale
        s = jnp.where(kpos[None, :] < limit[:, None], s, NEG)
        p = jax.nn.softmax(s, axis=-1).astype(v.dtype)
        return jnp.einsum('bhqk,bkhd->bqhd', p, v)

    o = lax.map(one, (qb, jnp.arange(nqb)))
    return jnp.moveaxis(o, 0, 1).reshape(b_, t, h_, C_V_DIM)


def attn_full(q, k, v):
    s = jnp.einsum('bqhd,bkhd->bhqk', q, k).astype(f32) * ((C_NOPE_DIM + C_ROPE_DIM) ** -0.5)
    p = jax.nn.softmax(s, axis=-1).astype(v.dtype)
    return jnp.einsum('bhqk,bkhd->bqhd', p, v)


def mix_out(a_o, b_o, c_o, w_out):
    b_, t = a_o.shape[0], a_o.shape[1]
    o = jnp.concatenate([a_o.reshape(b_, t, A_W), b_o.reshape(b_, t, B_W), c_o.reshape(b_, t, C_W)], axis=-1)
    return jnp.einsum('bte,ed->btd', o, w_out)


def token_mix_prompt(xn, lp):
    b_, t = xn.shape[0], xn.shape[1]
    pos = jnp.arange(t)
    aq, ak, av, bq, bk, bv, b_log_a, bg, c_q, ckv, ckr = project(xn, lp, pos)
    a_o = band_attn_prompt(aq, ak, av, lp['a_rel_bias'])
    s0 = jnp.zeros((b_, B_HEADS, B_KEY_DIM, B_VAL_DIM), f32)
    b_o, b_s = gla_blockwise(bq, bk, bv, b_log_a, s0, CHUNK)
    b_o = gla_output(b_o, bg, lp['b_out_g'], xn.dtype)
    ck, cv = mla_keys(ckv, ckr, lp)
    c_o = mla_attn_prompt(c_q, ck, cv)
    y = mix_out(a_o, b_o, c_o, lp['w_out'])
    la = min(A_PAST, t)
    return y, (ak[:, t - la:], av[:, t - la:], b_s, ckv, ckr)


def token_mix_sample(xn, lp, a_k_c, a_v_c, b_s_c, c_kv_c, c_kr_c):
    s_len = xn.shape[1]
    pos = PAST_LEN + jnp.arange(s_len)
    aq, ak, av, bq, bk, bv, b_log_a, bg, c_q, ckv, ckr = project(xn, lp, pos)
    a_o = band_attn_sample(aq, ak, av, a_k_c, a_v_c, lp['a_rel_bias'])
    b_o, b_s = gla_blockwise(bq, bk, bv, b_log_a, b_s_c, s_len)
    b_o = gla_output(b_o, bg, lp['b_out_g'], xn.dtype)
    ck_new, cv_new = mla_keys(ckv, ckr, lp)
    ck_past, cv_past = mla_keys(c_kv_c.astype(xn.dtype), c_kr_c.astype(xn.dtype), lp)
    c_o = attn_full(c_q, jnp.concatenate([ck_past, ck_new], axis=1), jnp.concatenate([cv_past, cv_new], axis=1))
    y = mix_out(a_o, b_o, c_o, lp['w_out'])
    return y, (ak, av, b_s, ckv, ckr)


def moe(x, lp):
    shp = x.shape
    xt = x.reshape(-1, D_MODEL)
    n = xt.shape[0]
    logits = jnp.einsum('nd,de->ne', xt, lp['router_w']).astype(f32) + lp['router_b'].astype(f32)
    top_v, top_i = lax.top_k(logits, TOP_K)
    gates = jax.nn.softmax(top_v, axis=-1)
    e_flat = top_i.reshape(-1)
    tok_flat = jnp.repeat(jnp.arange(n, dtype=jnp.int32), TOP_K)
    g_flat = gates.reshape(-1)
    order = jnp.argsort(e_flat)
    e_sorted = e_flat[order]
    counts = jnp.bincount(e_flat, length=N_EXPERTS)
    padded = ((counts + MOE_BLOCK - 1) // MOE_BLOCK) * MOE_BLOCK
    pend = jnp.cumsum(padded)
    pstart = pend - padded
    ustart = jnp.cumsum(counts) - counts
    dest = pstart[e_sorted] + jnp.arange(n * TOP_K) - ustart[e_sorted]
    n_blocks = -(-(n * TOP_K) // MOE_BLOCK) + N_EXPERTS
    rows = n_blocks * MOE_BLOCK
    row_tok = jnp.zeros((rows,), jnp.int32).at[dest].set(tok_flat[order])
    row_gate = jnp.zeros((rows,), f32).at[dest].set(g_flat[order])
    blk_e = jnp.minimum(jnp.searchsorted(pend, jnp.arange(n_blocks) * MOE_BLOCK, side='right'), N_EXPERTS - 1)
    w_gate, b_gate = lp['e_w_gate'], lp['e_b_gate']
    w_up, b_up = lp['e_w_up'], lp['e_b_up']
    w_down, b_down = lp['e_w_down'], lp['e_b_down']

    def expert_block(args):
        toks, gts, e = args
        xb = xt[toks]
        g = xb @ w_gate[e] + b_gate[e]
        u = xb @ w_up[e] + b_up[e]
        g = jnp.minimum(g, SWIGLU_LIMIT)
        u = jnp.clip(u, -SWIGLU_LIMIT, SWIGLU_LIMIT)
        hdn = (u + 1.0) * (g * jax.nn.sigmoid(g * SWIGLU_ALPHA))
        y = hdn @ w_down[e] + b_down[e]
        return y * gts[:, None].astype(y.dtype)

    y_rows = lax.map(expert_block, (row_tok.reshape(n_blocks, MOE_BLOCK), row_gate.reshape(n_blocks, MOE_BLOCK), blk_e))
    out = jnp.zeros_like(xt).at[row_tok].add(y_rows.reshape(rows, D_MODEL).astype(xt.dtype))
    return out.reshape(shp)


def setup_inputs(seed: int = 0) -> dict:
    key = jax.random.key(seed)
    ks = jax.random.split(key, 40)

    def nrm(k, shape, scale):
        return scale * jax.random.normal(k, shape, f32)

    def gain(k, shape):
        return 1.0 + 0.02 * jax.random.normal(k, shape, f32)

    la = min(A_PAST, PAST_LEN)
    return {
        'x_prompt': nrm(ks[0], (BATCH, SEQ, D_MODEL), 1.0),
        'x_sample': nrm(ks[1], (DEC_BATCH, DEC_SEQ, D_MODEL), 1.0),
        'cache_a_k': nrm(ks[2], (DEPTH, DEC_BATCH, la, A_HEADS, A_HEAD_DIM), 1.0),
        'cache_a_v': nrm(ks[3], (DEPTH, DEC_BATCH, la, A_HEADS, A_HEAD_DIM), 1.0),
        'state_b_s': nrm(ks[4], (DEPTH, DEC_BATCH, B_HEADS, B_KEY_DIM, B_VAL_DIM), 1.0),
        'cache_c_kv': nrm(ks[5], (DEPTH, DEC_BATCH, PAST_LEN, C_KV_RANK), 1.0),
        'cache_c_kr': nrm(ks[6], (DEPTH, DEC_BATCH, PAST_LEN, C_ROPE_DIM), 1.0),
        'ln1_g': gain(ks[7], (DEPTH, D_MODEL)),
        'ln2_g': gain(ks[8], (DEPTH, D_MODEL)),
        'w_in': nrm(ks[9], (DEPTH, D_MODEL, IN_W), D_MODEL ** -0.5),
        'a_q_g': gain(ks[10], (DEPTH, A_HEAD_DIM)),
        'a_k_g': gain(ks[11], (DEPTH, A_HEAD_DIM)),
        'a_rel_bias': nrm(ks[12], (DEPTH, A_HEADS, 2 * A_REL_CLIP + 1), 0.1),
        'b_gate_w2': nrm(ks[13], (DEPTH, B_GATE_RANK, B_KW), B_GATE_RANK ** -0.5),
        'b_gate_b': nrm(ks[14], (DEPTH, B_KW), 0.1),
        'b_out_g': gain(ks[15], (DEPTH, B_VAL_DIM)),
        'c_qa_g': gain(ks[16], (DEPTH, C_Q_RANK)),
        'c_w_uq': nrm(ks[17], (DEPTH, C_Q_RANK, C_HEADS * (C_NOPE_DIM + C_ROPE_DIM)), C_Q_RANK ** -0.5),
        'c_kva_g': gain(ks[18], (DEPTH, C_KV_RANK)),
        'c_w_ukv': nrm(ks[19], (DEPTH, C_KV_RANK, C_HEADS * (C_NOPE_DIM + C_V_DIM)), C_KV_RANK ** -0.5),
        'c_qn_g': gain(ks[20], (DEPTH, C_NOPE_DIM)),
        'c_qr_g': gain(ks[21], (DEPTH, C_ROPE_DIM)),
        'c_kn_g': gain(ks[22], (DEPTH, C_NOPE_DIM)),
        'c_kr_g': gain(ks[23], (DEPTH, C_ROPE_DIM)),
        'w_out': nrm(ks[24], (DEPTH, MIX_W, D_MODEL), 0.5 * MIX_W ** -0.5),
        'router_w': nrm(ks[25], (DEPTH, D_MODEL, N_EXPERTS), D_MODEL ** -0.5),
        'router_b': nrm(ks[26], (DEPTH, N_EXPERTS), 0.01),
        'e_w_gate': nrm(ks[27], (DEPTH, N_EXPERTS, D_MODEL, D_EXPERT), D_MODEL ** -0.5),
        'e_b_gate': nrm(ks[28], (DEPTH, N_EXPERTS, D_EXPERT), 0.01),
        'e_w_up': nrm(ks[29], (DEPTH, N_EXPERTS, D_MODEL, D_EXPERT), D_MODEL ** -0.5),
        'e_b_up': nrm(ks[30], (DEPTH, N_EXPERTS, D_EXPERT), 0.01),
        'e_w_down': nrm(ks[31], (DEPTH, N_EXPERTS, D_EXPERT, D_MODEL), 0.5 * D_EXPERT ** -0.5),
        'e_b_down': nrm(ks[32], (DEPTH, N_EXPERTS, D_MODEL), 0.01),
    }


def reference(x_prompt, x_sample, cache_a_k, cache_a_v, state_b_s, cache_c_kv, cache_c_kr,
              ln1_g, ln2_g, w_in, a_q_g, a_k_g, a_rel_bias, b_gate_w2, b_gate_b, b_out_g,
              c_qa_g, c_w_uq, c_kva_g, c_w_ukv, c_qn_g, c_qr_g, c_kn_g, c_kr_g, w_out,
              router_w, router_b, e_w_gate, e_b_gate, e_w_up, e_b_up, e_w_down, e_b_down):
    h_p, h_s = x_prompt, x_sample
    ak_p, av_p, bs_p, ckv_p, ckr_p = [], [], [], [], []
    ak_s, av_s, bs_s, ckv_s, ckr_s = [], [], [], [], []
    for l in range(DEPTH):
        lp = {
            'w_in': w_in[l], 'a_q_g': a_q_g[l], 'a_k_g': a_k_g[l], 'a_rel_bias': a_rel_bias[l],
            'b_gate_w2': b_gate_w2[l], 'b_gate_b': b_gate_b[l], 'b_out_g': b_out_g[l],
            'c_qa_g': c_qa_g[l], 'c_w_uq': c_w_uq[l], 'c_kva_g': c_kva_g[l], 'c_w_ukv': c_w_ukv[l],
            'c_qn_g': c_qn_g[l], 'c_qr_g': c_qr_g[l], 'c_kn_g': c_kn_g[l], 'c_kr_g': c_kr_g[l],
            'w_out': w_out[l], 'router_w': router_w[l], 'router_b': router_b[l],
            'e_w_gate': e_w_gate[l], 'e_b_gate': e_b_gate[l], 'e_w_up': e_w_up[l], 'e_b_up': e_b_up[l],
            'e_w_down': e_w_down[l], 'e_b_down': e_b_down[l],
        }
        y, st = token_mix_prompt(rmsnorm(h_p, ln1_g[l]), lp)
        h_p = h_p + y
        h_p = h_p + moe(rmsnorm(h_p, ln2_g[l]), lp)
        for lst, s in zip((ak_p, av_p, bs_p, ckv_p, ckr_p), st):
            lst.append(s)
        y, st = token_mix_sample(rmsnorm(h_s, ln1_g[l]), lp, cache_a_k[l], cache_a_v[l],
                                 state_b_s[l], cache_c_kv[l], cache_c_kr[l])
        h_s = h_s + y
        h_s = h_s + moe(rmsnorm(h_s, ln2_g[l]), lp)
        for lst, s in zip((ak_s, av_s, bs_s, ckv_s, ckr_s), st):
            lst.append(s)
    return (h_p, h_s,
            jnp.stack(ak_p), jnp.stack(av_p), jnp.stack(bs_p), jnp.stack(ckv_p), jnp.stack(ckr_p),
            jnp.stack(ak_s), jnp.stack(av_s), jnp.stack(bs_s), jnp.stack(ckv_s), jnp.stack(ckr_s))
```

```python
import functools
import math

import jax
import jax.numpy as jnp
from jax import lax
from jax.experimental import pallas as pl
from jax.experimental.pallas import tpu as pltpu

f32 = jnp.float32
bf16 = jnp.bfloat16
i32 = jnp.int32

D_MODEL = 1024
PAST_LEN = 2048
CHUNK = 64
A_HEADS = 6
A_HEAD_DIM = 64
A_BAND_CHUNKS = 8
A_PAST = A_BAND_CHUNKS * CHUNK
A_BAND = A_PAST + CHUNK
A_REL_CLIP = 128
B_HEADS = 4
B_KEY_DIM = 64
B_VAL_DIM = 96
B_GATE_RANK = 16
B_GATE_TAU = 16.0
C_HEADS = 4
C_Q_RANK = 192
C_KV_RANK = 128
C_NOPE_DIM = 64
C_ROPE_DIM = 32
C_V_DIM = 64
C_PAD_DIM = 128
ROPE_THETA = 10000.0
N_EXPERTS = 32
TOP_K = 4
D_EXPERT = 1024
SWIGLU_LIMIT = 7.0
SWIGLU_ALPHA = 1.702
EPS = 1e-6
NEG = -1e30

A_W = A_HEADS * A_HEAD_DIM
B_W = B_HEADS * B_VAL_DIM
C_W = C_HEADS * C_V_DIM
B_KW = B_HEADS * B_KEY_DIM
C_QW = C_HEADS * C_PAD_DIM

P_AQ, P_AK, P_AV = 0, 384, 768
P_BQ, P_BK, P_BV, P_BG = 1152, 1408, 1664, 2048
P_CQ, P_CKV, P_CKR, P_BLR = 2432, 2688, 2816, 2848
IN_P = 2944
S_BQ, S_BK, S_BV, S_LA, S_BG, S_W = 0, 256, 512, 896, 1152, 1536

ROW_TILE = 512
MOE_BLOCK = 256
COMBINE_TILE = 128
VMEM_LIMIT = 56 * 1024 * 1024


def _cparams(*sem):
    return pltpu.CompilerParams(dimension_semantics=sem, vmem_limit_bytes=VMEM_LIMIT)


def _rms(x, g):
    return x * lax.rsqrt(jnp.mean(x * x, axis=-1, keepdims=True) + EPS) * g


def _rope(x, cos, sin):
    half = C_ROPE_DIM // 2
    x1, x2 = x[:, :half], x[:, half:]
    return jnp.concatenate([x1 * cos - x2 * sin, x2 * cos + x1 * sin], axis=-1)


def _dot_nt(a, b):
    return lax.dot_general(a, b, (((1,), (1,)), ((), ())), preferred_element_type=f32)


def _dot_tn(a, b):
    return lax.dot_general(a, b, (((0,), (0,)), ((), ())), preferred_element_type=f32)


def _sigmoid(x):
    return 1.0 / (1.0 + jnp.exp(-x))


def _proj_body(x_ref, g1_ref, w_ref, aqg_ref, akg_ref, w2_ref, gb_ref, qag_ref, wuq_ref,
               qng_ref, qrg_ref, kvag_ref, krg_ref, cos_ref, sin_ref,
               aq_o, ak_o, av_o, akb_o, avb_o, b_o, cq_o, ckv_o, ckr_o, h_sc):
    xn = _rms(x_ref[...], g1_ref[...])
    h_sc[...] = jnp.dot(xn.astype(bf16), w_ref[...], preferred_element_type=f32)
    cos = cos_ref[...]
    sin = sin_ref[...]
    for hd in range(A_HEADS):
        lo = hd * A_HEAD_DIM
        q = _rms(h_sc[:, P_AQ + lo:P_AQ + lo + A_HEAD_DIM], aqg_ref[...])
        aq_o[:, lo:lo + A_HEAD_DIM] = (q * (A_HEAD_DIM ** -0.5)).astype(bf16)
        k = _rms(h_sc[:, P_AK + lo:P_AK + lo + A_HEAD_DIM], akg_ref[...])
        ak_o[:, lo:lo + A_HEAD_DIM] = k
        akb_o[:, lo:lo + A_HEAD_DIM] = k.astype(bf16)
    av = h_sc[:, P_AV:P_AV + A_W]
    av_o[...] = av
    avb_o[...] = av.astype(bf16)
    b_o[:, S_BQ:S_LA] = h_sc[:, P_BQ:P_BG]
    z = jnp.dot(h_sc[:, P_BLR:P_BLR + B_GATE_RANK], w2_ref[...], preferred_element_type=f32,
                precision=lax.Precision.HIGHEST) + gb_ref[...]
    log_sig = jnp.minimum(z, 0.0) - jnp.log1p(jnp.exp(-jnp.abs(z)))
    b_o[:, S_LA:S_BG] = log_sig * (1.0 / B_GATE_TAU)
    b_o[:, S_BG:S_W] = h_sc[:, P_BG:P_BG + B_W]
    cql = _rms(h_sc[:, P_CQ:P_CQ + C_Q_RANK], qag_ref[...])
    cq = jnp.dot(cql.astype(bf16), wuq_ref[...], preferred_element_type=f32)
    zeros = jnp.zeros((cq.shape[0], C_PAD_DIM - C_NOPE_DIM - C_ROPE_DIM), bf16)
    for hd in range(C_HEADS):
        lo = hd * C_PAD_DIM
        nope = _rms(cq[:, lo:lo + C_NOPE_DIM], qng_ref[...])
        rot = _rope(_rms(cq[:, lo + C_NOPE_DIM:lo + C_NOPE_DIM + C_ROPE_DIM], qrg_ref[...]), cos, sin)
        cq_o[:, lo:lo + C_NOPE_DIM] = nope.astype(bf16)
        cq_o[:, lo + C_NOPE_DIM:lo + C_NOPE_DIM + C_ROPE_DIM] = rot.astype(bf16)
        cq_o[:, lo + C_NOPE_DIM + C_ROPE_DIM:lo + C_PAD_DIM] = zeros
    ckv_o[...] = _rms(h_sc[:, P_CKV:P_CKV + C_KV_RANK], kvag_ref[...])
    ckr_o[...] = _rope(_rms(h_sc[:, P_CKR:P_CKR + C_ROPE_DIM], krg_ref[...]), cos, sin)


def _proj(x, lw, cos, sin):
    n = x.shape[0]
    tm = ROW_TILE
    npos = cos.shape[0] // tm
    row = lambda w: pl.BlockSpec((tm, w), lambda i: (i, 0))
    const = lambda a: pl.BlockSpec(a.shape, lambda i: (0,) * a.ndim)
    pos = pl.BlockSpec((tm, C_ROPE_DIM // 2), lambda i: (i % npos, 0))
    consts = (lw['ln1_g'], lw['w_in'], lw['a_q_g'], lw['a_k_g'], lw['b_gate_w2'], lw['b_gate_b'],
              lw['c_qa_g'], lw['c_w_uq'], lw['c_qn_g'], lw['c_qr_g'], lw['c_kva_g'], lw['c_kr_g'])
    out_w = ((A_W, bf16), (A_W, f32), (A_W, f32), (A_W, bf16), (A_W, bf16), (S_W, f32),
             (C_QW, bf16), (C_KV_RANK, f32), (C_ROPE_DIM, f32))
    return pl.pallas_call(
        _proj_body,
        grid=(n // tm,),
        in_specs=[row(D_MODEL)] + [const(a) for a in consts] + [pos, pos],
        out_specs=[row(w) for w, _ in out_w],
        out_shape=[jax.ShapeDtypeStruct((n, w), dt) for w, dt in out_w],
        scratch_shapes=[pltpu.VMEM((tm, IN_P), f32)],
        compiler_params=_cparams("parallel"),
        name="proj",
    )(x, *consts, cos, sin)


def _mla_keys_body(ckv_ref, ckr_ref, w_ref, kng_ref, ck_o, cv_o):
    kv = jnp.dot(ckv_ref[...].astype(bf16), w_ref[...], preferred_element_type=f32)
    kr = ckr_ref[...].astype(bf16)
    zeros = jnp.zeros((kv.shape[0], C_PAD_DIM - C_NOPE_DIM - C_ROPE_DIM), bf16)
    for hd in range(C_HEADS):
        lo = hd * C_PAD_DIM
        kn = _rms(kv[:, hd * C_NOPE_DIM:(hd + 1) * C_NOPE_DIM], kng_ref[...])
        ck_o[:, lo:lo + C_NOPE_DIM] = kn.astype(bf16)
        ck_o[:, lo + C_NOPE_DIM:lo + C_NOPE_DIM + C_ROPE_DIM] = kr
        ck_o[:, lo + C_NOPE_DIM + C_ROPE_DIM:lo + C_PAD_DIM] = zeros
    cv_o[...] = kv[:, C_HEADS * C_NOPE_DIM:].astype(bf16)


def _mla_keys(ckv, ckr, lw):
    n = ckv.shape[0]
    tm = ROW_TILE
    row = lambda w: pl.BlockSpec((tm, w), lambda i: (i, 0))
    const = lambda a: pl.BlockSpec(a.shape, lambda i: (0,) * a.ndim)
    return pl.pallas_call(
        _mla_keys_body,
        grid=(n // tm,),
        in_specs=[row(C_KV_RANK), row(C_ROPE_DIM), const(lw['c_w_ukv']), const(lw['c_kn_g'])],
        out_specs=[row(C_QW), row(C_W)],
        out_shape=[jax.ShapeDtypeStruct((n, C_QW), bf16), jax.ShapeDtypeStruct((n, C_W), bf16)],
        compiler_params=_cparams("parallel"),
        name="mla_keys",
    )(ckv, ckr, lw['c_w_ukv'], lw['c_kn_g'])


def _band_body(q_ref, kp_ref, kc_ref, vp_ref, vc_ref, bias_ref, o_ref, *, chunks, past_from_seq):
    blk = pl.program_id(1)
    col = lax.broadcasted_iota(i32, (CHUNK, A_BAND), 1)
    for hd in range(A_HEADS):
        sl = slice(hd * A_HEAD_DIM, (hd + 1) * A_HEAD_DIM)
        k_all = jnp.concatenate([kp_ref[:, sl], kc_ref[:, sl]], axis=0)
        v_all = jnp.concatenate([vp_ref[:, sl], vc_ref[:, sl]], axis=0)
        bias = bias_ref[hd]
        for c in range(chunks):
            q = q_ref[c * CHUNK:(c + 1) * CHUNK, sl]
            s = _dot_nt(q, k_all[c * CHUNK:c * CHUNK + A_BAND]) + bias
            if past_from_seq:
                s = jnp.where(col + ((blk * chunks + c) * CHUNK - A_PAST) >= 0, s, NEG)
            p = jnp.exp(s - jnp.max(s, axis=-1, keepdims=True))
            p = p / jnp.sum(p, axis=-1, keepdims=True)
            o = jnp.dot(p.astype(bf16), v_all[c * CHUNK:c * CHUNK + A_BAND], preferred_element_type=f32)
            o_ref[c * CHUNK:(c + 1) * CHUNK, sl] = o.astype(bf16)


def _band(q, k, v, k_past, v_past, bias):
    b_, t, _ = q.shape
    from_seq = k_past is None
    chunks = A_BAND_CHUNKS if from_seq else 1
    rows = chunks * CHUNK
    cur = pl.BlockSpec((None, rows, A_W), lambda b, i: (b, i, 0))
    if from_seq:
        past = pl.BlockSpec((None, A_PAST, A_W), lambda b, i: (b, jnp.maximum(i - 1, 0), 0))
        k_past, v_past = k, v
    else:
        past = pl.BlockSpec((None, A_PAST, A_W), lambda b, i: (b, 0, 0))
    return pl.pallas_call(
        functools.partial(_band_body, chunks=chunks, past_from_seq=from_seq),
        grid=(b_, t // rows),
        in_specs=[cur, past, cur, past, cur, pl.BlockSpec(bias.shape, lambda b, i: (0, 0, 0))],
        out_specs=cur,
        out_shape=jax.ShapeDtypeStruct((b_, t, A_W), bf16),
        compiler_params=_cparams("parallel", "parallel"),
        name="band",
    )(q, k_past, k, v_past, v, bias)


def _gla_body(b_ref, s0_ref, gain_ref, o_ref, st_ref, *, chunks):
    @pl.when(pl.program_id(1) == 0)
    def _():
        st_ref[...] = s0_ref[...]

    r = lax.broadcasted_iota(i32, (CHUNK, CHUNK), 0)
    c = lax.broadcasted_iota(i32, (CHUNK, CHUNK), 1)
    causal = r >= c
    tril = causal.astype(f32)
    for ch in range(chunks):
        rows = slice(ch * CHUNK, (ch + 1) * CHUNK)
        for hd in range(B_HEADS):
            kd = slice(hd * B_KEY_DIM, (hd + 1) * B_KEY_DIM)
            vd = slice(hd * B_VAL_DIM, (hd + 1) * B_VAL_DIM)
            q = b_ref[rows, S_BQ + kd.start:S_BQ + kd.stop] * (B_KEY_DIM ** -0.5)
            k = b_ref[rows, S_BK + kd.start:S_BK + kd.stop]
            v = b_ref[rows, S_BV + vd.start:S_BV + vd.stop]
            la = b_ref[rows, S_LA + kd.start:S_LA + kd.stop]
            g = b_ref[rows, S_BG + vd.start:S_BG + vd.stop]
            bcum = jnp.dot(tril, la, preferred_element_type=f32, precision=lax.Precision.HIGHEST)
            b_last = bcum[CHUNK - 1:CHUNK, :]
            qt = q * jnp.exp(bcum)
            kt = k * jnp.exp(-bcum)
            att = jnp.where(causal, _dot_nt(qt, kt), 0.0)
            st = st_ref[hd]
            o = jnp.dot(att, v, preferred_element_type=f32) + _dot_nt(qt, st)
            ds_t = _dot_tn(v, k * jnp.exp(b_last - bcum))
            st_ref[hd] = st * jnp.exp(b_last) + ds_t
            o = _rms(o, gain_ref[...]) * (g * _sigmoid(g))
            o_ref[rows, vd] = o.astype(bf16)


def _gla(slab, s0_t, gain):
    b_, t, _ = slab.shape
    chunks = min(A_BAND_CHUNKS, t // CHUNK)
    rows = chunks * CHUNK
    st_spec = pl.BlockSpec((None, B_HEADS, B_VAL_DIM, B_KEY_DIM), lambda b, i: (b, 0, 0, 0))
    return pl.pallas_call(
        functools.partial(_gla_body, chunks=chunks),
        grid=(b_, t // rows),
        in_specs=[pl.BlockSpec((None, rows, S_W), lambda b, i: (b, i, 0)), st_spec,
                  pl.BlockSpec(gain.shape, lambda b, i: (0, 0))],
        out_specs=[pl.BlockSpec((None, rows, B_W), lambda b, i: (b, i, 0)), st_spec],
        out_shape=[jax.ShapeDtypeStruct((b_, t, B_W), bf16),
                   jax.ShapeDtypeStruct((b_, B_HEADS, B_VAL_DIM, B_KEY_DIM), f32)],
        compiler_params=_cparams("parallel", "arbitrary"),
        name="gla",
    )(slab, s0_t, gain)


def _mla_body(q_ref, k_ref, v_ref, o_ref, m_sc, l_sc, acc_sc, *, causal, nk):
    i = pl.program_id(1)
    j = pl.program_id(2)
    tq, tk = q_ref.shape[0], k_ref.shape[0]
    scale = (C_NOPE_DIM + C_ROPE_DIM) ** -0.5

    @pl.when(j == 0)
    def _():
        m_sc[...] = jnp.full(m_sc.shape, NEG, f32)
        l_sc[...] = jnp.zeros(l_sc.shape, f32)
        acc_sc[...] = jnp.zeros(acc_sc.shape, f32)

    def step(diagonal):
        if diagonal:
            r = lax.broadcasted_iota(i32, (tq, tk), 0) // CHUNK
            c = lax.broadcasted_iota(i32, (tq, tk), 1) // CHUNK
            visible = c <= r
        for hd in range(C_HEADS):
            qk = slice(hd * C_PAD_DIM, (hd + 1) * C_PAD_DIM)
            s = _dot_nt(q_ref[:, qk], k_ref[:, qk]) * scale
            if diagonal:
                s = jnp.where(visible, s, NEG)
            m_old = m_sc[hd]
            m_new = jnp.maximum(m_old, jnp.max(s, axis=-1, keepdims=True))
            alpha = jnp.exp(m_old - m_new)
            p = jnp.exp(s - m_new)
            l_sc[hd] = alpha * l_sc[hd] + jnp.sum(p, axis=-1, keepdims=True)
            pv = jnp.dot(p.astype(bf16), v_ref[:, hd * C_V_DIM:(hd + 1) * C_V_DIM], preferred_element_type=f32)
            acc_sc[hd] = alpha * acc_sc[hd] + pv
            m_sc[hd] = m_new

    def finish():
        for hd in range(C_HEADS):
            o_ref[:, hd * C_V_DIM:(hd + 1) * C_V_DIM] = (acc_sc[hd] / l_sc[hd]).astype(bf16)

    if causal:
        pl.when(j < i)(functools.partial(step, False))

        @pl.when(j == i)
        def _():
            step(True)
            finish()
    else:
        step(False)
        pl.when(j == nk - 1)(finish)


def _mla(q, k, v, *, causal, tq, tk):
    b_, t_q, _ = q.shape
    t_k = k.shape[1]
    nq, nk = t_q // tq, t_k // tk
    kv_idx = (lambda b, i, j: (b, jnp.minimum(i, j), 0)) if causal else (lambda b, i, j: (b, j, 0))
    return pl.pallas_call(
        functools.partial(_mla_body, causal=causal, nk=nk),
        grid=(b_, nq, nk),
        in_specs=[pl.BlockSpec((None, tq, C_QW), lambda b, i, j: (b, i, 0)),
                  pl.BlockSpec((None, tk, C_QW), kv_idx),
                  pl.BlockSpec((None, tk, C_W), kv_idx)],
        out_specs=pl.BlockSpec((None, tq, C_W), lambda b, i, j: (b, i, 0)),
        out_shape=jax.ShapeDtypeStruct((b_, t_q, C_W), bf16),
        scratch_shapes=[pltpu.VMEM((C_HEADS, tq, 1), f32), pltpu.VMEM((C_HEADS, tq, 1), f32),
                        pltpu.VMEM((C_HEADS, tq, C_V_DIM), f32)],
        compiler_params=_cparams("parallel", "parallel", "arbitrary"),
        name="mla_causal" if causal else "mla_full",
    )(q, k, v)


def _outproj_body(a_ref, b_ref, c_ref, h_ref, w_ref, g2_ref, rw_ref, rb_ref, h_o, xn_o, lg_o):
    y = jnp.dot(a_ref[...], w_ref[0:A_W, :], preferred_element_type=f32)
    y += jnp.dot(b_ref[...], w_ref[A_W:A_W + B_W, :], preferred_element_type=f32)
    y += jnp.dot(c_ref[...], w_ref[A_W + B_W:, :], preferred_element_type=f32)
    h = h_ref[...] + y
    h_o[...] = h
    xn = _rms(h, g2_ref[...])
    xn_o[...] = xn
    lg_o[...] = jnp.dot(xn, rw_ref[...], preferred_element_type=f32,
                        precision=lax.Precision.HIGHEST) + rb_ref[...]


def _outproj(a_o, b_o, c_o, h, lw):
    n = h.shape[0]
    tm = ROW_TILE
    row = lambda w: pl.BlockSpec((tm, w), lambda i: (i, 0))
    const = lambda a: pl.BlockSpec(a.shape, lambda i: (0,) * a.ndim)
    consts = (lw['w_out'], lw['ln2_g'], lw['router_w'], lw['router_b'])
    return pl.pallas_call(
        _outproj_body,
        grid=(n // tm,),
        in_specs=[row(A_W), row(B_W), row(C_W), row(D_MODEL)] + [const(a) for a in consts],
        out_specs=[row(D_MODEL), row(D_MODEL), row(N_EXPERTS)],
        out_shape=[jax.ShapeDtypeStruct((n, D_MODEL), f32), jax.ShapeDtypeStruct((n, D_MODEL), f32),
                   jax.ShapeDtypeStruct((n, N_EXPERTS), f32)],
        compiler_params=_cparams("parallel"),
        name="outproj",
    )(a_o, b_o, c_o, h, *consts)


def _route_body(lg_ref, e_o, g_o, r_o, cnt_o):
    @pl.when(pl.program_id(0) == 0)
    def _():
        cnt_o[...] = jnp.zeros(cnt_o.shape, f32)

    tm = lg_ref.shape[0]
    lane = lax.broadcasted_iota(i32, (tm, N_EXPERTS), 1)
    slot = lax.broadcasted_iota(i32, (tm, TOP_K), 1)
    work = lg_ref[...]
    onehots, vals = [], []
    e_out = jnp.zeros((tm, TOP_K), i32)
    for k in range(TOP_K):
        m = jnp.max(work, axis=-1, keepdims=True)
        idx = jnp.min(jnp.where(work == m, lane, N_EXPERTS), axis=-1, keepdims=True)
        oh = lane == idx
        work = jnp.where(oh, -jnp.inf, work)
        onehots.append(oh)
        vals.append(m)
        e_out = jnp.where(slot == k, idx, e_out)
    ex = [jnp.exp(v - vals[0]) for v in vals]
    denom = ex[0] + ex[1] + ex[2] + ex[3]
    g_out = jnp.zeros((tm, TOP_K), f32)
    sel = jnp.zeros((tm, N_EXPERTS), f32)
    for k in range(TOP_K):
        g_out = jnp.where(slot == k, ex[k] / denom, g_out)
        sel = sel + onehots[k].astype(f32)
    r = lax.broadcasted_iota(i32, (tm, tm), 0)
    c = lax.broadcasted_iota(i32, (tm, tm), 1)
    before = (c < r).astype(bf16)
    rank = jnp.dot(before, sel.astype(bf16), preferred_element_type=f32) + cnt_o[...]
    r_out = jnp.zeros((tm, TOP_K), f32)
    for k in range(TOP_K):
        rk = jnp.sum(jnp.where(onehots[k], rank, 0.0), axis=-1, keepdims=True)
        r_out = jnp.where(slot == k, rk, r_out)
    e_o[...] = e_out
    g_o[...] = g_out
    r_o[...] = r_out.astype(i32)
    cnt_o[...] += jnp.sum(sel, axis=0, keepdims=True)


def _route(logits):
    n = logits.shape[0]
    tm = ROW_TILE
    row = lambda w: pl.BlockSpec((tm, w), lambda i: (i, 0))
    return pl.pallas_call(
        _route_body,
        grid=(n // tm,),
        in_specs=[row(N_EXPERTS)],
        out_specs=[row(TOP_K), row(TOP_K), row(TOP_K), pl.BlockSpec((1, N_EXPERTS), lambda i: (0, 0))],
        out_shape=[jax.ShapeDtypeStruct((n, TOP_K), i32), jax.ShapeDtypeStruct((n, TOP_K), f32),
                   jax.ShapeDtypeStruct((n, TOP_K), i32), jax.ShapeDtypeStruct((1, N_EXPERTS), f32)],
        compiler_params=_cparams("arbitrary"),
        name="route",
    )(logits)


def _row_copy(src_hbm, row, dst, slot, j, sem):
    return pltpu.make_async_copy(src_hbm.at[pl.ds(row, 1), :], dst.at[slot, pl.ds(j, 1), :], sem.at[slot])


def _experts_body(blk_e_ref, row_tok_ref, x_hbm, wg_ref, bg_ref, wu_ref, bu_ref, wd_ref, bd_ref, y_o,
                  xbuf, wbf, sem):
    b = pl.program_id(0)
    nb = pl.num_programs(0)
    bm = MOE_BLOCK

    def start_block(blk, slot):
        def issue(j, carry):
            _row_copy(x_hbm, row_tok_ref[blk * bm + j], xbuf, slot, j, sem).start()
            return carry
        lax.fori_loop(0, bm, issue, 0)

    def wait_block(slot):
        def wait(j, carry):
            _row_copy(x_hbm, 0, xbuf, slot, j, sem).wait()
            return carry
        lax.fori_loop(0, bm, wait, 0)

    slot = b % 2

    @pl.when(b == 0)
    def _():
        start_block(0, 0)

    @pl.when(b + 1 < nb)
    def _():
        start_block(b + 1, 1 - slot)

    prev_e = blk_e_ref[jnp.maximum(b - 1, 0)]

    @pl.when(jnp.logical_or(b == 0, blk_e_ref[b] != prev_e))
    def _():
        wbf[0] = wg_ref[...].astype(bf16)
        wbf[1] = wu_ref[...].astype(bf16)
        wbf[2] = wd_ref[...].astype(bf16)

    wait_block(slot)
    x = xbuf[slot].astype(bf16)
    g = jnp.dot(x, wbf[0], preferred_element_type=f32) + bg_ref[...]
    u = jnp.dot(x, wbf[1], preferred_element_type=f32) + bu_ref[...]
    g = jnp.minimum(g, SWIGLU_LIMIT)
    u = jnp.clip(u, -SWIGLU_LIMIT, SWIGLU_LIMIT)
    hdn = (u + 1.0) * (g * _sigmoid(g * SWIGLU_ALPHA))
    y_o[...] = jnp.dot(hdn.astype(bf16), wbf[2], preferred_element_type=f32) + bd_ref[...]


def _experts(x, blk_e, row_tok, lw):
    nblk = blk_e.shape[0]
    bm = MOE_BLOCK
    wspec = lambda: pl.BlockSpec((None, D_MODEL, D_EXPERT), lambda b, be, rt: (be[b], 0, 0))
    bspec = lambda: pl.BlockSpec((None, 1, D_EXPERT), lambda b, be, rt: (be[b], 0, 0))
    grid_spec = pltpu.PrefetchScalarGridSpec(
        num_scalar_prefetch=2,
        grid=(nblk,),
        in_specs=[pl.BlockSpec(memory_space=pl.ANY), wspec(), bspec(), wspec(), bspec(), wspec(), bspec()],
        out_specs=pl.BlockSpec((bm, D_MODEL), lambda b, be, rt: (b, 0)),
        scratch_shapes=[pltpu.VMEM((2, bm, D_MODEL), f32), pltpu.VMEM((3, D_MODEL, D_EXPERT), bf16),
                        pltpu.SemaphoreType.DMA((2,))],
    )
    return pl.pallas_call(
        _experts_body,
        grid_spec=grid_spec,
        out_shape=jax.ShapeDtypeStruct((nblk * bm, D_MODEL), f32),
        compiler_params=_cparams("arbitrary"),
        name="experts",
    )(blk_e, row_tok, x, lw['e_w_gate'], lw['e_b_gate'], lw['e_w_up'], lw['e_b_up'],
      lw['e_w_down'], lw['e_b_down'])


def _combine_body(dest_ref, y_hbm, h_ref, g_ref, o_ref, ybuf, sem):
    i = pl.program_id(0)
    nt = pl.num_programs(0)
    tt = COMBINE_TILE

    def copy(tile, slot, k, j):
        return pltpu.make_async_copy(
            y_hbm.at[pl.ds(dest_ref[(tile * tt + j) * TOP_K + k], 1), :],
            ybuf.at[slot, k, pl.ds(j, 1), :], sem.at[slot])

    def start_tile(tile, slot):
        def issue(j, carry):
            for k in range(TOP_K):
                copy(tile, slot, k, j).start()
            return carry
        lax.fori_loop(0, tt, issue, 0)

    def wait_tile(slot):
        def wait(j, carry):
            for k in range(TOP_K):
                copy(0, slot, k, j).wait()
            return carry
        lax.fori_loop(0, tt, wait, 0)

    slot = i % 2

    @pl.when(i == 0)
    def _():
        start_tile(0, 0)

    @pl.when(i + 1 < nt)
    def _():
        start_tile(i + 1, 1 - slot)

    wait_tile(slot)
    g = g_ref[...]
    acc = h_ref[...]
    for k in range(TOP_K):
        acc = acc + ybuf[slot, k] * g[:, k:k + 1]
    o_ref[...] = acc


def _combine(y_rows, dest, h, gates):
    n = h.shape[0]
    tt = COMBINE_TILE
    grid_spec = pltpu.PrefetchScalarGridSpec(
        num_scalar_prefetch=1,
        grid=(n // tt,),
        in_specs=[pl.BlockSpec(memory_space=pl.ANY),
                  pl.BlockSpec((tt, D_MODEL), lambda i, d: (i, 0)),
                  pl.BlockSpec((tt, TOP_K), lambda i, d: (i, 0))],
        out_specs=pl.BlockSpec((tt, D_MODEL), lambda i, d: (i, 0)),
        scratch_shapes=[pltpu.VMEM((2, TOP_K, tt, D_MODEL), f32), pltpu.SemaphoreType.DMA((2,))],
    )
    return pl.pallas_call(
        _combine_body,
        grid_spec=grid_spec,
        out_shape=jax.ShapeDtypeStruct((n, D_MODEL), f32),
        compiler_params=_cparams("arbitrary"),
        name="combine",
    )(dest, y_rows, h, gates)


def _moe(h, xn, logits, lw):
    n = h.shape[0]
    bm = MOE_BLOCK
    top_e, gates, rank, counts = _route(logits)
    counts = counts[0].astype(i32)
    padded = ((counts + bm - 1) // bm) * bm
    pend = jnp.cumsum(padded)
    pstart = pend - padded
    dest = pstart[top_e] + rank
    nblk = (n * TOP_K) // bm + N_EXPERTS
    tok = jnp.broadcast_to(jnp.arange(n, dtype=i32)[:, None], (n, TOP_K))
    row_tok = jnp.zeros((nblk * bm,), i32).at[dest.reshape(-1)].set(tok.reshape(-1))
    blk_e = jnp.minimum(jnp.searchsorted(pend, jnp.arange(nblk, dtype=i32) * bm, side='right'),
                        N_EXPERTS - 1).astype(i32)
    y_rows = _experts(xn, blk_e, row_tok, lw)
    return _combine(y_rows, dest.reshape(-1), h, gates)


def _rope_tables(pos, tile_rows):
    half = C_ROPE_DIM // 2
    inv = ROPE_THETA ** (-jnp.arange(half, dtype=f32) / half)
    ang = pos.astype(f32)[:, None] * inv[None, :]
    reps = max(1, tile_rows // ang.shape[0])
    return jnp.tile(jnp.cos(ang), (reps, 1)), jnp.tile(jnp.sin(ang), (reps, 1))


def _band_bias(rel_bias):
    rel = (A_PAST + jnp.arange(CHUNK))[:, None] - jnp.arange(A_BAND)[None, :]
    idx = jnp.clip(rel, -A_REL_CLIP, A_REL_CLIP) + A_REL_CLIP
    return rel_bias[:, idx].astype(f32)


def _layer_weights(l, p):
    w = p['w_in'][l]
    seg = lambda o, n: w[:, o:o + n]
    zc = lambda n: jnp.zeros((D_MODEL, n), f32)
    o_aq, o_ak, o_av, o_bq, o_bk, o_bv = 0, 384, 768, 1152, 1408, 1664
    o_blr, o_bg, o_cq, o_ckv, o_ckr = 2048, 2064, 2448, 2640, 2768
    w_in = jnp.concatenate([
        seg(o_aq, 384), seg(o_ak, 384), seg(o_av, 384), seg(o_bq, 256), seg(o_bk, 256), seg(o_bv, 384),
        seg(o_bg, 384), seg(o_cq, C_Q_RANK), zc(P_CKV - P_CQ - C_Q_RANK), seg(o_ckv, 128),
        seg(o_ckr, 32), seg(o_blr, 16), zc(IN_P - P_BLR - B_GATE_RANK)], axis=1).astype(bf16)
    hq = C_NOPE_DIM + C_ROPE_DIM
    wuq = p['c_w_uq'][l].reshape(C_Q_RANK, C_HEADS, hq)
    wuq = jnp.pad(wuq, ((0, 0), (0, 0), (0, C_PAD_DIM - hq))).reshape(C_Q_RANK, C_QW).astype(bf16)
    wukv = p['c_w_ukv'][l].reshape(C_KV_RANK, C_HEADS, C_NOPE_DIM + C_V_DIM)
    wukv = jnp.concatenate([wukv[:, :, :C_NOPE_DIM].reshape(C_KV_RANK, -1),
                            wukv[:, :, C_NOPE_DIM:].reshape(C_KV_RANK, -1)], axis=1).astype(bf16)
    r2 = lambda a: a.reshape(1, -1)
    return {
        'ln1_g': r2(p['ln1_g'][l]), 'ln2_g': r2(p['ln2_g'][l]), 'w_in': w_in,
        'a_q_g': r2(p['a_q_g'][l]), 'a_k_g': r2(p['a_k_g'][l]), 'bias': _band_bias(p['a_rel_bias'][l]),
        'b_gate_w2': p['b_gate_w2'][l], 'b_gate_b': r2(p['b_gate_b'][l]), 'b_out_g': r2(p['b_out_g'][l]),
        'c_qa_g': r2(p['c_qa_g'][l]), 'c_w_uq': wuq, 'c_kva_g': r2(p['c_kva_g'][l]), 'c_w_ukv': wukv,
        'c_qn_g': r2(p['c_qn_g'][l]), 'c_qr_g': r2(p['c_qr_g'][l]), 'c_kn_g': r2(p['c_kn_g'][l]),
        'c_kr_g': r2(p['c_kr_g'][l]), 'w_out': p['w_out'][l].astype(bf16),
        'router_w': p['router_w'][l], 'router_b': r2(p['router_b'][l]),
        'e_w_gate': p['e_w_gate'][l], 'e_b_gate': p['e_b_gate'][l][:, None, :],
        'e_w_up': p['e_w_up'][l], 'e_b_up': p['e_b_up'][l][:, None, :],
        'e_w_down': p['e_w_down'][l], 'e_b_down': p['e_b_down'][l][:, None, :],
    }


def _token_mix(h, b_, t, lw, cos, sin, past):
    aq, ak, av, akb, avb, slab, cq, ckv, ckr = _proj(h, lw, cos, sin)
    r3 = lambda a: a.reshape(b_, t, a.shape[-1])
    ck, cv = _mla_keys(ckv, ckr, lw)
    if past is None:
        a_o = _band(r3(aq), r3(akb), r3(avb), None, None, lw['bias'])
        s0_t = jnp.zeros((b_, B_HEADS, B_VAL_DIM, B_KEY_DIM), f32)
        c_o = _mla(r3(cq), r3(ck), r3(cv), causal=True, tq=ROW_TILE, tk=ROW_TILE)
    else:
        a_k, a_v, b_s, c_kv, c_kr = past
        lp = a_k.shape[1]
        a_o = _band(r3(aq), r3(akb), r3(avb), a_k.reshape(b_, lp, A_W).astype(bf16),
                    a_v.reshape(b_, lp, A_W).astype(bf16), lw['bias'])
        s0_t = jnp.swapaxes(b_s, -1, -2)
        ck_p, cv_p = _mla_keys(c_kv.reshape(-1, C_KV_RANK), c_kr.reshape(-1, C_ROPE_DIM), lw)
        k_all = jnp.concatenate([ck_p.reshape(b_, -1, C_QW), r3(ck)], axis=1)
        v_all = jnp.concatenate([cv_p.reshape(b_, -1, C_W), r3(cv)], axis=1)
        c_o = _mla(r3(cq), k_all, v_all, causal=False, tq=t, tk=k_all.shape[1])
    b_o, s_t = _gla(r3(slab), s0_t, lw['b_out_g'])
    flat = lambda a: a.reshape(b_ * t, a.shape[-1])
    h2, xn2, logits = _outproj(flat(a_o), flat(b_o), flat(c_o), h, lw)
    la = min(A_PAST, t)
    state = (r3(ak)[:, t - la:].reshape(b_, la, A_HEADS, A_HEAD_DIM),
             r3(av)[:, t - la:].reshape(b_, la, A_HEADS, A_HEAD_DIM),
             jnp.swapaxes(s_t, -1, -2), r3(ckv), r3(ckr))
    return h2, xn2, logits, state


def kernel(x_prompt, x_sample, cache_a_k, cache_a_v, state_b_s, cache_c_kv, cache_c_kr, ln1_g, ln2_g, w_in, a_q_g, a_k_g, a_rel_bias, b_gate_w2, b_gate_b, b_out_g, c_qa_g, c_w_uq, c_kva_g, c_w_ukv, c_qn_g, c_qr_g, c_kn_g, c_kr_g, w_out, router_w, router_b, e_w_gate, e_b_gate, e_w_up, e_b_up, e_w_down, e_b_down):
    params = dict(ln1_g=ln1_g, ln2_g=ln2_g, w_in=w_in, a_q_g=a_q_g, a_k_g=a_k_g, a_rel_bias=a_rel_bias,
                  b_gate_w2=b_gate_w2, b_gate_b=b_gate_b, b_out_g=b_out_g, c_qa_g=c_qa_g, c_w_uq=c_w_uq,
                  c_kva_g=c_kva_g, c_w_ukv=c_w_ukv, c_qn_g=c_qn_g, c_qr_g=c_qr_g, c_kn_g=c_kn_g,
                  c_kr_g=c_kr_g, w_out=w_out, router_w=router_w, router_b=router_b, e_w_gate=e_w_gate,
                  e_b_gate=e_b_gate, e_w_up=e_w_up, e_b_up=e_b_up, e_w_down=e_w_down, e_b_down=e_b_down)
    bp, tp, _ = x_prompt.shape
    bs, ts, _ = x_sample.shape
    depth = w_in.shape[0]
    n_p = bp * tp
    cos_p, sin_p = _rope_tables(jnp.arange(tp), ROW_TILE)
    cos_s, sin_s = _rope_tables(PAST_LEN + jnp.arange(ts), ROW_TILE)
    h_p = x_prompt.reshape(n_p, D_MODEL)
    h_s = x_sample.reshape(bs * ts, D_MODEL)
    st_p, st_s = [], []
    for l in range(depth):
        lw = _layer_weights(l, params)
        h_p, xn_p, lg_p, s_p = _token_mix(h_p, bp, tp, lw, cos_p, sin_p, None)
        h_s, xn_s, lg_s, s_s = _token_mix(h_s, bs, ts, lw, cos_s, sin_s,
                                          (cache_a_k[l], cache_a_v[l], state_b_s[l], cache_c_kv[l], cache_c_kr[l]))
        h = _moe(jnp.concatenate([h_p, h_s]), jnp.concatenate([xn_p, xn_s]), jnp.concatenate([lg_p, lg_s]), lw)
        h_p, h_s = h[:n_p], h[n_p:]
        st_p.append(s_p)
        st_s.append(s_s)
    stack = lambda sts, i: jnp.stack([s[i] for s in sts])
    return (h_p.reshape(bp, tp, D_MODEL), h_s.reshape(bs, ts, D_MODEL),
            stack(st_p, 0), stack(st_p, 1), stack(st_p, 2), stack(st_p, 3), stack(st_p, 4),
            stack(st_s, 0), stack(st_s, 1), stack(st_s, 2), stack(st_s, 3), stack(st_s, 4))
```

```python
import functools
import math

import jax
import jax.numpy as jnp
from jax import lax
from jax.experimental import pallas as pl
from jax.experimental.pallas import tpu as pltpu

f32 = jnp.float32
bf16 = jnp.bfloat16
i32 = jnp.int32

D_MODEL = 1024
PAST_LEN = 2048
CHUNK = 64
A_HEADS = 6
A_HEAD_DIM = 64
A_BAND_CHUNKS = 8
A_PAST = A_BAND_CHUNKS * CHUNK
A_BAND = A_PAST + CHUNK
A_REL_CLIP = 128
B_HEADS = 4
B_KEY_DIM = 64
B_VAL_DIM = 96
B_GATE_RANK = 16
B_GATE_TAU = 16.0
C_HEADS = 4
C_Q_RANK = 192
C_KV_RANK = 128
C_NOPE_DIM = 64
C_ROPE_DIM = 32
C_V_DIM = 64
ROPE_THETA = 10000.0
N_EXPERTS = 32
TOP_K = 4
D_EXPERT = 1024
SWIGLU_LIMIT = 7.0
SWIGLU_ALPHA = 1.702
EPS = 1e-6
NEG = -1e30
LOG2E = 1.0 / math.log(2.0)

LANES = 128
C_PAD_DIM = LANES
V_EXT = LANES
A_QSCALE = A_HEAD_DIM ** -0.5 * LOG2E
C_QSCALE = (C_NOPE_DIM + C_ROPE_DIM) ** -0.5 * LOG2E

A_W = A_HEADS * A_HEAD_DIM
B_W = B_HEADS * B_VAL_DIM
C_W = C_HEADS * C_V_DIM
B_KW = B_HEADS * B_KEY_DIM
C_QW = C_HEADS * C_PAD_DIM
A_VW = A_HEADS * V_EXT
C_VW = C_HEADS * V_EXT

P_AQ, P_AK, P_AV = 0, 384, 768
P_BQ, P_BK, P_BV, P_BG = 1152, 1408, 1664, 2048
P_CQ, P_CKV, P_CKR, P_BLR = 2432, 2688, 2816, 2848
IN_P = 2944
S_BQ, S_BK, S_BV, S_LA, S_BG, S_W = 0, 256, 512, 896, 1152, 1536

ROW_TILE = 512
BAND_ROWS = 256
GLA_CHUNKS = 4
MLA_TILE = 512
MOE_BLOCK = 256
COMBINE_TILE = 128
VMEM_LIMIT = 56 * 1024 * 1024


def _cparams(*sem):
    return pltpu.CompilerParams(dimension_semantics=sem, vmem_limit_bytes=VMEM_LIMIT)


def _rms(x, g):
    return x * lax.rsqrt(jnp.mean(x * x, axis=-1, keepdims=True) + EPS) * g


def _rope(x, cos, sin):
    half = C_ROPE_DIM // 2
    x1, x2 = x[:, :half], x[:, half:]
    return jnp.concatenate([x1 * cos - x2 * sin, x2 * cos + x1 * sin], axis=-1)


def _dot(a, b):
    return jnp.dot(a, b, preferred_element_type=f32)


def _dot_nt(a, b):
    return lax.dot_general(a, b, (((1,), (1,)), ((), ())), preferred_element_type=f32)


def _dot_tn(a, b):
    return lax.dot_general(a, b, (((0,), (0,)), ((), ())), preferred_element_type=f32)


def _sigmoid(x):
    return 1.0 / (1.0 + jnp.exp(-x))


def _ones_column(rows):
    lane = lax.broadcasted_iota(i32, (rows, V_EXT - C_V_DIM), 1)
    return jnp.where(lane == 0, 1.0, 0.0).astype(bf16)


def _softmax_pv(s, v_blocks, widths):
    p = jnp.exp2(s - jnp.max(s, axis=-1, keepdims=True)).astype(bf16)
    acc, lo = None, 0
    for v, w in zip(v_blocks, widths):
        part = _dot(p[:, lo:lo + w], v)
        acc = part if acc is None else acc + part
        lo += w
    return acc[:, :C_V_DIM] / acc[:, C_V_DIM:C_V_DIM + 1]


def _proj_body(x_ref, g1_ref, w_ref, aqg_ref, akg_ref, w2_ref, gb_ref, qag_ref, wuq_ref,
               qng_ref, qrg_ref, kvag_ref, krg_ref, cos_ref, sin_ref,
               aq_o, ak_o, av_o, akb_o, avb_o, b_o, cq_o, ckv_o, ckr_o, h_sc):
    tm = x_ref.shape[0]
    xn = _rms(x_ref[...], g1_ref[...])
    h_sc[...] = _dot(xn.astype(bf16), w_ref[...])
    cos = cos_ref[...]
    sin = sin_ref[...]
    ones = _ones_column(tm)
    for hd in range(A_HEADS):
        lo = hd * A_HEAD_DIM
        q = _rms(h_sc[:, P_AQ + lo:P_AQ + lo + A_HEAD_DIM], aqg_ref[...])
        aq_o[:, lo:lo + A_HEAD_DIM] = (q * A_QSCALE).astype(bf16)
        k = _rms(h_sc[:, P_AK + lo:P_AK + lo + A_HEAD_DIM], akg_ref[...])
        ak_o[:, lo:lo + A_HEAD_DIM] = k
        akb_o[:, lo:lo + A_HEAD_DIM] = k.astype(bf16)
        v = h_sc[:, P_AV + lo:P_AV + lo + A_HEAD_DIM]
        avb_o[:, hd * V_EXT:hd * V_EXT + A_HEAD_DIM] = v.astype(bf16)
        avb_o[:, hd * V_EXT + A_HEAD_DIM:(hd + 1) * V_EXT] = ones
    av_o[...] = h_sc[:, P_AV:P_AV + A_W]
    b_o[:, S_BQ:S_LA] = h_sc[:, P_BQ:P_BG]
    z = jnp.dot(h_sc[:, P_BLR:P_BLR + B_GATE_RANK], w2_ref[...], preferred_element_type=f32,
                precision=lax.Precision.HIGHEST) + gb_ref[...]
    log_sig = jnp.minimum(z, 0.0) - jnp.log1p(jnp.exp(-jnp.abs(z)))
    b_o[:, S_LA:S_BG] = log_sig * (1.0 / B_GATE_TAU)
    b_o[:, S_BG:S_W] = h_sc[:, P_BG:P_BG + B_W]
    cql = _rms(h_sc[:, P_CQ:P_CQ + C_Q_RANK], qag_ref[...])
    cq = _dot(cql.astype(bf16), wuq_ref[...])
    zeros = jnp.zeros((tm, C_PAD_DIM - C_NOPE_DIM - C_ROPE_DIM), bf16)
    for hd in range(C_HEADS):
        lo = hd * C_PAD_DIM
        nope = _rms(cq[:, lo:lo + C_NOPE_DIM], qng_ref[...])
        rot = _rope(_rms(cq[:, lo + C_NOPE_DIM:lo + C_NOPE_DIM + C_ROPE_DIM], qrg_ref[...]), cos, sin)
        cq_o[:, lo:lo + C_NOPE_DIM] = (nope * C_QSCALE).astype(bf16)
        cq_o[:, lo + C_NOPE_DIM:lo + C_NOPE_DIM + C_ROPE_DIM] = (rot * C_QSCALE).astype(bf16)
        cq_o[:, lo + C_NOPE_DIM + C_ROPE_DIM:lo + C_PAD_DIM] = zeros
    ckv_o[...] = _rms(h_sc[:, P_CKV:P_CKV + C_KV_RANK], kvag_ref[...])
    ckr_o[...] = _rope(_rms(h_sc[:, P_CKR:P_CKR + C_ROPE_DIM], krg_ref[...]), cos, sin)


def _proj(h_all, row0, n, lw, cos, sin):
    tm = ROW_TILE
    blk0 = row0 // tm
    npos = cos.shape[0] // tm
    row = lambda w: pl.BlockSpec((tm, w), lambda i: (i, 0))
    const = lambda a: pl.BlockSpec(a.shape, lambda i: (0,) * a.ndim)
    pos = pl.BlockSpec((tm, C_ROPE_DIM // 2), lambda i: (i % npos, 0))
    consts = (lw['ln1_g'], lw['w_in'], lw['a_q_g'], lw['a_k_g'], lw['b_gate_w2'], lw['b_gate_b'],
              lw['c_qa_g'], lw['c_w_uq'], lw['c_qn_g'], lw['c_qr_g'], lw['c_kva_g'], lw['c_kr_g'])
    out_w = ((A_W, bf16), (A_W, f32), (A_W, f32), (A_W, bf16), (A_VW, bf16), (S_W, f32),
             (C_QW, bf16), (C_KV_RANK, f32), (C_ROPE_DIM, f32))
    return pl.pallas_call(
        _proj_body,
        grid=(n // tm,),
        in_specs=[pl.BlockSpec((tm, D_MODEL), lambda i: (i + blk0, 0))] + [const(a) for a in consts] + [pos, pos],
        out_specs=[row(w) for w, _ in out_w],
        out_shape=[jax.ShapeDtypeStruct((n, w), dt) for w, dt in out_w],
        scratch_shapes=[pltpu.VMEM((tm, IN_P), f32)],
        compiler_params=_cparams("parallel"),
        name="proj",
    )(h_all, *consts, cos, sin)


def _expand_keys(kv, kr, kn_gain, k_dst, v_dst, rows):
    n = kv.shape[0]
    zeros = jnp.zeros((n, C_PAD_DIM - C_NOPE_DIM - C_ROPE_DIM), bf16)
    ones = _ones_column(n)
    for hd in range(C_HEADS):
        lo = hd * C_PAD_DIM
        kn = _rms(kv[:, hd * C_NOPE_DIM:(hd + 1) * C_NOPE_DIM], kn_gain)
        k_dst[rows, lo:lo + C_NOPE_DIM] = kn.astype(bf16)
        k_dst[rows, lo + C_NOPE_DIM:lo + C_NOPE_DIM + C_ROPE_DIM] = kr
        k_dst[rows, lo + C_NOPE_DIM + C_ROPE_DIM:lo + C_PAD_DIM] = zeros
        v = kv[:, C_HEADS * C_NOPE_DIM + hd * C_V_DIM:C_HEADS * C_NOPE_DIM + (hd + 1) * C_V_DIM]
        v_dst[rows, hd * V_EXT:hd * V_EXT + C_V_DIM] = v.astype(bf16)
        v_dst[rows, hd * V_EXT + C_V_DIM:(hd + 1) * V_EXT] = ones


def _mla_keys_body(ckv_ref, ckr_ref, w_ref, kng_ref, ck_o, cv_o):
    kv = _dot(ckv_ref[...].astype(bf16), w_ref[...])
    _expand_keys(kv, ckr_ref[...].astype(bf16), kng_ref[...], ck_o, cv_o, slice(None))


def _mla_keys(ckv, ckr, lw):
    n = ckv.shape[0]
    tm = ROW_TILE
    row = lambda w: pl.BlockSpec((tm, w), lambda i: (i, 0))
    const = lambda a: pl.BlockSpec(a.shape, lambda i: (0,) * a.ndim)
    return pl.pallas_call(
        _mla_keys_body,
        grid=(n // tm,),
        in_specs=[row(C_KV_RANK), row(C_ROPE_DIM), const(lw['c_w_ukv']), const(lw['c_kn_g'])],
        out_specs=[row(C_QW), row(C_VW)],
        out_shape=[jax.ShapeDtypeStruct((n, C_QW), bf16), jax.ShapeDtypeStruct((n, C_VW), bf16)],
        compiler_params=_cparams("parallel"),
        name="mla_keys",
    )(ckv, ckr, lw['c_w_ukv'], lw['c_kn_g'])


def _band_body(q_ref, *refs, n_kb, past_from_seq):
    k_refs, v_refs = refs[:n_kb], refs[n_kb:2 * n_kb]
    bias_ref, o_ref = refs[2 * n_kb], refs[2 * n_kb + 1]
    rows = q_ref.shape[0]
    widths = [k.shape[0] for k in k_refs]
    if past_from_seq:
        col = lax.broadcasted_iota(i32, (rows, sum(widths)), 1)
        exists = col + (pl.program_id(1) * rows - A_PAST) >= 0
    for hd in range(A_HEADS):
        qs = slice(hd * A_HEAD_DIM, (hd + 1) * A_HEAD_DIM)
        vs = slice(hd * V_EXT, (hd + 1) * V_EXT)
        q = q_ref[:, qs]
        s = jnp.concatenate([_dot_nt(q, k[:, qs]) for k in k_refs], axis=1) + bias_ref[hd]
        if past_from_seq:
            s = jnp.where(exists, s, NEG)
        o_ref[:, qs] = _softmax_pv(s, [v[:, vs] for v in v_refs], widths).astype(bf16)


def _band(q, k, v, k_past, v_past, bias):
    b_, t, _ = q.shape
    from_seq = k_past is None
    rows = BAND_ROWS if from_seq else t
    n_back = A_PAST // rows if from_seq else 1
    cur = lambda w: pl.BlockSpec((None, rows, w), lambda b, i: (b, i, 0))
    if from_seq:
        back = lambda w, d: pl.BlockSpec((None, rows, w), lambda b, i: (b, jnp.maximum(i - d, 0), 0))
        k_specs = [back(A_W, d) for d in range(n_back, 0, -1)] + [cur(A_W)]
        v_specs = [back(A_VW, d) for d in range(n_back, 0, -1)] + [cur(A_VW)]
        k_args, v_args = [k] * (n_back + 1), [v] * (n_back + 1)
    else:
        past = lambda w: pl.BlockSpec((None, A_PAST, w), lambda b, i: (b, 0, 0))
        k_specs, v_specs = [past(A_W), cur(A_W)], [past(A_VW), cur(A_VW)]
        k_args, v_args = [k_past, k], [v_past, v]
    return pl.pallas_call(
        functools.partial(_band_body, n_kb=len(k_specs), past_from_seq=from_seq),
        grid=(b_, t // rows),
        in_specs=[cur(A_W)] + k_specs + v_specs + [pl.BlockSpec(bias.shape, lambda b, i: (0, 0, 0))],
        out_specs=cur(A_W),
        out_shape=jax.ShapeDtypeStruct((b_, t, A_W), bf16),
        compiler_params=_cparams("parallel", "parallel"),
        name="band",
    )(q, *k_args, *v_args, bias)


def _gla_body(b_ref, s0_ref, gain_ref, o_ref, st_ref, *, chunks):
    @pl.when(pl.program_id(1) == 0)
    def _():
        st_ref[...] = s0_ref[...]

    rows = chunks * CHUNK
    r = lax.broadcasted_iota(i32, (rows, rows), 0)
    c = lax.broadcasted_iota(i32, (rows, rows), 1)
    causal = jnp.logical_and(r >= c, r // CHUNK == c // CHUNK)
    tril = (lax.broadcasted_iota(i32, (CHUNK, CHUNK), 0) >= lax.broadcasted_iota(i32, (CHUNK, CHUNK), 1)).astype(f32)
    bcum, btot = [], []
    for ch in range(chunks):
        la = b_ref[ch * CHUNK:(ch + 1) * CHUNK, S_LA:S_LA + B_KW]
        cum = jnp.dot(tril, la, preferred_element_type=f32, precision=lax.Precision.HIGHEST)
        bcum.append(cum)
        btot.append(jnp.broadcast_to(cum[CHUNK - 1:CHUNK, :], cum.shape))
    bcum = jnp.concatenate(bcum, axis=0)
    btot = jnp.concatenate(btot, axis=0)
    k = b_ref[:, S_BK:S_BK + B_KW]
    qt = b_ref[:, S_BQ:S_BQ + B_KW] * (B_KEY_DIM ** -0.5) * jnp.exp(bcum)
    kt = k * jnp.exp(-bcum)
    kd = k * jnp.exp(btot - bcum)
    decay = jnp.exp(btot)
    for hd in range(B_HEADS):
        ks = slice(hd * B_KEY_DIM, (hd + 1) * B_KEY_DIM)
        vs = slice(hd * B_VAL_DIM, (hd + 1) * B_VAL_DIM)
        v = b_ref[:, S_BV + vs.start:S_BV + vs.stop]
        att = jnp.where(causal, _dot_nt(qt[:, ks], kt[:, ks]), 0.0)
        o_intra = _dot(att, v)
        st = st_ref[hd]
        outs = []
        for ch in range(chunks):
            rs = slice(ch * CHUNK, (ch + 1) * CHUNK)
            outs.append(o_intra[rs] + _dot_nt(qt[rs, ks], st))
            st = st * decay[ch * CHUNK:ch * CHUNK + 1, ks] + _dot_tn(v[rs], kd[rs, ks])
        st_ref[hd] = st
        g = b_ref[:, S_BG + vs.start:S_BG + vs.stop]
        o = _rms(jnp.concatenate(outs, axis=0), gain_ref[...]) * (g * _sigmoid(g))
        o_ref[:, vs] = o.astype(bf16)


def _gla(slab, s0_t, gain):
    b_, t, _ = slab.shape
    chunks = min(GLA_CHUNKS, t // CHUNK)
    rows = chunks * CHUNK
    st_spec = pl.BlockSpec((None, B_HEADS, B_VAL_DIM, B_KEY_DIM), lambda b, i: (b, 0, 0, 0))
    return pl.pallas_call(
        functools.partial(_gla_body, chunks=chunks),
        grid=(b_, t // rows),
        in_specs=[pl.BlockSpec((None, rows, S_W), lambda b, i: (b, i, 0)), st_spec,
                  pl.BlockSpec(gain.shape, lambda b, i: (0, 0))],
        out_specs=[pl.BlockSpec((None, rows, B_W), lambda b, i: (b, i, 0)), st_spec],
        out_shape=[jax.ShapeDtypeStruct((b_, t, B_W), bf16),
                   jax.ShapeDtypeStruct((b_, B_HEADS, B_VAL_DIM, B_KEY_DIM), f32)],
        compiler_params=_cparams("parallel", "arbitrary"),
        name="gla",
    )(slab, s0_t, gain)


def _mla_causal_body(q_ref, k_ref, v_ref, o_ref, m_sc, acc_sc):
    i = pl.program_id(1)
    t = q_ref.shape[0]
    m_sc[...] = jnp.full(m_sc.shape, NEG, f32)
    acc_sc[...] = jnp.zeros(acc_sc.shape, f32)

    def tile(j, diagonal):
        rows = pl.ds(pl.multiple_of(j * t, t), t)
        if diagonal:
            visible = (lax.broadcasted_iota(i32, (t, t), 1) // CHUNK) <= (lax.broadcasted_iota(i32, (t, t), 0) // CHUNK)
        for hd in range(C_HEADS):
            hs = slice(hd * C_PAD_DIM, (hd + 1) * C_PAD_DIM)
            s = _dot_nt(q_ref[:, hs], k_ref[rows, hs])
            if diagonal:
                s = jnp.where(visible, s, NEG)
            m_old = m_sc[hd]
            m_new = jnp.maximum(m_old, jnp.max(s, axis=-1, keepdims=True))
            p = jnp.exp2(s - jnp.concatenate([m_new] * (t // LANES), axis=1))
            acc_sc[hd] = jnp.exp2(m_old - m_new) * acc_sc[hd] + _dot(p.astype(bf16), v_ref[rows, hs])
            m_sc[hd] = m_new

    def earlier(j, carry):
        tile(j, False)
        return carry

    lax.fori_loop(0, i, earlier, 0)
    tile(i, True)
    for hd in range(C_HEADS):
        acc = acc_sc[hd]
        o_ref[:, hd * C_V_DIM:(hd + 1) * C_V_DIM] = (acc[:, :C_V_DIM] / acc[:, C_V_DIM:C_V_DIM + 1]).astype(bf16)


def _mla_causal(q, k, v):
    b_, t, _ = q.shape
    tq = MLA_TILE
    whole = lambda w: pl.BlockSpec((None, t, w), lambda b, i: (b, 0, 0))
    return pl.pallas_call(
        _mla_causal_body,
        grid=(b_, t // tq),
        in_specs=[pl.BlockSpec((None, tq, C_QW), lambda b, i: (b, i, 0)), whole(C_QW), whole(C_VW)],
        out_specs=pl.BlockSpec((None, tq, C_W), lambda b, i: (b, i, 0)),
        out_shape=jax.ShapeDtypeStruct((b_, t, C_W), bf16),
        scratch_shapes=[pltpu.VMEM((C_HEADS, tq, LANES), f32), pltpu.VMEM((C_HEADS, tq, V_EXT), f32)],
        compiler_params=_cparams("parallel", "arbitrary"),
        name="mla_causal",
    )(q, k, v)


def _mla_sample_body(q_ref, ckv_ref, ckr_ref, kn_ref, vn_ref, w_ref, kng_ref, o_ref, k_sc, v_sc):
    past = ckv_ref.shape[0]
    kv = _dot(ckv_ref[...].astype(bf16), w_ref[...])
    _expand_keys(kv, ckr_ref[...].astype(bf16), kng_ref[...], k_sc, v_sc, slice(0, past))
    k_sc[past:, :] = kn_ref[...]
    v_sc[past:, :] = vn_ref[...]
    total = k_sc.shape[0]
    for hd in range(C_HEADS):
        hs = slice(hd * C_PAD_DIM, (hd + 1) * C_PAD_DIM)
        s = _dot_nt(q_ref[:, hs], k_sc[:, hs])
        o_ref[:, hd * C_V_DIM:(hd + 1) * C_V_DIM] = _softmax_pv(s, [v_sc[:, hs]], [total]).astype(bf16)


def _mla_sample(q, c_kv, c_kr, k_new, v_new, lw):
    b_, s_len, _ = q.shape
    past = c_kv.shape[1]
    per_b = lambda r, w: pl.BlockSpec((None, r, w), lambda b: (b, 0, 0))
    const = lambda a: pl.BlockSpec(a.shape, lambda b: (0,) * a.ndim)
    return pl.pallas_call(
        _mla_sample_body,
        grid=(b_,),
        in_specs=[per_b(s_len, C_QW), per_b(past, C_KV_RANK), per_b(past, C_ROPE_DIM), per_b(s_len, C_QW),
                  per_b(s_len, C_VW), const(lw['c_w_ukv']), const(lw['c_kn_g'])],
        out_specs=per_b(s_len, C_W),
        out_shape=jax.ShapeDtypeStruct((b_, s_len, C_W), bf16),
        scratch_shapes=[pltpu.VMEM((past + s_len, C_QW), bf16), pltpu.VMEM((past + s_len, C_VW), bf16)],
        compiler_params=_cparams("parallel"),
        name="mla_sample",
    )(q, c_kv, c_kr, k_new, v_new, lw['c_w_ukv'], lw['c_kn_g'])


def _outproj_body(ap_ref, bp_ref, cp_ref, as_ref, bs_ref, cs_ref, h_ref, w_ref, g2_ref, rw_ref, rb_ref,
                  h_o, xn_o, lg_o, *, prompt_tiles):
    def run(a_ref, b_ref, c_ref):
        y = _dot(a_ref[...], w_ref[0:A_W, :])
        y += _dot(b_ref[...], w_ref[A_W:A_W + B_W, :])
        y += _dot(c_ref[...], w_ref[A_W + B_W:, :])
        h = h_ref[...] + y
        h_o[...] = h
        xn = _rms(h, g2_ref[...])
        xn_o[...] = xn
        lg_o[...] = jnp.dot(xn, rw_ref[...], preferred_element_type=f32,
                            precision=lax.Precision.HIGHEST) + rb_ref[...]

    i = pl.program_id(0)
    pl.when(i < prompt_tiles)(functools.partial(run, ap_ref, bp_ref, cp_ref))
    pl.when(i >= prompt_tiles)(functools.partial(run, as_ref, bs_ref, cs_ref))


def _outproj(mix_p, mix_s, h_all, lw):
    n = h_all.shape[0]
    tm = ROW_TILE
    pt = mix_p[0].shape[0] // tm
    row = lambda w: pl.BlockSpec((tm, w), lambda i: (i, 0))
    first = lambda w: pl.BlockSpec((tm, w), lambda i: (jnp.minimum(i, pt - 1), 0))
    second = lambda w: pl.BlockSpec((tm, w), lambda i: (jnp.maximum(i - pt, 0), 0))
    const = lambda a: pl.BlockSpec(a.shape, lambda i: (0,) * a.ndim)
    consts = (lw['w_out'], lw['ln2_g'], lw['router_w'], lw['router_b'])
    return pl.pallas_call(
        functools.partial(_outproj_body, prompt_tiles=pt),
        grid=(n // tm,),
        in_specs=[first(A_W), first(B_W), first(C_W), second(A_W), second(B_W), second(C_W), row(D_MODEL)]
        + [const(a) for a in consts],
        out_specs=[row(D_MODEL), row(D_MODEL), row(N_EXPERTS)],
        out_shape=[jax.ShapeDtypeStruct((n, D_MODEL), f32), jax.ShapeDtypeStruct((n, D_MODEL), f32),
                   jax.ShapeDtypeStruct((n, N_EXPERTS), f32)],
        compiler_params=_cparams("parallel"),
        name="outproj",
    )(*mix_p, *mix_s, h_all, *consts)


def _route_body(lg_ref, e_o, g_o, r_o, cnt_o):
    @pl.when(pl.program_id(0) == 0)
    def _():
        cnt_o[...] = jnp.zeros(cnt_o.shape, f32)

    tm = lg_ref.shape[0]
    lane = lax.broadcasted_iota(i32, (tm, N_EXPERTS), 1)
    slot = lax.broadcasted_iota(i32, (tm, TOP_K), 1)
    work = lg_ref[...]
    onehots, vals = [], []
    e_out = jnp.zeros((tm, TOP_K), i32)
    for k in range(TOP_K):
        m = jnp.max(work, axis=-1, keepdims=True)
        idx = jnp.min(jnp.where(work == m, lane, N_EXPERTS), axis=-1, keepdims=True)
        oh = lane == idx
        work = jnp.where(oh, -jnp.inf, work)
        onehots.append(oh)
        vals.append(m)
        e_out = jnp.where(slot == k, idx, e_out)
    ex = [jnp.exp(v - vals[0]) for v in vals]
    denom = ex[0] + ex[1] + ex[2] + ex[3]
    g_out = jnp.zeros((tm, TOP_K), f32)
    sel = jnp.zeros((tm, N_EXPERTS), f32)
    for k in range(TOP_K):
        g_out = jnp.where(slot == k, ex[k] / denom, g_out)
        sel = sel + onehots[k].astype(f32)
    r = lax.broadcasted_iota(i32, (tm, tm), 0)
    c = lax.broadcasted_iota(i32, (tm, tm), 1)
    before = (c < r).astype(bf16)
    rank = _dot(before, sel.astype(bf16)) + cnt_o[...]
    r_out = jnp.zeros((tm, TOP_K), f32)
    for k in range(TOP_K):
        rk = jnp.sum(jnp.where(onehots[k], rank, 0.0), axis=-1, keepdims=True)
        r_out = jnp.where(slot == k, rk, r_out)
    e_o[...] = e_out
    g_o[...] = g_out
    r_o[...] = r_out.astype(i32)
    cnt_o[...] += jnp.sum(sel, axis=0, keepdims=True)


def _route(logits):
    n = logits.shape[0]
    tm = ROW_TILE
    row = lambda w: pl.BlockSpec((tm, w), lambda i: (i, 0))
    return pl.pallas_call(
        _route_body,
        grid=(n // tm,),
        in_specs=[row(N_EXPERTS)],
        out_specs=[row(TOP_K), row(TOP_K), row(TOP_K), pl.BlockSpec((1, N_EXPERTS), lambda i: (0, 0))],
        out_shape=[jax.ShapeDtypeStruct((n, TOP_K), i32), jax.ShapeDtypeStruct((n, TOP_K), f32),
                   jax.ShapeDtypeStruct((n, TOP_K), i32), jax.ShapeDtypeStruct((1, N_EXPERTS), f32)],
        compiler_params=_cparams("arbitrary"),
        name="route",
    )(logits)


def _row_copy(src_hbm, row, dst, slot, j, sem):
    return pltpu.make_async_copy(src_hbm.at[pl.ds(row, 1), :], dst.at[slot, pl.ds(j, 1), :], sem.at[slot])


def _experts_body(blk_e_ref, row_tok_ref, used_ref, x_hbm, wg_ref, bg_ref, wu_ref, bu_ref, wd_ref, bd_ref, y_o,
                  xbuf, xb, wbf, sem):
    b = pl.program_id(0)
    bm = MOE_BLOCK
    last = used_ref[0] - 1
    slot = b % 2

    def wait_block(s):
        pltpu.make_async_copy(x_hbm.at[pl.ds(0, bm), :], xbuf.at[s], sem.at[s]).wait()

    @pl.when(b == 0)
    def _():
        def issue(j, carry):
            _row_copy(x_hbm, row_tok_ref[j], xbuf, 0, j, sem).start()
            return carry
        lax.fori_loop(0, bm, issue, 0)

    @pl.when(b <= last)
    def _():
        wait_block(slot)
        xb[...] = xbuf[slot].astype(bf16)

        @pl.when(jnp.logical_or(b == 0, blk_e_ref[b] != blk_e_ref[jnp.maximum(b - 1, 0)]))
        def _():
            wbf[0] = wg_ref[...].astype(bf16)
            wbf[1] = wu_ref[...].astype(bf16)
            wbf[2] = wd_ref[...].astype(bf16)

        nxt = jnp.minimum(b + 1, last) * bm
        for j in range(bm):
            _row_copy(x_hbm, row_tok_ref[nxt + j], xbuf, 1 - slot, j, sem).start()

        x = xb[...]
        g = _dot(x, wbf[0]) + bg_ref[...]
        u = _dot(x, wbf[1]) + bu_ref[...]
        g = jnp.minimum(g, SWIGLU_LIMIT)
        u = jnp.clip(u, -SWIGLU_LIMIT, SWIGLU_LIMIT)
        hdn = (u + 1.0) * (g * _sigmoid(g * SWIGLU_ALPHA))
        y_o[...] = _dot(hdn.astype(bf16), wbf[2]) + bd_ref[...]

    @pl.when(b == last)
    def _():
        wait_block(1 - slot)

    @pl.when(b > last)
    def _():
        y_o[...] = jnp.zeros(y_o.shape, f32)


def _experts(x, blk_e, row_tok, n_used, layer, ew):
    nblk = blk_e.shape[0]
    bm = MOE_BLOCK
    wspec = lambda: pl.BlockSpec((None, None, D_MODEL, D_EXPERT), lambda b, be, rt, nu: (layer, be[b], 0, 0))
    bspec = lambda: pl.BlockSpec((None, None, 1, D_EXPERT), lambda b, be, rt, nu: (layer, be[b], 0, 0))
    grid_spec = pltpu.PrefetchScalarGridSpec(
        num_scalar_prefetch=3,
        grid=(nblk,),
        in_specs=[pl.BlockSpec(memory_space=pl.ANY), wspec(), bspec(), wspec(), bspec(), wspec(), bspec()],
        out_specs=pl.BlockSpec((bm, D_MODEL), lambda b, be, rt, nu: (b, 0)),
        scratch_shapes=[pltpu.VMEM((2, bm, D_MODEL), f32), pltpu.VMEM((bm, D_MODEL), bf16),
                        pltpu.VMEM((3, D_MODEL, D_EXPERT), bf16), pltpu.SemaphoreType.DMA((2,))],
    )
    return pl.pallas_call(
        _experts_body,
        grid_spec=grid_spec,
        out_shape=jax.ShapeDtypeStruct((nblk * bm, D_MODEL), f32),
        compiler_params=_cparams("arbitrary"),
        name="experts",
    )(blk_e, row_tok, n_used, x, ew['e_w_gate'], ew['e_b_gate'], ew['e_w_up'], ew['e_b_up'],
      ew['e_w_down'], ew['e_b_down'])


def _combine_body(dest_ref, y_hbm, h_ref, g_ref, o_ref, ybuf, sem):
    i = pl.program_id(0)
    nt = pl.num_programs(0)
    tt = COMBINE_TILE

    def start_tile(tile, slot):
        def issue(j, carry):
            for k in range(TOP_K):
                pltpu.make_async_copy(y_hbm.at[pl.ds(dest_ref[(tile * tt + j) * TOP_K + k], 1), :],
                                      ybuf.at[slot, k, pl.ds(j, 1), :], sem.at[slot]).start()
            return carry
        lax.fori_loop(0, tt, issue, 0, unroll=8)

    def wait_tile(slot):
        for k in range(TOP_K):
            pltpu.make_async_copy(y_hbm.at[pl.ds(0, tt), :], ybuf.at[slot, k], sem.at[slot]).wait()

    slot = i % 2

    @pl.when(i == 0)
    def _():
        start_tile(0, 0)

    @pl.when(i + 1 < nt)
    def _():
        start_tile(i + 1, 1 - slot)

    wait_tile(slot)
    g = g_ref[...]
    acc = h_ref[...]
    for k in range(TOP_K):
        acc = acc + ybuf[slot, k] * g[:, k:k + 1]
    o_ref[...] = acc


def _combine(y_rows, dest, h, gates):
    n = h.shape[0]
    tt = COMBINE_TILE
    grid_spec = pltpu.PrefetchScalarGridSpec(
        num_scalar_prefetch=1,
        grid=(n // tt,),
        in_specs=[pl.BlockSpec(memory_space=pl.ANY),
                  pl.BlockSpec((tt, D_MODEL), lambda i, d: (i, 0)),
                  pl.BlockSpec((tt, TOP_K), lambda i, d: (i, 0))],
        out_specs=pl.BlockSpec((tt, D_MODEL), lambda i, d: (i, 0)),
        scratch_shapes=[pltpu.VMEM((2, TOP_K, tt, D_MODEL), f32), pltpu.SemaphoreType.DMA((2,))],
    )
    return pl.pallas_call(
        _combine_body,
        grid_spec=grid_spec,
        out_shape=jax.ShapeDtypeStruct((n, D_MODEL), f32),
        compiler_params=_cparams("arbitrary"),
        name="combine",
    )(dest, y_rows, h, gates)


def _moe(h, xn, logits, layer, ew):
    n = h.shape[0]
    bm = MOE_BLOCK
    top_e, gates, rank, counts = _route(logits)
    counts = counts[0].astype(i32)
    padded = ((counts + bm - 1) // bm) * bm
    pend = jnp.cumsum(padded)
    pstart = pend - padded
    expert = jnp.arange(N_EXPERTS, dtype=i32)
    dest = jnp.sum(jnp.where(top_e[:, :, None] == expert, pstart, 0), axis=-1) + rank
    nblk = (n * TOP_K) // bm + N_EXPERTS
    tok = jnp.broadcast_to(jnp.arange(n, dtype=i32)[:, None], (n, TOP_K))
    row_tok = jnp.zeros((nblk * bm,), i32).at[dest.reshape(-1)].set(tok.reshape(-1))
    blk_start = jnp.arange(nblk, dtype=i32) * bm
    blk_e = jnp.minimum(jnp.sum((pend[None, :] <= blk_start[:, None]).astype(i32), axis=1), N_EXPERTS - 1)
    y_rows = _experts(xn, blk_e, row_tok, pend[-1:] // bm, layer, ew)
    return _combine(y_rows, dest.reshape(-1), h, gates)


def _rope_tables(pos, tile_rows):
    half = C_ROPE_DIM // 2
    inv = ROPE_THETA ** (-jnp.arange(half, dtype=f32) / half)
    ang = pos.astype(f32)[:, None] * inv[None, :]
    reps = max(1, tile_rows // ang.shape[0])
    return jnp.tile(jnp.cos(ang), (reps, 1)), jnp.tile(jnp.sin(ang), (reps, 1))


def _band_bias(rel_bias, rows):
    r = jnp.arange(rows)[:, None]
    c = jnp.arange(A_PAST + rows)[None, :]
    idx = jnp.clip(A_PAST + r - c, -A_REL_CLIP, A_REL_CLIP) + A_REL_CLIP
    lo = (r // CHUNK) * CHUNK
    in_band = jnp.logical_and(c >= lo, c < lo + A_BAND)
    return jnp.where(in_band[None], rel_bias[:, idx].astype(f32) * LOG2E, NEG)


def _layer_weights(l, p):
    w = p['w_in'][l]
    seg = lambda o, n: w[:, o:o + n]
    zc = lambda n: jnp.zeros((D_MODEL, n), f32)
    o_aq, o_ak, o_av, o_bq, o_bk, o_bv = 0, 384, 768, 1152, 1408, 1664
    o_blr, o_bg, o_cq, o_ckv, o_ckr = 2048, 2064, 2448, 2640, 2768
    w_in = jnp.concatenate([
        seg(o_aq, 384), seg(o_ak, 384), seg(o_av, 384), seg(o_bq, 256), seg(o_bk, 256), seg(o_bv, 384),
        seg(o_bg, 384), seg(o_cq, C_Q_RANK), zc(P_CKV - P_CQ - C_Q_RANK), seg(o_ckv, 128),
        seg(o_ckr, 32), seg(o_blr, 16), zc(IN_P - P_BLR - B_GATE_RANK)], axis=1).astype(bf16)
    hq = C_NOPE_DIM + C_ROPE_DIM
    wuq = p['c_w_uq'][l].reshape(C_Q_RANK, C_HEADS, hq)
    wuq = jnp.pad(wuq, ((0, 0), (0, 0), (0, C_PAD_DIM - hq))).reshape(C_Q_RANK, C_QW).astype(bf16)
    wukv = p['c_w_ukv'][l].reshape(C_KV_RANK, C_HEADS, C_NOPE_DIM + C_V_DIM)
    wukv = jnp.concatenate([wukv[:, :, :C_NOPE_DIM].reshape(C_KV_RANK, -1),
                            wukv[:, :, C_NOPE_DIM:].reshape(C_KV_RANK, -1)], axis=1).astype(bf16)
    r2 = lambda a: a.reshape(1, -1)
    return {
        'ln1_g': r2(p['ln1_g'][l]), 'ln2_g': r2(p['ln2_g'][l]), 'w_in': w_in,
        'a_q_g': r2(p['a_q_g'][l]), 'a_k_g': r2(p['a_k_g'][l]),
        'bias_p': _band_bias(p['a_rel_bias'][l], BAND_ROWS), 'bias_s': _band_bias(p['a_rel_bias'][l], CHUNK),
        'b_gate_w2': p['b_gate_w2'][l], 'b_gate_b': r2(p['b_gate_b'][l]), 'b_out_g': r2(p['b_out_g'][l]),
        'c_qa_g': r2(p['c_qa_g'][l]), 'c_w_uq': wuq, 'c_kva_g': r2(p['c_kva_g'][l]), 'c_w_ukv': wukv,
        'c_qn_g': r2(p['c_qn_g'][l]), 'c_qr_g': r2(p['c_qr_g'][l]), 'c_kn_g': r2(p['c_kn_g'][l]),
        'c_kr_g': r2(p['c_kr_g'][l]), 'w_out': p['w_out'][l].astype(bf16),
        'router_w': p['router_w'][l], 'router_b': r2(p['router_b'][l]),
    }


def _token_mix(h_all, row0, b_, t, lw, cos, sin, past):
    aq, ak, av, akb, avb, slab, cq, ckv, ckr = _proj(h_all, row0, b_ * t, lw, cos, sin)
    r3 = lambda a: a.reshape(b_, t, a.shape[-1])
    ck, cv = _mla_keys(ckv, ckr, lw)
    if past is None:
        a_o = _band(r3(aq), r3(akb), r3(avb), None, None, lw['bias_p'])
        s0_t = jnp.zeros((b_, B_HEADS, B_VAL_DIM, B_KEY_DIM), f32)
        c_o = _mla_causal(r3(cq), r3(ck), r3(cv))
    else:
        a_k, a_v, b_s, c_kv, c_kr = past
        lp = a_k.shape[1]
        v_past = jnp.concatenate([a_v.astype(bf16), jnp.ones(a_v.shape[:-1] + (1,), bf16),
                                  jnp.zeros(a_v.shape[:-1] + (V_EXT - A_HEAD_DIM - 1,), bf16)], axis=-1)
        a_o = _band(r3(aq), r3(akb), r3(avb), a_k.reshape(b_, lp, A_W).astype(bf16),
                    v_past.reshape(b_, lp, A_VW), lw['bias_s'])
        s0_t = jnp.swapaxes(b_s, -1, -2)
        c_o = _mla_sample(r3(cq), c_kv, c_kr, r3(ck), r3(cv), lw)
    b_o, s_t = _gla(r3(slab), s0_t, lw['b_out_g'])
    flat = lambda a: a.reshape(b_ * t, a.shape[-1])
    la = min(A_PAST, t)
    state = (r3(ak)[:, t - la:].reshape(b_, la, A_HEADS, A_HEAD_DIM),
             r3(av)[:, t - la:].reshape(b_, la, A_HEADS, A_HEAD_DIM),
             jnp.swapaxes(s_t, -1, -2), r3(ckv), r3(ckr))
    return (flat(a_o), flat(b_o), flat(c_o)), state


def kernel(x_prompt, x_sample, cache_a_k, cache_a_v, state_b_s, cache_c_kv, cache_c_kr, ln1_g, ln2_g, w_in, a_q_g, a_k_g, a_rel_bias, b_gate_w2, b_gate_b, b_out_g, c_qa_g, c_w_uq, c_kva_g, c_w_ukv, c_qn_g, c_qr_g, c_kn_g, c_kr_g, w_out, router_w, router_b, e_w_gate, e_b_gate, e_w_up, e_b_up, e_w_down, e_b_down):
    params = dict(ln1_g=ln1_g, ln2_g=ln2_g, w_in=w_in, a_q_g=a_q_g, a_k_g=a_k_g, a_rel_bias=a_rel_bias,
                  b_gate_w2=b_gate_w2, b_gate_b=b_gate_b, b_out_g=b_out_g, c_qa_g=c_qa_g, c_w_uq=c_w_uq,
                  c_kva_g=c_kva_g, c_w_ukv=c_w_ukv, c_qn_g=c_qn_g, c_qr_g=c_qr_g, c_kn_g=c_kn_g,
                  c_kr_g=c_kr_g, w_out=w_out, router_w=router_w, router_b=router_b)
    ew = dict(e_w_gate=e_w_gate, e_b_gate=e_b_gate[:, :, None, :], e_w_up=e_w_up, e_b_up=e_b_up[:, :, None, :],
              e_w_down=e_w_down, e_b_down=e_b_down[:, :, None, :])
    bp, tp, _ = x_prompt.shape
    bs, ts, _ = x_sample.shape
    depth = w_in.shape[0]
    n_p = bp * tp
    cos_p, sin_p = _rope_tables(jnp.arange(tp), ROW_TILE)
    cos_s, sin_s = _rope_tables(PAST_LEN + jnp.arange(ts), ROW_TILE)
    h = jnp.concatenate([x_prompt.reshape(n_p, D_MODEL), x_sample.reshape(bs * ts, D_MODEL)])
    st_p, st_s = [], []
    for l in range(depth):
        lw = _layer_weights(l, params)
        mix_p, s_p = _token_mix(h, 0, bp, tp, lw, cos_p, sin_p, None)
        mix_s, s_s = _token_mix(h, n_p, bs, ts, lw, cos_s, sin_s,
                                (cache_a_k[l], cache_a_v[l], state_b_s[l], cache_c_kv[l], cache_c_kr[l]))
        h, xn, logits = _outproj(mix_p, mix_s, h, lw)
        h = _moe(h, xn, logits, l, ew)
        st_p.append(s_p)
        st_s.append(s_s)
    stack = lambda sts, i: jnp.stack([s[i] for s in sts])
    return (h[:n_p].reshape(bp, tp, D_MODEL), h[n_p:].reshape(bs, ts, D_MODEL),
            stack(st_p, 0), stack(st_p, 1), stack(st_p, 2), stack(st_p, 3), stack(st_p, 4),
            stack(st_s, 0), stack(st_s, 1), stack(st_s, 2), stack(st_s, 3), stack(st_s, 4))
```

```python
import functools
import math

import jax
import jax.numpy as jnp
import numpy as np
from jax import lax
from jax.experimental import pallas as pl
from jax.experimental.pallas import tpu as pltpu

f32 = jnp.float32
bf16 = jnp.bfloat16
i32 = jnp.int32

D_MODEL = 1024
PAST_LEN = 2048
CHUNK = 64
A_HEADS = 6
A_HEAD_DIM = 64
A_BAND_CHUNKS = 8
A_PAST = A_BAND_CHUNKS * CHUNK
A_BAND = A_PAST + CHUNK
A_REL_CLIP = 128
B_HEADS = 4
B_KEY_DIM = 64
B_VAL_DIM = 96
B_GATE_RANK = 16
B_GATE_TAU = 16.0
C_HEADS = 4
C_Q_RANK = 192
C_KV_RANK = 128
C_NOPE_DIM = 64
C_ROPE_DIM = 32
C_V_DIM = 64
ROPE_THETA = 10000.0
N_EXPERTS = 32
TOP_K = 4
D_EXPERT = 1024
SWIGLU_LIMIT = 7.0
SWIGLU_ALPHA = 1.702
EPS = 1e-6
NEG = -1e30
LOG2E = 1.0 / math.log(2.0)

LANES = 128
C_PAD_DIM = LANES
V_EXT = LANES
A_QSCALE = A_HEAD_DIM ** -0.5 * LOG2E
C_QSCALE = (C_NOPE_DIM + C_ROPE_DIM) ** -0.5 * LOG2E

A_W = A_HEADS * A_HEAD_DIM
B_W = B_HEADS * B_VAL_DIM
C_W = C_HEADS * C_V_DIM
B_KW = B_HEADS * B_KEY_DIM
C_QW = C_HEADS * C_PAD_DIM
A_VW = A_HEADS * V_EXT
C_VW = C_HEADS * V_EXT

P_AQ, P_AK, P_AV = 0, 384, 768
P_BQ, P_BK, P_BV, P_BG = 1152, 1408, 1664, 2048
P_CQ, P_CKV, P_CKR, P_BLR = 2432, 2688, 2816, 2912
IN_P = 2944
S_BQ, S_BK, S_BV, S_LA, S_BG, S_W = 0, 256, 512, 896, 1152, 1536

ROW_TILE = 512
BAND_ROWS = 256
GLA_CHUNKS = 4
MLA_TILE = 512
MOE_BLOCK = 256
COMBINE_TILE = 128
VMEM_LIMIT = 56 * 1024 * 1024


def _cparams(*sem):
    return pltpu.CompilerParams(dimension_semantics=sem, vmem_limit_bytes=VMEM_LIMIT)


def _rms(x, g):
    return x * lax.rsqrt(jnp.mean(x * x, axis=-1, keepdims=True) + EPS) * g


def _rope_lanes(x, cos_t, sin_t):
    w = x.shape[1]
    lane = lax.broadcasted_iota(i32, x.shape, 1) % LANES
    first_half = jnp.logical_and(lane >= C_NOPE_DIM, lane < C_NOPE_DIM + C_ROPE_DIM // 2)
    swapped = jnp.where(first_half, pltpu.roll(x, w - C_ROPE_DIM // 2, 1), pltpu.roll(x, C_ROPE_DIM // 2, 1))
    return x * cos_t + swapped * sin_t


def _dot(a, b):
    return jnp.dot(a, b, preferred_element_type=f32)


def _dot_nt(a, b):
    return lax.dot_general(a, b, (((1,), (1,)), ((), ())), preferred_element_type=f32)


def _dot_tn(a, b):
    return lax.dot_general(a, b, (((0,), (0,)), ((), ())), preferred_element_type=f32)


def _sigmoid(x):
    return 1.0 / (1.0 + jnp.exp(-x))


def _ones_column(rows):
    lane = lax.broadcasted_iota(i32, (rows, V_EXT - C_V_DIM), 1)
    return jnp.where(lane == 0, 1.0, 0.0).astype(bf16)


def _group_mean_sq(x, g):
    sq = x * x
    hi = sq.astype(bf16)
    lo = (sq - hi.astype(f32)).astype(bf16)
    return _dot(hi, g) + _dot(lo, g)


def _group_rms(x, g, gain):
    return x * lax.rsqrt(_group_mean_sq(x, g) + EPS) * gain


def _softmax_pv(s, v_blocks, widths):
    p = jnp.exp2(s - jnp.max(s, axis=-1, keepdims=True)).astype(bf16)
    acc, lo = None, 0
    for v, w in zip(v_blocks, widths):
        part = _dot(p[:, lo:lo + w], v)
        acc = part if acc is None else acc + part
        lo += w
    return acc[:, :C_V_DIM] / acc[:, C_V_DIM:C_V_DIM + 1]


def _proj_body(x_ref, g1_ref, w_ref, ga_ref, aqg_ref, akg_ref, w2_ref, gb_ref, qag_ref, wuq_ref,
               gc_ref, cqg_ref, kvag_ref, krg_ref, cos_ref, sin_ref,
               aq_o, ak_o, av_o, akb_o, avb_o, b_o, cq_o, ckv_o, ckr_o, h_sc):
    tm = x_ref.shape[0]
    xn = _rms(x_ref[...], g1_ref[...])
    h_sc[...] = _dot(xn.astype(bf16), w_ref[...])
    cos_t = cos_ref[...]
    sin_t = sin_ref[...]
    q = _group_rms(h_sc[:, P_AQ:P_AQ + A_W], ga_ref[...], aqg_ref[...])
    aq_o[...] = (q * A_QSCALE).astype(bf16)
    k = _group_rms(h_sc[:, P_AK:P_AK + A_W], ga_ref[...], akg_ref[...])
    ak_o[...] = k
    akb_o[...] = k.astype(bf16)
    av_o[...] = h_sc[:, P_AV:P_AV + A_W]
    ones = _ones_column(tm)
    for hd in range(A_HEADS):
        lo = P_AV + hd * A_HEAD_DIM
        avb_o[:, hd * V_EXT:hd * V_EXT + A_HEAD_DIM] = h_sc[:, lo:lo + A_HEAD_DIM].astype(bf16)
        avb_o[:, hd * V_EXT + A_HEAD_DIM:(hd + 1) * V_EXT] = ones
    b_o[:, S_BQ:S_LA] = h_sc[:, P_BQ:P_BG]
    z = jnp.dot(h_sc[:, P_BLR:P_BLR + B_GATE_RANK], w2_ref[...], preferred_element_type=f32,
                precision=lax.Precision.HIGHEST) + gb_ref[...]
    log_sig = jnp.minimum(z, 0.0) - jnp.log1p(jnp.exp(-jnp.abs(z)))
    b_o[:, S_LA:S_BG] = log_sig * (1.0 / B_GATE_TAU)
    b_o[:, S_BG:S_W] = h_sc[:, P_BG:P_BG + B_W]
    cql = _rms(h_sc[:, P_CQ:P_CQ + C_Q_RANK], qag_ref[...])
    cq = _group_rms(_dot(cql.astype(bf16), wuq_ref[...]), gc_ref[...], cqg_ref[...])
    cos4 = jnp.concatenate([cos_t] * C_HEADS, axis=1)
    sin4 = jnp.concatenate([sin_t] * C_HEADS, axis=1)
    cq_o[...] = (_rope_lanes(cq, cos4, sin4) * C_QSCALE).astype(bf16)
    ckv_o[...] = _rms(h_sc[:, P_CKV:P_CKV + C_KV_RANK], kvag_ref[...])
    kr = _group_rms(h_sc[:, P_CKR:P_CKR + LANES], gc_ref[0:LANES, 0:LANES], krg_ref[...])
    ckr_o[...] = _rope_lanes(kr, cos_t, sin_t)[:, C_NOPE_DIM:C_NOPE_DIM + C_ROPE_DIM]


def _proj(h_all, row0, n, lw, cos, sin):
    tm = ROW_TILE
    blk0 = row0 // tm
    npos = cos.shape[0] // tm
    row = lambda w: pl.BlockSpec((tm, w), lambda i: (i, 0))
    const = lambda a: pl.BlockSpec(a.shape, lambda i: (0,) * a.ndim)
    pos = pl.BlockSpec((tm, LANES), lambda i: (i % npos, 0))
    consts = (lw['ln1_g'], lw['w_in'], lw['g_a'], lw['a_q_g'], lw['a_k_g'], lw['b_gate_w2'], lw['b_gate_b'],
              lw['c_qa_g'], lw['c_w_uq'], lw['g_c'], lw['c_q_gain'], lw['c_kva_g'], lw['c_kr_gain'])
    out_w = ((A_W, bf16), (A_W, f32), (A_W, f32), (A_W, bf16), (A_VW, bf16), (S_W, f32),
             (C_QW, bf16), (C_KV_RANK, f32), (C_ROPE_DIM, f32))
    return pl.pallas_call(
        _proj_body,
        grid=(n // tm,),
        in_specs=[pl.BlockSpec((tm, D_MODEL), lambda i: (i + blk0, 0))] + [const(a) for a in consts] + [pos, pos],
        out_specs=[row(w) for w, _ in out_w],
        out_shape=[jax.ShapeDtypeStruct((n, w), dt) for w, dt in out_w],
        scratch_shapes=[pltpu.VMEM((tm, IN_P), f32)],
        compiler_params=_cparams("parallel"),
        name="proj",
    )(h_all, *consts, cos, sin)


def _expand_keys(kv, kr, g_c, k_gain, place, k_dst, v_dst, rows):
    kn = _group_rms(kv[:, :C_QW], g_c, k_gain)
    k_dst[rows, :] = (kn + _dot(kr, place)).astype(bf16)
    v = kv[:, C_QW:]
    lane = lax.broadcasted_iota(i32, v.shape, 1) % LANES
    v_dst[rows, :] = jnp.where(lane == C_V_DIM, 1.0, v).astype(bf16)


def _mla_keys_body(ckv_ref, ckr_ref, w_ref, gc_ref, kg_ref, pl_ref, ck_o, cv_o):
    kv = _dot(ckv_ref[...].astype(bf16), w_ref[...])
    _expand_keys(kv, ckr_ref[...].astype(bf16), gc_ref[...], kg_ref[...], pl_ref[...], ck_o, cv_o, slice(None))


def _mla_keys(ckv, ckr, lw):
    n = ckv.shape[0]
    tm = ROW_TILE
    row = lambda w: pl.BlockSpec((tm, w), lambda i: (i, 0))
    const = lambda a: pl.BlockSpec(a.shape, lambda i: (0,) * a.ndim)
    consts = (lw['c_w_ukv'], lw['g_c'], lw['c_k_gain'], lw['kr_place'])
    return pl.pallas_call(
        _mla_keys_body,
        grid=(n // tm,),
        in_specs=[row(C_KV_RANK), row(C_ROPE_DIM)] + [const(a) for a in consts],
        out_specs=[row(C_QW), row(C_VW)],
        out_shape=[jax.ShapeDtypeStruct((n, C_QW), bf16), jax.ShapeDtypeStruct((n, C_VW), bf16)],
        compiler_params=_cparams("parallel"),
        name="mla_keys",
    )(ckv, ckr, *consts)


def _band_body(q_ref, *refs, n_kb, past_from_seq):
    k_refs, v_refs = refs[:n_kb], refs[n_kb:2 * n_kb]
    bias_ref, o_ref = refs[2 * n_kb], refs[2 * n_kb + 1]
    rows = q_ref.shape[0]
    widths = [k.shape[0] for k in k_refs]
    if past_from_seq:
        col = lax.broadcasted_iota(i32, (rows, sum(widths)), 1)
        exists = col + (pl.program_id(1) * rows - A_PAST) >= 0
    for hd in range(A_HEADS):
        qs = slice(hd * A_HEAD_DIM, (hd + 1) * A_HEAD_DIM)
        vs = slice(hd * V_EXT, (hd + 1) * V_EXT)
        q = q_ref[:, qs]
        s = jnp.concatenate([_dot_nt(q, k[:, qs]) for k in k_refs], axis=1) + bias_ref[hd]
        if past_from_seq:
            s = jnp.where(exists, s, NEG)
        o_ref[:, qs] = _softmax_pv(s, [v[:, vs] for v in v_refs], widths).astype(bf16)


def _band(q, k, v, k_past, v_past, bias):
    b_, t, _ = q.shape
    from_seq = k_past is None
    rows = BAND_ROWS if from_seq else t
    n_back = A_PAST // rows if from_seq else 1
    cur = lambda w: pl.BlockSpec((None, rows, w), lambda b, i: (b, i, 0))
    if from_seq:
        back = lambda w, d: pl.BlockSpec((None, rows, w), lambda b, i: (b, jnp.maximum(i - d, 0), 0))
        k_specs = [back(A_W, d) for d in range(n_back, 0, -1)] + [cur(A_W)]
        v_specs = [back(A_VW, d) for d in range(n_back, 0, -1)] + [cur(A_VW)]
        k_args, v_args = [k] * (n_back + 1), [v] * (n_back + 1)
    else:
        past = lambda w: pl.BlockSpec((None, A_PAST, w), lambda b, i: (b, 0, 0))
        k_specs, v_specs = [past(A_W), cur(A_W)], [past(A_VW), cur(A_VW)]
        k_args, v_args = [k_past, k], [v_past, v]
    return pl.pallas_call(
        functools.partial(_band_body, n_kb=len(k_specs), past_from_seq=from_seq),
        grid=(b_, t // rows),
        in_specs=[cur(A_W)] + k_specs + v_specs + [pl.BlockSpec(bias.shape, lambda b, i: (0, 0, 0))],
        out_specs=cur(A_W),
        out_shape=jax.ShapeDtypeStruct((b_, t, A_W), bf16),
        compiler_params=_cparams("parallel", "parallel"),
        name="band",
    )(q, *k_args, *v_args, bias)


def _gla_body(b_ref, s0_ref, gain_ref, o_ref, st_ref, *, chunks):
    @pl.when(pl.program_id(1) == 0)
    def _():
        st_ref[...] = s0_ref[...]

    rows = chunks * CHUNK
    r = lax.broadcasted_iota(i32, (rows, rows), 0)
    c = lax.broadcasted_iota(i32, (rows, rows), 1)
    causal = jnp.logical_and(r >= c, r // CHUNK == c // CHUNK)
    tril = (lax.broadcasted_iota(i32, (CHUNK, CHUNK), 0) >= lax.broadcasted_iota(i32, (CHUNK, CHUNK), 1)).astype(f32)
    bcum, btot = [], []
    for ch in range(chunks):
        la = b_ref[ch * CHUNK:(ch + 1) * CHUNK, S_LA:S_LA + B_KW]
        cum = jnp.dot(tril, la, preferred_element_type=f32, precision=lax.Precision.HIGHEST)
        bcum.append(cum)
        btot.append(jnp.broadcast_to(cum[CHUNK - 1:CHUNK, :], cum.shape))
    bcum = jnp.concatenate(bcum, axis=0)
    btot = jnp.concatenate(btot, axis=0)
    k = b_ref[:, S_BK:S_BK + B_KW]
    qt = b_ref[:, S_BQ:S_BQ + B_KW] * (B_KEY_DIM ** -0.5) * jnp.exp(bcum)
    kt = k * jnp.exp(-bcum)
    kd = k * jnp.exp(btot - bcum)
    decay = jnp.exp(btot)
    for hd in range(B_HEADS):
        ks = slice(hd * B_KEY_DIM, (hd + 1) * B_KEY_DIM)
        vs = slice(hd * B_VAL_DIM, (hd + 1) * B_VAL_DIM)
        v = b_ref[:, S_BV + vs.start:S_BV + vs.stop]
        att = jnp.where(causal, _dot_nt(qt[:, ks], kt[:, ks]), 0.0)
        o_intra = _dot(att, v)
        st = st_ref[hd]
        outs = []
        for ch in range(chunks):
            rs = slice(ch * CHUNK, (ch + 1) * CHUNK)
            outs.append(o_intra[rs] + _dot_nt(qt[rs, ks], st))
            st = st * decay[ch * CHUNK:ch * CHUNK + 1, ks] + _dot_tn(v[rs], kd[rs, ks])
        st_ref[hd] = st
        g = b_ref[:, S_BG + vs.start:S_BG + vs.stop]
        o = _rms(jnp.concatenate(outs, axis=0), gain_ref[...]) * (g * _sigmoid(g))
        o_ref[:, vs] = o.astype(bf16)


def _gla(slab, s0_t, gain):
    b_, t, _ = slab.shape
    chunks = min(GLA_CHUNKS, t // CHUNK)
    rows = chunks * CHUNK
    st_spec = pl.BlockSpec((None, B_HEADS, B_VAL_DIM, B_KEY_DIM), lambda b, i: (b, 0, 0, 0))
    return pl.pallas_call(
        functools.partial(_gla_body, chunks=chunks),
        grid=(b_, t // rows),
        in_specs=[pl.BlockSpec((None, rows, S_W), lambda b, i: (b, i, 0)), st_spec,
                  pl.BlockSpec(gain.shape, lambda b, i: (0, 0))],
        out_specs=[pl.BlockSpec((None, rows, B_W), lambda b, i: (b, i, 0)), st_spec],
        out_shape=[jax.ShapeDtypeStruct((b_, t, B_W), bf16),
                   jax.ShapeDtypeStruct((b_, B_HEADS, B_VAL_DIM, B_KEY_DIM), f32)],
        compiler_params=_cparams("parallel", "arbitrary"),
        name="gla",
    )(slab, s0_t, gain)


def _mla_causal_body(q_ref, k_ref, v_ref, o_ref, m_sc, acc_sc):
    i = pl.program_id(1)
    t = q_ref.shape[0]
    m_sc[...] = jnp.full(m_sc.shape, NEG, f32)
    acc_sc[...] = jnp.zeros(acc_sc.shape, f32)

    def tile(j, diagonal):
        rows = pl.ds(pl.multiple_of(j * t, t), t)
        if diagonal:
            visible = (lax.broadcasted_iota(i32, (t, t), 1) // CHUNK) <= (lax.broadcasted_iota(i32, (t, t), 0) // CHUNK)
        for hd in range(C_HEADS):
            hs = slice(hd * C_PAD_DIM, (hd + 1) * C_PAD_DIM)
            s = _dot_nt(q_ref[:, hs], k_ref[rows, hs])
            if diagonal:
                s = jnp.where(visible, s, NEG)
            m_old = m_sc[hd]
            m_new = jnp.maximum(m_old, jnp.max(s, axis=-1, keepdims=True))
            p = jnp.exp2(s - jnp.concatenate([m_new] * (t // LANES), axis=1))
            acc_sc[hd] = jnp.exp2(m_old - m_new) * acc_sc[hd] + _dot(p.astype(bf16), v_ref[rows, hs])
            m_sc[hd] = m_new

    def earlier(j, carry):
        tile(j, False)
        return carry

    lax.fori_loop(0, i, earlier, 0)
    tile(i, True)
    for hd in range(C_HEADS):
        acc = acc_sc[hd]
        o_ref[:, hd * C_V_DIM:(hd + 1) * C_V_DIM] = (acc[:, :C_V_DIM] / acc[:, C_V_DIM:C_V_DIM + 1]).astype(bf16)


def _mla_causal(q, k, v):
    b_, t, _ = q.shape
    tq = MLA_TILE
    whole = lambda w: pl.BlockSpec((None, t, w), lambda b, i: (b, 0, 0))
    return pl.pallas_call(
        _mla_causal_body,
        grid=(b_, t // tq),
        in_specs=[pl.BlockSpec((None, tq, C_QW), lambda b, i: (b, i, 0)), whole(C_QW), whole(C_VW)],
        out_specs=pl.BlockSpec((None, tq, C_W), lambda b, i: (b, i, 0)),
        out_shape=jax.ShapeDtypeStruct((b_, t, C_W), bf16),
        scratch_shapes=[pltpu.VMEM((C_HEADS, tq, LANES), f32), pltpu.VMEM((C_HEADS, tq, V_EXT), f32)],
        compiler_params=_cparams("parallel", "arbitrary"),
        name="mla_causal",
    )(q, k, v)


def _mla_sample_body(q_ref, ckv_ref, ckr_ref, kn_ref, vn_ref, w_ref, gc_ref, kg_ref, pl_ref, o_ref, k_sc, v_sc):
    past = ckv_ref.shape[0]
    kv = _dot(ckv_ref[...].astype(bf16), w_ref[...])
    _expand_keys(kv, ckr_ref[...].astype(bf16), gc_ref[...], kg_ref[...], pl_ref[...], k_sc, v_sc, slice(0, past))
    k_sc[past:, :] = kn_ref[...]
    v_sc[past:, :] = vn_ref[...]
    total = k_sc.shape[0]
    for hd in range(C_HEADS):
        hs = slice(hd * C_PAD_DIM, (hd + 1) * C_PAD_DIM)
        s = _dot_nt(q_ref[:, hs], k_sc[:, hs])
        o_ref[:, hd * C_V_DIM:(hd + 1) * C_V_DIM] = _softmax_pv(s, [v_sc[:, hs]], [total]).astype(bf16)


def _mla_sample(q, c_kv, c_kr, k_new, v_new, lw):
    b_, s_len, _ = q.shape
    past = c_kv.shape[1]
    per_b = lambda r, w: pl.BlockSpec((None, r, w), lambda b: (b, 0, 0))
    const = lambda a: pl.BlockSpec(a.shape, lambda b: (0,) * a.ndim)
    consts = (lw['c_w_ukv'], lw['g_c'], lw['c_k_gain'], lw['kr_place'])
    return pl.pallas_call(
        _mla_sample_body,
        grid=(b_,),
        in_specs=[per_b(s_len, C_QW), per_b(past, C_KV_RANK), per_b(past, C_ROPE_DIM), per_b(s_len, C_QW),
                  per_b(s_len, C_VW)] + [const(a) for a in consts],
        out_specs=per_b(s_len, C_W),
        out_shape=jax.ShapeDtypeStruct((b_, s_len, C_W), bf16),
        scratch_shapes=[pltpu.VMEM((past + s_len, C_QW), bf16), pltpu.VMEM((past + s_len, C_VW), bf16)],
        compiler_params=_cparams("parallel"),
        name="mla_sample",
    )(q, c_kv, c_kr, k_new, v_new, *consts)


def _outproj_body(ap_ref, bp_ref, cp_ref, as_ref, bs_ref, cs_ref, h_ref, w_ref, g2_ref, rwh_ref, rwl_ref, rb_ref,
                  h_o, xn_o, lg_o, *, prompt_tiles):
    def run(a_ref, b_ref, c_ref):
        y = _dot(a_ref[...], w_ref[0:A_W, :])
        y += _dot(b_ref[...], w_ref[A_W:A_W + B_W, :])
        y += _dot(c_ref[...], w_ref[A_W + B_W:, :])
        h = h_ref[...] + y
        h_o[...] = h
        xn = _rms(h, g2_ref[...])
        xn_o[...] = xn
        xh = xn.astype(bf16)
        xl = (xn - xh.astype(f32)).astype(bf16)
        lg_o[...] = _dot(xh, rwh_ref[...]) + _dot(xl, rwh_ref[...]) + _dot(xh, rwl_ref[...]) + rb_ref[...]

    i = pl.program_id(0)
    pl.when(i < prompt_tiles)(functools.partial(run, ap_ref, bp_ref, cp_ref))
    pl.when(i >= prompt_tiles)(functools.partial(run, as_ref, bs_ref, cs_ref))


def _outproj(mix_p, mix_s, h_all, lw):
    n = h_all.shape[0]
    tm = ROW_TILE
    pt = mix_p[0].shape[0] // tm
    row = lambda w: pl.BlockSpec((tm, w), lambda i: (i, 0))
    first = lambda w: pl.BlockSpec((tm, w), lambda i: (jnp.minimum(i, pt - 1), 0))
    second = lambda w: pl.BlockSpec((tm, w), lambda i: (jnp.maximum(i - pt, 0), 0))
    const = lambda a: pl.BlockSpec(a.shape, lambda i: (0,) * a.ndim)
    consts = (lw['w_out'], lw['ln2_g'], lw['router_w_hi'], lw['router_w_lo'], lw['router_b'])
    return pl.pallas_call(
        functools.partial(_outproj_body, prompt_tiles=pt),
        grid=(n // tm,),
        in_specs=[first(A_W), first(B_W), first(C_W), second(A_W), second(B_W), second(C_W), row(D_MODEL)]
        + [const(a) for a in consts],
        out_specs=[row(D_MODEL), row(D_MODEL), row(N_EXPERTS)],
        out_shape=[jax.ShapeDtypeStruct((n, D_MODEL), f32), jax.ShapeDtypeStruct((n, D_MODEL), f32),
                   jax.ShapeDtypeStruct((n, N_EXPERTS), f32)],
        compiler_params=_cparams("parallel"),
        name="outproj",
    )(*mix_p, *mix_s, h_all, *consts)


def _route_body(lg_ref, e_o, g_o, r_o, cnt_o):
    @pl.when(pl.program_id(0) == 0)
    def _():
        cnt_o[...] = jnp.zeros(cnt_o.shape, f32)

    tm = lg_ref.shape[0]
    lane = lax.broadcasted_iota(i32, (tm, N_EXPERTS), 1)
    slot = lax.broadcasted_iota(i32, (tm, TOP_K), 1)
    work = lg_ref[...]
    onehots, vals = [], []
    e_out = jnp.zeros((tm, TOP_K), i32)
    for k in range(TOP_K):
        m = jnp.max(work, axis=-1, keepdims=True)
        idx = jnp.min(jnp.where(work == m, lane, N_EXPERTS), axis=-1, keepdims=True)
        oh = lane == idx
        work = jnp.where(oh, -jnp.inf, work)
        onehots.append(oh)
        vals.append(m)
        e_out = jnp.where(slot == k, idx, e_out)
    ex = [jnp.exp(v - vals[0]) for v in vals]
    denom = ex[0] + ex[1] + ex[2] + ex[3]
    g_out = jnp.zeros((tm, TOP_K), f32)
    sel = jnp.zeros((tm, N_EXPERTS), f32)
    for k in range(TOP_K):
        g_out = jnp.where(slot == k, ex[k] / denom, g_out)
        sel = sel + onehots[k].astype(f32)
    r = lax.broadcasted_iota(i32, (tm, tm), 0)
    c = lax.broadcasted_iota(i32, (tm, tm), 1)
    before = (c < r).astype(bf16)
    rank = _dot(before, sel.astype(bf16)) + cnt_o[...]
    r_out = jnp.zeros((tm, TOP_K), f32)
    for k in range(TOP_K):
        rk = jnp.sum(jnp.where(onehots[k], rank, 0.0), axis=-1, keepdims=True)
        r_out = jnp.where(slot == k, rk, r_out)
    e_o[...] = e_out
    g_o[...] = g_out
    r_o[...] = r_out.astype(i32)
    cnt_o[...] += jnp.sum(sel, axis=0, keepdims=True)


def _route(logits):
    n = logits.shape[0]
    tm = ROW_TILE
    row = lambda w: pl.BlockSpec((tm, w), lambda i: (i, 0))
    return pl.pallas_call(
        _route_body,
        grid=(n // tm,),
        in_specs=[row(N_EXPERTS)],
        out_specs=[row(TOP_K), row(TOP_K), row(TOP_K), pl.BlockSpec((1, N_EXPERTS), lambda i: (0, 0))],
        out_shape=[jax.ShapeDtypeStruct((n, TOP_K), i32), jax.ShapeDtypeStruct((n, TOP_K), f32),
                   jax.ShapeDtypeStruct((n, TOP_K), i32), jax.ShapeDtypeStruct((1, N_EXPERTS), f32)],
        compiler_params=_cparams("arbitrary"),
        name="route",
    )(logits)


def _row_copy(src_hbm, row, dst, slot, j, sem):
    return pltpu.make_async_copy(src_hbm.at[pl.ds(row, 1), :], dst.at[slot, pl.ds(j, 1), :], sem.at[slot])


def _experts_body(blk_e_ref, row_tok_ref, used_ref, x_hbm, wg_ref, bg_ref, wu_ref, bu_ref, wd_ref, bd_ref, y_o,
                  xbuf, xb, wbf, sem):
    b = pl.program_id(0)
    bm = MOE_BLOCK
    last = used_ref[0] - 1
    slot = b % 2

    def wait_block(s):
        pltpu.make_async_copy(x_hbm.at[pl.ds(0, bm), :], xbuf.at[s], sem.at[s]).wait()

    @pl.when(b == 0)
    def _():
        def issue(j, carry):
            _row_copy(x_hbm, row_tok_ref[j], xbuf, 0, j, sem).start()
            return carry
        lax.fori_loop(0, bm, issue, 0)

    @pl.when(b <= last)
    def _():
        wait_block(slot)
        xb[...] = xbuf[slot].astype(bf16)
        nxt = jnp.minimum(b + 1, last) * bm
        for s in range(2):
            @pl.when(slot != s)
            def _(s=s):
                for j in range(bm):
                    _row_copy(x_hbm, row_tok_ref[nxt + j], xbuf, s, j, sem).start()

        @pl.when(jnp.logical_or(b == 0, blk_e_ref[b] != blk_e_ref[jnp.maximum(b - 1, 0)]))
        def _():
            wbf[0] = wg_ref[...].astype(bf16)
            wbf[1] = wu_ref[...].astype(bf16)
            wbf[2] = wd_ref[...].astype(bf16)

        x = xb[...]
        g = _dot(x, wbf[0]) + bg_ref[...]
        u = _dot(x, wbf[1]) + bu_ref[...]
        g = jnp.minimum(g, SWIGLU_LIMIT)
        u = jnp.clip(u, -SWIGLU_LIMIT, SWIGLU_LIMIT)
        hdn = (u + 1.0) * (g * _sigmoid(g * SWIGLU_ALPHA))
        y_o[...] = _dot(hdn.astype(bf16), wbf[2]) + bd_ref[...]

    @pl.when(b == last)
    def _():
        wait_block(1 - slot)

    @pl.when(b > last)
    def _():
        y_o[...] = jnp.zeros(y_o.shape, f32)


def _experts(x, blk_e, row_tok, n_used, layer, ew):
    nblk = blk_e.shape[0]
    bm = MOE_BLOCK
    wspec = lambda: pl.BlockSpec((None, None, D_MODEL, D_EXPERT), lambda b, be, rt, nu: (layer, be[b], 0, 0))
    bspec = lambda: pl.BlockSpec((None, None, 1, D_EXPERT), lambda b, be, rt, nu: (layer, be[b], 0, 0))
    grid_spec = pltpu.PrefetchScalarGridSpec(
        num_scalar_prefetch=3,
        grid=(nblk,),
        in_specs=[pl.BlockSpec(memory_space=pl.ANY), wspec(), bspec(), wspec(), bspec(), wspec(), bspec()],
        out_specs=pl.BlockSpec((bm, D_MODEL), lambda b, be, rt, nu: (b, 0)),
        scratch_shapes=[pltpu.VMEM((2, bm, D_MODEL), f32), pltpu.VMEM((bm, D_MODEL), bf16),
                        pltpu.VMEM((3, D_MODEL, D_EXPERT), bf16), pltpu.SemaphoreType.DMA((2,))],
    )
    return pl.pallas_call(
        _experts_body,
        grid_spec=grid_spec,
        out_shape=jax.ShapeDtypeStruct((nblk * bm, D_MODEL), f32),
        compiler_params=_cparams("arbitrary"),
        name="experts",
    )(blk_e, row_tok, n_used, x, ew['e_w_gate'], ew['e_b_gate'], ew['e_w_up'], ew['e_b_up'],
      ew['e_w_down'], ew['e_b_down'])


def _combine_body(dest_ref, y_hbm, h_ref, g_ref, o_ref, ybuf, sem):
    i = pl.program_id(0)
    nt = pl.num_programs(0)
    tt = COMBINE_TILE

    def row_copy(tile, slot, j, k):
        return pltpu.make_async_copy(y_hbm.at[pl.ds(dest_ref[(tile * tt + j) * TOP_K + k], 1), :],
                                     ybuf.at[slot, k, pl.ds(j, 1), :], sem.at[slot])

    def wait_tile(slot):
        for k in range(TOP_K):
            pltpu.make_async_copy(y_hbm.at[pl.ds(0, tt), :], ybuf.at[slot, k], sem.at[slot]).wait()

    slot = i % 2

    @pl.when(i == 0)
    def _():
        def issue(j, carry):
            for k in range(TOP_K):
                row_copy(0, 0, j, k).start(priority=k % 2)
            return carry
        lax.fori_loop(0, tt, issue, 0)

    for s in range(2):
        @pl.when(jnp.logical_and(i + 1 < nt, slot != s))
        def _(s=s):
            for j in range(tt):
                for k in range(TOP_K):
                    row_copy(i + 1, s, j, k).start(priority=k % 2)

    wait_tile(slot)
    g = g_ref[...]
    acc = h_ref[...]
    for k in range(TOP_K):
        acc = acc + ybuf[slot, k] * g[:, k:k + 1]
    o_ref[...] = acc


def _combine(y_rows, dest, h, gates):
    n = h.shape[0]
    tt = COMBINE_TILE
    grid_spec = pltpu.PrefetchScalarGridSpec(
        num_scalar_prefetch=1,
        grid=(n // tt,),
        in_specs=[pl.BlockSpec(memory_space=pl.ANY),
                  pl.BlockSpec((tt, D_MODEL), lambda i, d: (i, 0)),
                  pl.BlockSpec((tt, TOP_K), lambda i, d: (i, 0))],
        out_specs=pl.BlockSpec((tt, D_MODEL), lambda i, d: (i, 0)),
        scratch_shapes=[pltpu.VMEM((2, TOP_K, tt, D_MODEL), f32), pltpu.SemaphoreType.DMA((2,))],
    )
    return pl.pallas_call(
        _combine_body,
        grid_spec=grid_spec,
        out_shape=jax.ShapeDtypeStruct((n, D_MODEL), f32),
        compiler_params=_cparams("arbitrary"),
        name="combine",
    )(dest, y_rows, h, gates)


def _moe(h, xn, logits, layer, ew):
    n = h.shape[0]
    bm = MOE_BLOCK
    top_e, gates, rank, counts = _route(logits)
    counts = counts[0].astype(i32)
    padded = ((counts + bm - 1) // bm) * bm
    pend = jnp.cumsum(padded)
    pstart = pend - padded
    expert = jnp.arange(N_EXPERTS, dtype=i32)
    dest = jnp.sum(jnp.where(top_e[:, :, None] == expert, pstart, 0), axis=-1) + rank
    nblk = (n * TOP_K) // bm + N_EXPERTS
    tok = jnp.broadcast_to(jnp.arange(n, dtype=i32)[:, None], (n, TOP_K))
    row_tok = jnp.zeros((nblk * bm,), i32).at[dest.reshape(-1)].set(tok.reshape(-1))
    blk_start = jnp.arange(nblk, dtype=i32) * bm
    blk_e = jnp.minimum(jnp.sum((pend[None, :] <= blk_start[:, None]).astype(i32), axis=1), N_EXPERTS - 1)
    y_rows = _experts(xn, blk_e, row_tok, pend[-1:] // bm, layer, ew)
    return _combine(y_rows, dest.reshape(-1), h, gates)


def _rope_tables(pos, tile_rows):
    half = C_ROPE_DIM // 2
    inv = ROPE_THETA ** (-jnp.arange(half, dtype=f32) / half)
    ang = pos.astype(f32)[:, None] * inv[None, :]
    cos, sin = jnp.cos(ang), jnp.sin(ang)
    n = ang.shape[0]
    tail = jnp.zeros((n, LANES - C_NOPE_DIM - C_ROPE_DIM), f32)
    cos_t = jnp.concatenate([jnp.ones((n, C_NOPE_DIM), f32), cos, cos, tail], axis=1)
    sin_t = jnp.concatenate([jnp.zeros((n, C_NOPE_DIM), f32), -sin, sin, tail], axis=1)
    reps = max(1, tile_rows // n)
    return jnp.tile(cos_t, (reps, 1)), jnp.tile(sin_t, (reps, 1))


def _band_bias(rel_bias, rows):
    cols = A_PAST + rows
    period = cols + rows
    x = jnp.arange(period)
    x = jnp.where(x < cols, x, x - period)
    idx = jnp.clip(A_PAST - x, -A_REL_CLIP, A_REL_CLIP) + A_REL_CLIP
    v = rel_bias[:, idx].astype(f32) * LOG2E
    skew = jnp.tile(v, (1, rows))[:, :rows * (period - 1)].reshape(-1, rows, period - 1)
    r = jnp.arange(rows)[:, None]
    c = jnp.arange(cols)[None, :]
    lo = (r // CHUNK) * CHUNK
    in_band = jnp.logical_and(c >= lo, c < lo + A_BAND)
    return jnp.where(in_band[None], skew[:, :, :cols], NEG)


def _group_matrix(width, groups):
    g = np.zeros((width, width), np.float32)
    for start, size in groups:
        g[start:start + size, start:start + size] = 1.0 / size
    return jnp.asarray(g, bf16)


def _layer_weights(l, p):
    w = p['w_in'][l]
    seg = lambda o, n: w[:, o:o + n]
    zc = lambda n: jnp.zeros((D_MODEL, n), f32)
    o_aq, o_ak, o_av, o_bq, o_bk, o_bv = 0, 384, 768, 1152, 1408, 1664
    o_blr, o_bg, o_cq, o_ckv, o_ckr = 2048, 2064, 2448, 2640, 2768
    w_in = jnp.concatenate([
        seg(o_aq, 384), seg(o_ak, 384), seg(o_av, 384), seg(o_bq, 256), seg(o_bk, 256), seg(o_bv, 384),
        seg(o_bg, 384), seg(o_cq, C_Q_RANK), zc(P_CKV - P_CQ - C_Q_RANK), seg(o_ckv, 128),
        zc(C_NOPE_DIM), seg(o_ckr, 32), seg(o_blr, 16), zc(IN_P - P_BLR - B_GATE_RANK)], axis=1).astype(bf16)
    hq = C_NOPE_DIM + C_ROPE_DIM
    pad_heads = lambda a, n: jnp.pad(a.reshape(a.shape[0], C_HEADS, n), ((0, 0), (0, 0), (0, C_PAD_DIM - n))
                                     ).reshape(a.shape[0], C_QW)
    wuq = pad_heads(p['c_w_uq'][l], hq).astype(bf16)
    wukv = p['c_w_ukv'][l].reshape(C_KV_RANK, C_HEADS, C_NOPE_DIM + C_V_DIM)
    wukv = jnp.concatenate([pad_heads(wukv[:, :, :C_NOPE_DIM].reshape(C_KV_RANK, -1), C_NOPE_DIM),
                            pad_heads(wukv[:, :, C_NOPE_DIM:].reshape(C_KV_RANK, -1), C_V_DIM)], axis=1).astype(bf16)
    r2 = lambda a: a.reshape(1, -1)
    z = lambda n: jnp.zeros((n,), f32)
    tail = C_PAD_DIM - hq
    head_groups = [(hd * C_PAD_DIM + o, n) for hd in range(C_HEADS) for o, n in ((0, C_NOPE_DIM), (C_NOPE_DIM, C_ROPE_DIM))]
    place = np.zeros((C_ROPE_DIM, C_QW), np.float32)
    for hd in range(C_HEADS):
        place[np.arange(C_ROPE_DIM), hd * C_PAD_DIM + C_NOPE_DIM + np.arange(C_ROPE_DIM)] = 1.0
    rw = p['router_w'][l]
    rw_hi = rw.astype(bf16)
    return {
        'ln1_g': r2(p['ln1_g'][l]), 'ln2_g': r2(p['ln2_g'][l]), 'w_in': w_in,
        'g_a': _group_matrix(A_W, [(hd * A_HEAD_DIM, A_HEAD_DIM) for hd in range(A_HEADS)]),
        'a_q_g': r2(jnp.tile(p['a_q_g'][l], A_HEADS)), 'a_k_g': r2(jnp.tile(p['a_k_g'][l], A_HEADS)),
        'bias_p': _band_bias(p['a_rel_bias'][l], BAND_ROWS), 'bias_s': _band_bias(p['a_rel_bias'][l], CHUNK),
        'b_gate_w2': p['b_gate_w2'][l], 'b_gate_b': r2(p['b_gate_b'][l]), 'b_out_g': r2(p['b_out_g'][l]),
        'c_qa_g': r2(p['c_qa_g'][l]), 'c_w_uq': wuq, 'c_kva_g': r2(p['c_kva_g'][l]), 'c_w_ukv': wukv,
        'g_c': _group_matrix(C_QW, head_groups),
        'c_q_gain': r2(jnp.tile(jnp.concatenate([p['c_qn_g'][l], p['c_qr_g'][l], z(tail)]), C_HEADS)),
        'c_k_gain': r2(jnp.tile(jnp.concatenate([p['c_kn_g'][l], z(C_PAD_DIM - C_NOPE_DIM)]), C_HEADS)),
        'c_kr_gain': r2(jnp.concatenate([z(C_NOPE_DIM), p['c_kr_g'][l], z(tail)])),
        'kr_place': jnp.asarray(place, bf16),
        'w_out': p['w_out'][l].astype(bf16),
        'router_w_hi': rw_hi, 'router_w_lo': (rw - rw_hi.astype(f32)).astype(bf16), 'router_b': r2(p['router_b'][l]),
    }


def _token_mix(h_all, row0, b_, t, lw, cos, sin, past):
    aq, ak, av, akb, avb, slab, cq, ckv, ckr = _proj(h_all, row0, b_ * t, lw, cos, sin)
    r3 = lambda a: a.reshape(b_, t, a.shape[-1])
    ck, cv = _mla_keys(ckv, ckr, lw)
    if past is None:
        a_o = _band(r3(aq), r3(akb), r3(avb), None, None, lw['bias_p'])
        s0_t = jnp.zeros((b_, B_HEADS, B_VAL_DIM, B_KEY_DIM), f32)
        c_o = _mla_causal(r3(cq), r3(ck), r3(cv))
    else:
        a_k, a_v, b_s, c_kv, c_kr = past
        lp = a_k.shape[1]
        v_past = jnp.concatenate([a_v.astype(bf16), jnp.ones(a_v.shape[:-1] + (1,), bf16),
                                  jnp.zeros(a_v.shape[:-1] + (V_EXT - A_HEAD_DIM - 1,), bf16)], axis=-1)
        a_o = _band(r3(aq), r3(akb), r3(avb), a_k.reshape(b_, lp, A_W).astype(bf16),
                    v_past.reshape(b_, lp, A_VW), lw['bias_s'])
        s0_t = jnp.swapaxes(b_s, -1, -2)
        c_o = _mla_sample(r3(cq), c_kv, c_kr, r3(ck), r3(cv), lw)
    b_o, s_t = _gla(r3(slab), s0_t, lw['b_out_g'])
    flat = lambda a: a.reshape(b_ * t, a.shape[-1])
    la = min(A_PAST, t)
    state = (r3(ak)[:, t - la:].reshape(b_, la, A_HEADS, A_HEAD_DIM),
             r3(av)[:, t - la:].reshape(b_, la, A_HEADS, A_HEAD_DIM),
             jnp.swapaxes(s_t, -1, -2), r3(ckv), r3(ckr))
    return (flat(a_o), flat(b_o), flat(c_o)), state


def kernel(x_prompt, x_sample, cache_a_k, cache_a_v, state_b_s, cache_c_kv, cache_c_kr, ln1_g, ln2_g, w_in, a_q_g, a_k_g, a_rel_bias, b_gate_w2, b_gate_b, b_out_g, c_qa_g, c_w_uq, c_kva_g, c_w_ukv, c_qn_g, c_qr_g, c_kn_g, c_kr_g, w_out, router_w, router_b, e_w_gate, e_b_gate, e_w_up, e_b_up, e_w_down, e_b_down):
    params = dict(ln1_g=ln1_g, ln2_g=ln2_g, w_in=w_in, a_q_g=a_q_g, a_k_g=a_k_g, a_rel_bias=a_rel_bias,
                  b_gate_w2=b_gate_w2, b_gate_b=b_gate_b, b_out_g=b_out_g, c_qa_g=c_qa_g, c_w_uq=c_w_uq,
                  c_kva_g=c_kva_g, c_w_ukv=c_w_ukv, c_qn_g=c_qn_g, c_qr_g=c_qr_g, c_kn_g=c_kn_g,
                  c_kr_g=c_kr_g, w_out=w_out, router_w=router_w, router_b=router_b)
    ew = dict(e_w_gate=e_w_gate, e_b_gate=e_b_gate[:, :, None, :], e_w_up=e_w_up, e_b_up=e_b_up[:, :, None, :],
              e_w_down=e_w_down, e_b_down=e_b_down[:, :, None, :])
    bp, tp, _ = x_prompt.shape
    bs, ts, _ = x_sample.shape
    depth = w_in.shape[0]
    n_p = bp * tp
    cos_p, sin_p = _rope_tables(jnp.arange(tp), ROW_TILE)
    cos_s, sin_s = _rope_tables(PAST_LEN + jnp.arange(ts), ROW_TILE)
    h = jnp.concatenate([x_prompt.reshape(n_p, D_MODEL), x_sample.reshape(bs * ts, D_MODEL)])
    st_p, st_s = [], []
    for l in range(depth):
        lw = _layer_weights(l, params)
        mix_p, s_p = _token_mix(h, 0, bp, tp, lw, cos_p, sin_p, None)
        mix_s, s_s = _token_mix(h, n_p, bs, ts, lw, cos_s, sin_s,
                                (cache_a_k[l], cache_a_v[l], state_b_s[l], cache_c_kv[l], cache_c_kr[l]))
        h, xn, logits = _outproj(mix_p, mix_s, h, lw)
        h = _moe(h, xn, logits, l, ew)
        st_p.append(s_p)
        st_s.append(s_s)
    stack = lambda sts, i: jnp.stack([s[i] for s in sts])
    return (h[:n_p].reshape(bp, tp, D_MODEL), h[n_p:].reshape(bs, ts, D_MODEL),
            stack(st_p, 0), stack(st_p, 1), stack(st_p, 2), stack(st_p, 3), stack(st_p, 4),
            stack(st_s, 0), stack(st_s, 1), stack(st_s, 2), stack(st_s, 3), stack(st_s, 4))
```

```python
import functools
import math

import jax
import jax.numpy as jnp
import numpy as np
from jax import lax
from jax.experimental import pallas as pl
from jax.experimental.pallas import tpu as pltpu

f32 = jnp.float32
bf16 = jnp.bfloat16
i32 = jnp.int32

D_MODEL = 1024
PAST_LEN = 2048
CHUNK = 64
A_HEADS = 6
A_HEAD_DIM = 64
A_BAND_CHUNKS = 8
A_PAST = A_BAND_CHUNKS * CHUNK
A_BAND = A_PAST + CHUNK
A_REL_CLIP = 128
B_HEADS = 4
B_KEY_DIM = 64
B_VAL_DIM = 96
B_GATE_RANK = 16
B_GATE_TAU = 16.0
C_HEADS = 4
C_Q_RANK = 192
C_KV_RANK = 128
C_NOPE_DIM = 64
C_ROPE_DIM = 32
C_V_DIM = 64
ROPE_THETA = 10000.0
N_EXPERTS = 32
TOP_K = 4
D_EXPERT = 1024
SWIGLU_LIMIT = 7.0
SWIGLU_ALPHA = 1.702
EPS = 1e-6
NEG = -1e30
LOG2E = 1.0 / math.log(2.0)

LANES = 128
ROW_TILES = D_MODEL // LANES
C_PAD_DIM = LANES
V_EXT = LANES
A_QSCALE = A_HEAD_DIM ** -0.5 * LOG2E
C_QSCALE = (C_NOPE_DIM + C_ROPE_DIM) ** -0.5 * LOG2E

A_W = A_HEADS * A_HEAD_DIM
B_W = B_HEADS * B_VAL_DIM
C_W = C_HEADS * C_V_DIM
B_KW = B_HEADS * B_KEY_DIM
C_QW = C_HEADS * C_PAD_DIM
A_VW = A_HEADS * V_EXT
C_VW = C_HEADS * V_EXT

P_AQ, P_AK, P_AV = 0, 384, 768
P_BQ, P_BK, P_BV, P_BG = 1152, 1408, 1664, 2048
P_CQ, P_CKV, P_CKR, P_BLR = 2432, 2688, 2816, 2912
IN_P = 2944
S_BQ, S_BK, S_BV, S_LA, S_BG, S_W = 0, 256, 512, 896, 1152, 1536

ROW_TILE = 512
BAND_ROWS = 256
GLA_CHUNKS = 4
MLA_TILE = 512
MOE_BLOCK = 256
COMBINE_TILE = 128
ROW_TOK_CHUNK = 4096
VMEM_LIMIT = 56 * 1024 * 1024


def _cparams(*sem):
    return pltpu.CompilerParams(dimension_semantics=sem, vmem_limit_bytes=VMEM_LIMIT)


def _rms(x, g):
    return x * lax.rsqrt(jnp.mean(x * x, axis=-1, keepdims=True) + EPS) * g


def _rope_lanes(x, cos_t, sin_t):
    w = x.shape[1]
    lane = lax.broadcasted_iota(i32, x.shape, 1) % LANES
    first_half = jnp.logical_and(lane >= C_NOPE_DIM, lane < C_NOPE_DIM + C_ROPE_DIM // 2)
    swapped = jnp.where(first_half, pltpu.roll(x, w - C_ROPE_DIM // 2, 1), pltpu.roll(x, C_ROPE_DIM // 2, 1))
    return x * cos_t + swapped * sin_t


def _dot(a, b):
    return jnp.dot(a, b, preferred_element_type=f32)


def _dot_nt(a, b):
    return lax.dot_general(a, b, (((1,), (1,)), ((), ())), preferred_element_type=f32)


def _dot_tn(a, b):
    return lax.dot_general(a, b, (((0,), (0,)), ((), ())), preferred_element_type=f32)


def _sigmoid(x):
    return 1.0 / (1.0 + jnp.exp(-x))


def _ones_column(rows):
    lane = lax.broadcasted_iota(i32, (rows, V_EXT - C_V_DIM), 1)
    return jnp.where(lane == 0, 1.0, 0.0).astype(bf16)


def _group_mean_sq(x, g):
    sq = x * x
    hi = sq.astype(bf16)
    lo = (sq - hi.astype(f32)).astype(bf16)
    return _dot(hi, g) + _dot(lo, g)


def _group_rms(x, g, gain):
    return x * lax.rsqrt(_group_mean_sq(x, g) + EPS) * gain


def _softmax_pv(s, v_blocks, widths):
    p = jnp.exp2(s - jnp.max(s, axis=-1, keepdims=True)).astype(bf16)
    acc, lo = None, 0
    for v, w in zip(v_blocks, widths):
        part = _dot(p[:, lo:lo + w], v)
        acc = part if acc is None else acc + part
        lo += w
    return acc[:, :C_V_DIM] / acc[:, C_V_DIM:C_V_DIM + 1]


def _proj_body(x_ref, g1_ref, w_ref, ga_ref, aqg_ref, akg_ref, w2_ref, gb_ref, qag_ref, wuq_ref,
               gc_ref, cqg_ref, kvag_ref, krg_ref, cos_ref, sin_ref,
               aq_o, ak_o, av_o, akb_o, avb_o, b_o, cq_o, ckv_o, ckr_o, h_sc):
    tm = x_ref.shape[0]
    xn = _rms(x_ref[...], g1_ref[...])
    h_sc[...] = _dot(xn.astype(bf16), w_ref[...])
    cos_t = cos_ref[...]
    sin_t = sin_ref[...]
    q = _group_rms(h_sc[:, P_AQ:P_AQ + A_W], ga_ref[...], aqg_ref[...])
    aq_o[...] = (q * A_QSCALE).astype(bf16)
    k = _group_rms(h_sc[:, P_AK:P_AK + A_W], ga_ref[...], akg_ref[...])
    ak_o[...] = k
    akb_o[...] = k.astype(bf16)
    av_o[...] = h_sc[:, P_AV:P_AV + A_W]
    ones = _ones_column(tm)
    for hd in range(A_HEADS):
        lo = P_AV + hd * A_HEAD_DIM
        avb_o[:, hd * V_EXT:hd * V_EXT + A_HEAD_DIM] = h_sc[:, lo:lo + A_HEAD_DIM].astype(bf16)
        avb_o[:, hd * V_EXT + A_HEAD_DIM:(hd + 1) * V_EXT] = ones
    b_o[:, S_BQ:S_LA] = h_sc[:, P_BQ:P_BG]
    z = jnp.dot(h_sc[:, P_BLR:P_BLR + B_GATE_RANK], w2_ref[...], preferred_element_type=f32,
                precision=lax.Precision.HIGHEST) + gb_ref[...]
    log_sig = jnp.minimum(z, 0.0) - jnp.log1p(jnp.exp(-jnp.abs(z)))
    b_o[:, S_LA:S_BG] = log_sig * (1.0 / B_GATE_TAU)
    b_o[:, S_BG:S_W] = h_sc[:, P_BG:P_BG + B_W]
    cql = _rms(h_sc[:, P_CQ:P_CQ + C_Q_RANK], qag_ref[...])
    cq = _group_rms(_dot(cql.astype(bf16), wuq_ref[...]), gc_ref[...], cqg_ref[...])
    cos4 = jnp.concatenate([cos_t] * C_HEADS, axis=1)
    sin4 = jnp.concatenate([sin_t] * C_HEADS, axis=1)
    cq_o[...] = (_rope_lanes(cq, cos4, sin4) * C_QSCALE).astype(bf16)
    ckv_o[...] = _rms(h_sc[:, P_CKV:P_CKV + C_KV_RANK], kvag_ref[...])
    kr = _group_rms(h_sc[:, P_CKR:P_CKR + LANES], gc_ref[0:LANES, 0:LANES], krg_ref[...])
    ckr_o[...] = _rope_lanes(kr, cos_t, sin_t)[:, C_NOPE_DIM:C_NOPE_DIM + C_ROPE_DIM]


def _proj(h_all, row0, n, lw, cos, sin):
    tm = ROW_TILE
    blk0 = row0 // tm
    npos = cos.shape[0] // tm
    row = lambda w: pl.BlockSpec((tm, w), lambda i: (i, 0))
    const = lambda a: pl.BlockSpec(a.shape, lambda i: (0,) * a.ndim)
    pos = pl.BlockSpec((tm, LANES), lambda i: (i % npos, 0))
    consts = (lw['ln1_g'], lw['w_in'], lw['g_a'], lw['a_q_g'], lw['a_k_g'], lw['b_gate_w2'], lw['b_gate_b'],
              lw['c_qa_g'], lw['c_w_uq'], lw['g_c'], lw['c_q_gain'], lw['c_kva_g'], lw['c_kr_gain'])
    out_w = ((A_W, bf16), (A_W, f32), (A_W, f32), (A_W, bf16), (A_VW, bf16), (S_W, f32),
             (C_QW, bf16), (C_KV_RANK, f32), (C_ROPE_DIM, f32))
    return pl.pallas_call(
        _proj_body,
        grid=(n // tm,),
        in_specs=[pl.BlockSpec((tm, D_MODEL), lambda i: (i + blk0, 0))] + [const(a) for a in consts] + [pos, pos],
        out_specs=[row(w) for w, _ in out_w],
        out_shape=[jax.ShapeDtypeStruct((n, w), dt) for w, dt in out_w],
        scratch_shapes=[pltpu.VMEM((tm, IN_P), f32)],
        compiler_params=_cparams("parallel"),
        name="proj",
    )(h_all, *consts, cos, sin)


def _expand_keys(kv, kr, g_c, k_gain, place, k_dst, v_dst, rows):
    kn = _group_rms(kv[:, :C_QW], g_c, k_gain)
    k_dst[rows, :] = (kn + _dot(kr, place)).astype(bf16)
    v = kv[:, C_QW:]
    lane = lax.broadcasted_iota(i32, v.shape, 1) % LANES
    v_dst[rows, :] = jnp.where(lane == C_V_DIM, 1.0, v).astype(bf16)


def _mla_keys_body(ckv_ref, ckr_ref, w_ref, gc_ref, kg_ref, pl_ref, ck_o, cv_o):
    kv = _dot(ckv_ref[...].astype(bf16), w_ref[...])
    _expand_keys(kv, ckr_ref[...].astype(bf16), gc_ref[...], kg_ref[...], pl_ref[...], ck_o, cv_o, slice(None))


def _mla_keys(ckv, ckr, lw):
    n = ckv.shape[0]
    tm = ROW_TILE
    row = lambda w: pl.BlockSpec((tm, w), lambda i: (i, 0))
    const = lambda a: pl.BlockSpec(a.shape, lambda i: (0,) * a.ndim)
    consts = (lw['c_w_ukv'], lw['g_c'], lw['c_k_gain'], lw['kr_place'])
    return pl.pallas_call(
        _mla_keys_body,
        grid=(n // tm,),
        in_specs=[row(C_KV_RANK), row(C_ROPE_DIM)] + [const(a) for a in consts],
        out_specs=[row(C_QW), row(C_VW)],
        out_shape=[jax.ShapeDtypeStruct((n, C_QW), bf16), jax.ShapeDtypeStruct((n, C_VW), bf16)],
        compiler_params=_cparams("parallel"),
        name="mla_keys",
    )(ckv, ckr, *consts)


def _band_body(q_ref, *refs, n_kb, past_from_seq):
    k_refs, v_refs = refs[:n_kb], refs[n_kb:2 * n_kb]
    bias_ref, o_ref = refs[2 * n_kb], refs[2 * n_kb + 1]
    rows = q_ref.shape[0]
    widths = [k.shape[0] for k in k_refs]
    if past_from_seq:
        col = lax.broadcasted_iota(i32, (rows, sum(widths)), 1)
        exists = col + (pl.program_id(1) * rows - A_PAST) >= 0
    for hd in range(A_HEADS):
        qs = slice(hd * A_HEAD_DIM, (hd + 1) * A_HEAD_DIM)
        vs = slice(hd * V_EXT, (hd + 1) * V_EXT)
        q = q_ref[:, qs]
        s = jnp.concatenate([_dot_nt(q, k[:, qs]) for k in k_refs], axis=1) + bias_ref[hd]
        if past_from_seq:
            s = jnp.where(exists, s, NEG)
        o_ref[:, qs] = _softmax_pv(s, [v[:, vs] for v in v_refs], widths).astype(bf16)


def _band(q, k, v, k_past, v_past, bias):
    b_, t, _ = q.shape
    from_seq = k_past is None
    rows = BAND_ROWS if from_seq else t
    n_back = A_PAST // rows if from_seq else 1
    cur = lambda w: pl.BlockSpec((None, rows, w), lambda b, i: (b, i, 0))
    if from_seq:
        back = lambda w, d: pl.BlockSpec((None, rows, w), lambda b, i: (b, jnp.maximum(i - d, 0), 0))
        k_specs = [back(A_W, d) for d in range(n_back, 0, -1)] + [cur(A_W)]
        v_specs = [back(A_VW, d) for d in range(n_back, 0, -1)] + [cur(A_VW)]
        k_args, v_args = [k] * (n_back + 1), [v] * (n_back + 1)
    else:
        past = lambda w: pl.BlockSpec((None, A_PAST, w), lambda b, i: (b, 0, 0))
        k_specs, v_specs = [past(A_W), cur(A_W)], [past(A_VW), cur(A_VW)]
        k_args, v_args = [k_past, k], [v_past, v]
    return pl.pallas_call(
        functools.partial(_band_body, n_kb=len(k_specs), past_from_seq=from_seq),
        grid=(b_, t // rows),
        in_specs=[cur(A_W)] + k_specs + v_specs + [pl.BlockSpec(bias.shape, lambda b, i: (0, 0, 0))],
        out_specs=cur(A_W),
        out_shape=jax.ShapeDtypeStruct((b_, t, A_W), bf16),
        compiler_params=_cparams("parallel", "parallel"),
        name="band",
    )(q, *k_args, *v_args, bias)


def _gla_body(b_ref, s0_ref, gain_ref, o_ref, st_ref, *, chunks):
    @pl.when(pl.program_id(1) == 0)
    def _():
        st_ref[...] = s0_ref[...]

    rows = chunks * CHUNK
    r = lax.broadcasted_iota(i32, (rows, rows), 0)
    c = lax.broadcasted_iota(i32, (rows, rows), 1)
    causal = jnp.logical_and(r >= c, r // CHUNK == c // CHUNK)
    tril = (lax.broadcasted_iota(i32, (CHUNK, CHUNK), 0) >= lax.broadcasted_iota(i32, (CHUNK, CHUNK), 1)).astype(f32)
    bcum, btot = [], []
    for ch in range(chunks):
        la = b_ref[ch * CHUNK:(ch + 1) * CHUNK, S_LA:S_LA + B_KW]
        cum = jnp.dot(tril, la, preferred_element_type=f32, precision=lax.Precision.HIGHEST)
        bcum.append(cum)
        btot.append(jnp.broadcast_to(cum[CHUNK - 1:CHUNK, :], cum.shape))
    bcum = jnp.concatenate(bcum, axis=0)
    btot = jnp.concatenate(btot, axis=0)
    k = b_ref[:, S_BK:S_BK + B_KW]
    qt = b_ref[:, S_BQ:S_BQ + B_KW] * (B_KEY_DIM ** -0.5) * jnp.exp(bcum)
    kt = k * jnp.exp(-bcum)
    kd = k * jnp.exp(btot - bcum)
    decay = jnp.exp(btot)
    for hd in range(B_HEADS):
        ks = slice(hd * B_KEY_DIM, (hd + 1) * B_KEY_DIM)
        vs = slice(hd * B_VAL_DIM, (hd + 1) * B_VAL_DIM)
        v = b_ref[:, S_BV + vs.start:S_BV + vs.stop]
        att = jnp.where(causal, _dot_nt(qt[:, ks], kt[:, ks]), 0.0)
        o_intra = _dot(att, v)
        st = st_ref[hd]
        outs = []
        for ch in range(chunks):
            rs = slice(ch * CHUNK, (ch + 1) * CHUNK)
            outs.append(o_intra[rs] + _dot_nt(qt[rs, ks], st))
            st = st * decay[ch * CHUNK:ch * CHUNK + 1, ks] + _dot_tn(v[rs], kd[rs, ks])
        st_ref[hd] = st
        g = b_ref[:, S_BG + vs.start:S_BG + vs.stop]
        o = _rms(jnp.concatenate(outs, axis=0), gain_ref[...]) * (g * _sigmoid(g))
        o_ref[:, vs] = o.astype(bf16)


def _gla(slab, s0_t, gain):
    b_, t, _ = slab.shape
    chunks = min(GLA_CHUNKS, t // CHUNK)
    rows = chunks * CHUNK
    st_spec = pl.BlockSpec((None, B_HEADS, B_VAL_DIM, B_KEY_DIM), lambda b, i: (b, 0, 0, 0))
    return pl.pallas_call(
        functools.partial(_gla_body, chunks=chunks),
        grid=(b_, t // rows),
        in_specs=[pl.BlockSpec((None, rows, S_W), lambda b, i: (b, i, 0)), st_spec,
                  pl.BlockSpec(gain.shape, lambda b, i: (0, 0))],
        out_specs=[pl.BlockSpec((None, rows, B_W), lambda b, i: (b, i, 0)), st_spec],
        out_shape=[jax.ShapeDtypeStruct((b_, t, B_W), bf16),
                   jax.ShapeDtypeStruct((b_, B_HEADS, B_VAL_DIM, B_KEY_DIM), f32)],
        compiler_params=_cparams("parallel", "arbitrary"),
        name="gla",
    )(slab, s0_t, gain)


def _mla_causal_body(q_ref, k_ref, v_ref, o_ref, m_sc, acc_sc):
    i = pl.program_id(1)
    t = q_ref.shape[0]
    m_sc[...] = jnp.full(m_sc.shape, NEG, f32)
    acc_sc[...] = jnp.zeros(acc_sc.shape, f32)

    def tile(j, diagonal):
        rows = pl.ds(pl.multiple_of(j * t, t), t)
        if diagonal:
            visible = (lax.broadcasted_iota(i32, (t, t), 1) // CHUNK) <= (lax.broadcasted_iota(i32, (t, t), 0) // CHUNK)
        for hd in range(C_HEADS):
            hs = slice(hd * C_PAD_DIM, (hd + 1) * C_PAD_DIM)
            s = _dot_nt(q_ref[:, hs], k_ref[rows, hs])
            if diagonal:
                s = jnp.where(visible, s, NEG)
            m_old = m_sc[hd]
            m_new = jnp.maximum(m_old, jnp.max(s, axis=-1, keepdims=True))
            p = jnp.exp2(s - jnp.concatenate([m_new] * (t // LANES), axis=1))
            acc_sc[hd] = jnp.exp2(m_old - m_new) * acc_sc[hd] + _dot(p.astype(bf16), v_ref[rows, hs])
            m_sc[hd] = m_new

    def earlier(j, carry):
        tile(j, False)
        return carry

    lax.fori_loop(0, i, earlier, 0)
    tile(i, True)
    for hd in range(C_HEADS):
        acc = acc_sc[hd]
        o_ref[:, hd * C_V_DIM:(hd + 1) * C_V_DIM] = (acc[:, :C_V_DIM] / acc[:, C_V_DIM:C_V_DIM + 1]).astype(bf16)


def _mla_causal(q, k, v):
    b_, t, _ = q.shape
    tq = MLA_TILE
    whole = lambda w: pl.BlockSpec((None, t, w), lambda b, i: (b, 0, 0))
    return pl.pallas_call(
        _mla_causal_body,
        grid=(b_, t // tq),
        in_specs=[pl.BlockSpec((None, tq, C_QW), lambda b, i: (b, i, 0)), whole(C_QW), whole(C_VW)],
        out_specs=pl.BlockSpec((None, tq, C_W), lambda b, i: (b, i, 0)),
        out_shape=jax.ShapeDtypeStruct((b_, t, C_W), bf16),
        scratch_shapes=[pltpu.VMEM((C_HEADS, tq, LANES), f32), pltpu.VMEM((C_HEADS, tq, V_EXT), f32)],
        compiler_params=_cparams("parallel", "arbitrary"),
        name="mla_causal",
    )(q, k, v)


def _mla_sample_body(q_ref, ckv_ref, ckr_ref, kn_ref, vn_ref, w_ref, gc_ref, kg_ref, pl_ref, o_ref, k_sc, v_sc):
    past = ckv_ref.shape[0]
    kv = _dot(ckv_ref[...].astype(bf16), w_ref[...])
    _expand_keys(kv, ckr_ref[...].astype(bf16), gc_ref[...], kg_ref[...], pl_ref[...], k_sc, v_sc, slice(0, past))
    k_sc[past:, :] = kn_ref[...]
    v_sc[past:, :] = vn_ref[...]
    total = k_sc.shape[0]
    for hd in range(C_HEADS):
        hs = slice(hd * C_PAD_DIM, (hd + 1) * C_PAD_DIM)
        s = _dot_nt(q_ref[:, hs], k_sc[:, hs])
        o_ref[:, hd * C_V_DIM:(hd + 1) * C_V_DIM] = _softmax_pv(s, [v_sc[:, hs]], [total]).astype(bf16)


def _mla_sample(q, c_kv, c_kr, k_new, v_new, lw):
    b_, s_len, _ = q.shape
    past = c_kv.shape[1]
    per_b = lambda r, w: pl.BlockSpec((None, r, w), lambda b: (b, 0, 0))
    const = lambda a: pl.BlockSpec(a.shape, lambda b: (0,) * a.ndim)
    consts = (lw['c_w_ukv'], lw['g_c'], lw['c_k_gain'], lw['kr_place'])
    return pl.pallas_call(
        _mla_sample_body,
        grid=(b_,),
        in_specs=[per_b(s_len, C_QW), per_b(past, C_KV_RANK), per_b(past, C_ROPE_DIM), per_b(s_len, C_QW),
                  per_b(s_len, C_VW)] + [const(a) for a in consts],
        out_specs=per_b(s_len, C_W),
        out_shape=jax.ShapeDtypeStruct((b_, s_len, C_W), bf16),
        scratch_shapes=[pltpu.VMEM((past + s_len, C_QW), bf16), pltpu.VMEM((past + s_len, C_VW), bf16)],
        compiler_params=_cparams("parallel"),
        name="mla_sample",
    )(q, c_kv, c_kr, k_new, v_new, *consts)


def _outproj_body(ap_ref, bp_ref, cp_ref, as_ref, bs_ref, cs_ref, h_ref, w_ref, g2_ref, rwh_ref, rwl_ref, rb_ref,
                  h_o, xn_o, lg_o, *, prompt_tiles):
    def run(a_ref, b_ref, c_ref):
        y = _dot(a_ref[...], w_ref[0:A_W, :])
        y += _dot(b_ref[...], w_ref[A_W:A_W + B_W, :])
        y += _dot(c_ref[...], w_ref[A_W + B_W:, :])
        h = h_ref[...] + y
        h_o[...] = h
        xn = _rms(h, g2_ref[...])
        for c in range(ROW_TILES):
            xn_o[pl.ds(c, h.shape[0], stride=ROW_TILES), :] = xn[:, c * LANES:(c + 1) * LANES]
        xh = xn.astype(bf16)
        xl = (xn - xh.astype(f32)).astype(bf16)
        lg_o[...] = _dot(xh, rwh_ref[...]) + _dot(xl, rwh_ref[...]) + _dot(xh, rwl_ref[...]) + rb_ref[...]

    i = pl.program_id(0)
    pl.when(i < prompt_tiles)(functools.partial(run, ap_ref, bp_ref, cp_ref))
    pl.when(i >= prompt_tiles)(functools.partial(run, as_ref, bs_ref, cs_ref))


def _outproj(mix_p, mix_s, h_all, lw):
    n = h_all.shape[0]
    tm = ROW_TILE
    pt = mix_p[0].shape[0] // tm
    row = lambda w: pl.BlockSpec((tm, w), lambda i: (i, 0))
    first = lambda w: pl.BlockSpec((tm, w), lambda i: (jnp.minimum(i, pt - 1), 0))
    second = lambda w: pl.BlockSpec((tm, w), lambda i: (jnp.maximum(i - pt, 0), 0))
    const = lambda a: pl.BlockSpec(a.shape, lambda i: (0,) * a.ndim)
    consts = (lw['w_out'], lw['ln2_g'], lw['router_w_hi'], lw['router_w_lo'], lw['router_b'])
    return pl.pallas_call(
        functools.partial(_outproj_body, prompt_tiles=pt),
        grid=(n // tm,),
        in_specs=[first(A_W), first(B_W), first(C_W), second(A_W), second(B_W), second(C_W), row(D_MODEL)]
        + [const(a) for a in consts],
        out_specs=[row(D_MODEL), pl.BlockSpec((tm * ROW_TILES, LANES), lambda i: (i, 0)), row(N_EXPERTS)],
        out_shape=[jax.ShapeDtypeStruct((n, D_MODEL), f32), jax.ShapeDtypeStruct((n * ROW_TILES, LANES), f32),
                   jax.ShapeDtypeStruct((n, N_EXPERTS), f32)],
        compiler_params=_cparams("parallel"),
        name="outproj",
    )(*mix_p, *mix_s, h_all, *consts)


def _route_body(lg_ref, e_o, g_o, r_o, cnt_o):
    @pl.when(pl.program_id(0) == 0)
    def _():
        cnt_o[...] = jnp.zeros(cnt_o.shape, f32)

    tm = lg_ref.shape[0]
    lane = lax.broadcasted_iota(i32, (tm, N_EXPERTS), 1)
    slot = lax.broadcasted_iota(i32, (tm, TOP_K), 1)
    work = lg_ref[...]
    onehots, vals = [], []
    e_out = jnp.zeros((tm, TOP_K), i32)
    for k in range(TOP_K):
        m = jnp.max(work, axis=-1, keepdims=True)
        idx = jnp.min(jnp.where(work == m, lane, N_EXPERTS), axis=-1, keepdims=True)
        oh = lane == idx
        work = jnp.where(oh, -jnp.inf, work)
        onehots.append(oh)
        vals.append(m)
        e_out = jnp.where(slot == k, idx, e_out)
    ex = [jnp.exp(v - vals[0]) for v in vals]
    denom = ex[0] + ex[1] + ex[2] + ex[3]
    g_out = jnp.zeros((tm, TOP_K), f32)
    sel = jnp.zeros((tm, N_EXPERTS), f32)
    for k in range(TOP_K):
        g_out = jnp.where(slot == k, ex[k] / denom, g_out)
        sel = sel + onehots[k].astype(f32)
    r = lax.broadcasted_iota(i32, (tm, tm), 0)
    c = lax.broadcasted_iota(i32, (tm, tm), 1)
    before = (c < r).astype(bf16)
    rank = _dot(before, sel.astype(bf16)) + cnt_o[...]
    r_out = jnp.zeros((tm, TOP_K), f32)
    for k in range(TOP_K):
        rk = jnp.sum(jnp.where(onehots[k], rank, 0.0), axis=-1, keepdims=True)
        r_out = jnp.where(slot == k, rk, r_out)
    e_o[...] = e_out
    g_o[...] = g_out
    r_o[...] = r_out.astype(i32)
    cnt_o[...] += jnp.sum(sel, axis=0, keepdims=True)


def _route(logits):
    n = logits.shape[0]
    tm = ROW_TILE
    row = lambda w: pl.BlockSpec((tm, w), lambda i: (i, 0))
    return pl.pallas_call(
        _route_body,
        grid=(n // tm,),
        in_specs=[row(N_EXPERTS)],
        out_specs=[row(TOP_K), row(TOP_K), row(TOP_K), pl.BlockSpec((1, N_EXPERTS), lambda i: (0, 0))],
        out_shape=[jax.ShapeDtypeStruct((n, TOP_K), i32), jax.ShapeDtypeStruct((n, TOP_K), f32),
                   jax.ShapeDtypeStruct((n, TOP_K), i32), jax.ShapeDtypeStruct((1, N_EXPERTS), f32)],
        compiler_params=_cparams("arbitrary"),
        name="route",
    )(logits)


def _row_copy(src_hbm, tok, dst, slot, j, sem):
    return pltpu.make_async_copy(src_hbm.at[pl.ds(pl.multiple_of(tok * ROW_TILES, ROW_TILES), ROW_TILES), :],
                                 dst.at[slot, pl.ds(j * ROW_TILES, ROW_TILES), :], sem.at[slot])


def _experts_body(blk_e_ref, row_tok_ref, used_ref, x_hbm, wg_ref, bg_ref, wu_ref, bu_ref, wd_ref, bd_ref, y_o,
                  xbuf, xb, wbf, sem):
    b = pl.program_id(0)
    bm = MOE_BLOCK
    last = used_ref[0] - 1
    slot = b % 2

    def wait_block(s):
        pltpu.make_async_copy(x_hbm.at[pl.ds(0, bm * ROW_TILES), :], xbuf.at[s], sem.at[s]).wait()

    @pl.when(b == 0)
    def _():
        def issue(j, carry):
            _row_copy(x_hbm, row_tok_ref[j], xbuf, 0, j, sem).start()
            return carry
        lax.fori_loop(0, bm, issue, 0)

    @pl.when(b <= last)
    def _():
        wait_block(slot)
        xb[...] = jnp.concatenate([xbuf[slot, pl.ds(c, bm, stride=ROW_TILES), :] for c in range(ROW_TILES)],
                                  axis=1).astype(bf16)
        nxt = jnp.minimum(b + 1, last) * bm
        for s in range(2):
            @pl.when(slot != s)
            def _(s=s):
                for j in range(bm):
                    _row_copy(x_hbm, row_tok_ref[nxt + j], xbuf, s, j, sem).start(priority=j % 2)

        @pl.when(jnp.logical_or(b == 0, blk_e_ref[b] != blk_e_ref[jnp.maximum(b - 1, 0)]))
        def _():
            wbf[0] = wg_ref[...].astype(bf16)
            wbf[1] = wu_ref[...].astype(bf16)
            wbf[2] = wd_ref[...].astype(bf16)

        x = xb[...]
        g = _dot(x, wbf[0]) + bg_ref[...]
        u = _dot(x, wbf[1]) + bu_ref[...]
        g = jnp.minimum(g, SWIGLU_LIMIT)
        u = jnp.clip(u, -SWIGLU_LIMIT, SWIGLU_LIMIT)
        hdn = (u + 1.0) * (g * _sigmoid(g * SWIGLU_ALPHA))
        y_o[...] = _dot(hdn.astype(bf16), wbf[2]) + bd_ref[...]

    @pl.when(b == last)
    def _():
        wait_block(1 - slot)

    @pl.when(b > last)
    def _():
        y_o[...] = jnp.zeros(y_o.shape, f32)


def _experts(x, blk_e, row_tok, n_used, layer, ew):
    nblk = blk_e.shape[0]
    bm = MOE_BLOCK
    wspec = lambda: pl.BlockSpec((None, None, D_MODEL, D_EXPERT), lambda b, be, rt, nu: (layer, be[b], 0, 0))
    bspec = lambda: pl.BlockSpec((None, None, 1, D_EXPERT), lambda b, be, rt, nu: (layer, be[b], 0, 0))
    grid_spec = pltpu.PrefetchScalarGridSpec(
        num_scalar_prefetch=3,
        grid=(nblk,),
        in_specs=[pl.BlockSpec(memory_space=pl.ANY), wspec(), bspec(), wspec(), bspec(), wspec(), bspec()],
        out_specs=pl.BlockSpec((bm, D_MODEL), lambda b, be, rt, nu: (b, 0)),
        scratch_shapes=[pltpu.VMEM((2, bm * ROW_TILES, LANES), f32), pltpu.VMEM((bm, D_MODEL), bf16),
                        pltpu.VMEM((3, D_MODEL, D_EXPERT), bf16), pltpu.SemaphoreType.DMA((2,))],
    )
    return pl.pallas_call(
        _experts_body,
        grid_spec=grid_spec,
        out_shape=jax.ShapeDtypeStruct((nblk * bm, D_MODEL), f32),
        compiler_params=_cparams("arbitrary"),
        name="experts",
    )(blk_e, row_tok, n_used, x, ew['e_w_gate'], ew['e_b_gate'], ew['e_w_up'], ew['e_b_up'],
      ew['e_w_down'], ew['e_b_down'])


def _combine_body(dest_ref, y_hbm, h_ref, g_ref, o_ref, ybuf, sem):
    i = pl.program_id(0)
    nt = pl.num_programs(0)
    tt = COMBINE_TILE

    def row_copy(tile, slot, j, k):
        return pltpu.make_async_copy(y_hbm.at[pl.ds(dest_ref[(tile * tt + j) * TOP_K + k], 1), :],
                                     ybuf.at[slot, k, pl.ds(j, 1), :], sem.at[slot])

    def wait_tile(slot):
        for k in range(TOP_K):
            pltpu.make_async_copy(y_hbm.at[pl.ds(0, tt), :], ybuf.at[slot, k], sem.at[slot]).wait()

    slot = i % 2

    @pl.when(i == 0)
    def _():
        def issue(j, carry):
            for k in range(TOP_K):
                row_copy(0, 0, j, k).start(priority=k % 2)
            return carry
        lax.fori_loop(0, tt, issue, 0)

    for s in range(2):
        @pl.when(jnp.logical_and(i + 1 < nt, slot != s))
        def _(s=s):
            for j in range(tt):
                for k in range(TOP_K):
                    row_copy(i + 1, s, j, k).start(priority=k % 2)

    wait_tile(slot)
    g = g_ref[...]
    acc = h_ref[...]
    for k in range(TOP_K):
        acc = acc + ybuf[slot, k] * g[:, k:k + 1]
    o_ref[...] = acc


def _combine(y_rows, dest, h, gates):
    n = h.shape[0]
    tt = COMBINE_TILE
    grid_spec = pltpu.PrefetchScalarGridSpec(
        num_scalar_prefetch=1,
        grid=(n // tt,),
        in_specs=[pl.BlockSpec(memory_space=pl.ANY),
                  pl.BlockSpec((tt, D_MODEL), lambda i, d: (i, 0)),
                  pl.BlockSpec((tt, TOP_K), lambda i, d: (i, 0))],
        out_specs=pl.BlockSpec((tt, D_MODEL), lambda i, d: (i, 0)),
        scratch_shapes=[pltpu.VMEM((2, TOP_K, tt, D_MODEL), f32), pltpu.SemaphoreType.DMA((2,))],
    )
    return pl.pallas_call(
        _combine_body,
        grid_spec=grid_spec,
        out_shape=jax.ShapeDtypeStruct((n, D_MODEL), f32),
        compiler_params=_cparams("arbitrary"),
        name="combine",
    )(dest, y_rows, h, gates)


def _row_tok_body(dest_hbm, zeros_hbm, out_ref, buf0, buf1, sem, zsem):
    ch = buf0.shape[0]
    nchunks = dest_hbm.shape[0] // ch
    bufs = (buf0, buf1)
    fill = pltpu.make_async_copy(zeros_hbm, out_ref, zsem)
    fill.start()
    fill.wait()

    def copy(c, slot):
        return pltpu.make_async_copy(dest_hbm.at[pl.ds(pl.multiple_of(c * ch, ch), ch)], bufs[slot], sem.at[slot])

    copy(0, 0).start()

    def per_pair(p, carry):
        for slot in range(2):
            c = 2 * p + slot

            @pl.when(c + 1 < nchunks)
            def _():
                copy(c + 1, 1 - slot).start()

            copy(c, slot).wait()
            base = c * (ch // TOP_K)

            def scatter(t, cc):
                for k in range(TOP_K):
                    out_ref[bufs[slot][TOP_K * t + k]] = base + t
                return cc

            lax.fori_loop(0, ch // TOP_K, scatter, 0, unroll=8)
        return carry

    lax.fori_loop(0, nchunks // 2, per_pair, 0)


def _row_tok(dest, rows):
    ch = min(ROW_TOK_CHUNK, dest.shape[0] // 2)
    assert dest.shape[0] % (2 * ch) == 0
    return pl.pallas_call(
        _row_tok_body,
        in_specs=[pl.BlockSpec(memory_space=pl.ANY), pl.BlockSpec(memory_space=pl.ANY)],
        out_specs=pl.BlockSpec(memory_space=pltpu.SMEM),
        out_shape=jax.ShapeDtypeStruct((rows,), i32),
        scratch_shapes=[pltpu.SMEM((ch,), i32), pltpu.SMEM((ch,), i32), pltpu.SemaphoreType.DMA((2,)),
                        pltpu.SemaphoreType.DMA(())],
        name="row_tok",
    )(dest, jnp.zeros((rows,), i32))


def _moe(h, xn, logits, layer, ew):
    n = h.shape[0]
    bm = MOE_BLOCK
    top_e, gates, rank, counts = _route(logits)
    counts = counts[0].astype(i32)
    padded = ((counts + bm - 1) // bm) * bm
    pend = jnp.cumsum(padded)
    pstart = pend - padded
    expert = jnp.arange(N_EXPERTS, dtype=i32)
    dest = jnp.sum(jnp.where(top_e[:, :, None] == expert, pstart, 0), axis=-1) + rank
    nblk = (n * TOP_K) // bm + N_EXPERTS
    row_tok = _row_tok(dest.reshape(-1), nblk * bm)
    blk_start = jnp.arange(nblk, dtype=i32) * bm
    blk_e = jnp.minimum(jnp.sum((pend[None, :] <= blk_start[:, None]).astype(i32), axis=1), N_EXPERTS - 1)
    y_rows = _experts(xn, blk_e, row_tok, pend[-1:] // bm, layer, ew)
    return _combine(y_rows, dest.reshape(-1), h, gates)


def _rope_tables(pos, tile_rows):
    half = C_ROPE_DIM // 2
    inv = ROPE_THETA ** (-jnp.arange(half, dtype=f32) / half)
    ang = pos.astype(f32)[:, None] * inv[None, :]
    cos, sin = jnp.cos(ang), jnp.sin(ang)
    n = ang.shape[0]
    tail = jnp.zeros((n, LANES - C_NOPE_DIM - C_ROPE_DIM), f32)
    cos_t = jnp.concatenate([jnp.ones((n, C_NOPE_DIM), f32), cos, cos, tail], axis=1)
    sin_t = jnp.concatenate([jnp.zeros((n, C_NOPE_DIM), f32), -sin, sin, tail], axis=1)
    reps = max(1, tile_rows // n)
    return jnp.tile(cos_t, (reps, 1)), jnp.tile(sin_t, (reps, 1))


def _band_bias(rel_bias, rows):
    cols = A_PAST + rows
    period = cols + rows
    x = jnp.arange(period)
    x = jnp.where(x < cols, x, x - period)
    idx = jnp.clip(A_PAST - x, -A_REL_CLIP, A_REL_CLIP) + A_REL_CLIP
    v = rel_bias[:, idx].astype(f32) * LOG2E
    skew = jnp.tile(v, (1, rows))[:, :rows * (period - 1)].reshape(-1, rows, period - 1)
    r = jnp.arange(rows)[:, None]
    c = jnp.arange(cols)[None, :]
    lo = (r // CHUNK) * CHUNK
    in_band = jnp.logical_and(c >= lo, c < lo + A_BAND)
    return jnp.where(in_band[None], skew[:, :, :cols], NEG)


def _group_matrix(width, groups):
    g = np.zeros((width, width), np.float32)
    for start, size in groups:
        g[start:start + size, start:start + size] = 1.0 / size
    return jnp.asarray(g, bf16)


def _layer_weights(l, p):
    w = p['w_in'][l]
    seg = lambda o, n: w[:, o:o + n]
    zc = lambda n: jnp.zeros((D_MODEL, n), f32)
    o_aq, o_ak, o_av, o_bq, o_bk, o_bv = 0, 384, 768, 1152, 1408, 1664
    o_blr, o_bg, o_cq, o_ckv, o_ckr = 2048, 2064, 2448, 2640, 2768
    w_in = jnp.concatenate([
        seg(o_aq, 384), seg(o_ak, 384), seg(o_av, 384), seg(o_bq, 256), seg(o_bk, 256), seg(o_bv, 384),
        seg(o_bg, 384), seg(o_cq, C_Q_RANK), zc(P_CKV - P_CQ - C_Q_RANK), seg(o_ckv, 128),
        zc(C_NOPE_DIM), seg(o_ckr, 32), seg(o_blr, 16), zc(IN_P - P_BLR - B_GATE_RANK)], axis=1).astype(bf16)
    hq = C_NOPE_DIM + C_ROPE_DIM
    pad_heads = lambda a, n: jnp.pad(a.reshape(a.shape[0], C_HEADS, n), ((0, 0), (0, 0), (0, C_PAD_DIM - n))
                                     ).reshape(a.shape[0], C_QW)
    wuq = pad_heads(p['c_w_uq'][l], hq).astype(bf16)
    wukv = p['c_w_ukv'][l].reshape(C_KV_RANK, C_HEADS, C_NOPE_DIM + C_V_DIM)
    wukv = jnp.concatenate([pad_heads(wukv[:, :, :C_NOPE_DIM].reshape(C_KV_RANK, -1), C_NOPE_DIM),
                            pad_heads(wukv[:, :, C_NOPE_DIM:].reshape(C_KV_RANK, -1), C_V_DIM)], axis=1).astype(bf16)
    r2 = lambda a: a.reshape(1, -1)
    z = lambda n: jnp.zeros((n,), f32)
    tail = C_PAD_DIM - hq
    head_groups = [(hd * C_PAD_DIM + o, n) for hd in range(C_HEADS) for o, n in ((0, C_NOPE_DIM), (C_NOPE_DIM, C_ROPE_DIM))]
    place = np.zeros((C_ROPE_DIM, C_QW), np.float32)
    for hd in range(C_HEADS):
        place[np.arange(C_ROPE_DIM), hd * C_PAD_DIM + C_NOPE_DIM + np.arange(C_ROPE_DIM)] = 1.0
    rw = p['router_w'][l]
    rw_hi = rw.astype(bf16)
    return {
        'ln1_g': r2(p['ln1_g'][l]), 'ln2_g': r2(p['ln2_g'][l]), 'w_in': w_in,
        'g_a': _group_matrix(A_W, [(hd * A_HEAD_DIM, A_HEAD_DIM) for hd in range(A_HEADS)]),
        'a_q_g': r2(jnp.tile(p['a_q_g'][l], A_HEADS)), 'a_k_g': r2(jnp.tile(p['a_k_g'][l], A_HEADS)),
        'bias_p': _band_bias(p['a_rel_bias'][l], BAND_ROWS), 'bias_s': _band_bias(p['a_rel_bias'][l], CHUNK),
        'b_gate_w2': p['b_gate_w2'][l], 'b_gate_b': r2(p['b_gate_b'][l]), 'b_out_g': r2(p['b_out_g'][l]),
        'c_qa_g': r2(p['c_qa_g'][l]), 'c_w_uq': wuq, 'c_kva_g': r2(p['c_kva_g'][l]), 'c_w_ukv': wukv,
        'g_c': _group_matrix(C_QW, head_groups),
        'c_q_gain': r2(jnp.tile(jnp.concatenate([p['c_qn_g'][l], p['c_qr_g'][l], z(tail)]), C_HEADS)),
        'c_k_gain': r2(jnp.tile(jnp.concatenate([p['c_kn_g'][l], z(C_PAD_DIM - C_NOPE_DIM)]), C_HEADS)),
        'c_kr_gain': r2(jnp.concatenate([z(C_NOPE_DIM), p['c_kr_g'][l], z(tail)])),
        'kr_place': jnp.asarray(place, bf16),
        'w_out': p['w_out'][l].astype(bf16),
        'router_w_hi': rw_hi, 'router_w_lo': (rw - rw_hi.astype(f32)).astype(bf16), 'router_b': r2(p['router_b'][l]),
    }


def _token_mix(h_all, row0, b_, t, lw, cos, sin, past):
    aq, ak, av, akb, avb, slab, cq, ckv, ckr = _proj(h_all, row0, b_ * t, lw, cos, sin)
    r3 = lambda a: a.reshape(b_, t, a.shape[-1])
    ck, cv = _mla_keys(ckv, ckr, lw)
    if past is None:
        a_o = _band(r3(aq), r3(akb), r3(avb), None, None, lw['bias_p'])
        s0_t = jnp.zeros((b_, B_HEADS, B_VAL_DIM, B_KEY_DIM), f32)
        c_o = _mla_causal(r3(cq), r3(ck), r3(cv))
    else:
        a_k, a_v, b_s, c_kv, c_kr = past
        lp = a_k.shape[1]
        v_past = jnp.concatenate([a_v.astype(bf16), jnp.ones(a_v.shape[:-1] + (1,), bf16),
                                  jnp.zeros(a_v.shape[:-1] + (V_EXT - A_HEAD_DIM - 1,), bf16)], axis=-1)
        a_o = _band(r3(aq), r3(akb), r3(avb), a_k.reshape(b_, lp, A_W).astype(bf16),
                    v_past.reshape(b_, lp, A_VW), lw['bias_s'])
        s0_t = jnp.swapaxes(b_s, -1, -2)
        c_o = _mla_sample(r3(cq), c_kv, c_kr, r3(ck), r3(cv), lw)
    b_o, s_t = _gla(r3(slab), s0_t, lw['b_out_g'])
    flat = lambda a: a.reshape(b_ * t, a.shape[-1])
    la = min(A_PAST, t)
    state = (r3(ak)[:, t - la:].reshape(b_, la, A_HEADS, A_HEAD_DIM),
             r3(av)[:, t - la:].reshape(b_, la, A_HEADS, A_HEAD_DIM),
             jnp.swapaxes(s_t, -1, -2), r3(ckv), r3(ckr))
    return (flat(a_o), flat(b_o), flat(c_o)), state


def kernel(x_prompt, x_sample, cache_a_k, cache_a_v, state_b_s, cache_c_kv, cache_c_kr, ln1_g, ln2_g, w_in, a_q_g, a_k_g, a_rel_bias, b_gate_w2, b_gate_b, b_out_g, c_qa_g, c_w_uq, c_kva_g, c_w_ukv, c_qn_g, c_qr_g, c_kn_g, c_kr_g, w_out, router_w, router_b, e_w_gate, e_b_gate, e_w_up, e_b_up, e_w_down, e_b_down):
    params = dict(ln1_g=ln1_g, ln2_g=ln2_g, w_in=w_in, a_q_g=a_q_g, a_k_g=a_k_g, a_rel_bias=a_rel_bias,
                  b_gate_w2=b_gate_w2, b_gate_b=b_gate_b, b_out_g=b_out_g, c_qa_g=c_qa_g, c_w_uq=c_w_uq,
                  c_kva_g=c_kva_g, c_w_ukv=c_w_ukv, c_qn_g=c_qn_g, c_qr_g=c_qr_g, c_kn_g=c_kn_g,
                  c_kr_g=c_kr_g, w_out=w_out, router_w=router_w, router_b=router_b)
    ew = dict(e_w_gate=e_w_gate, e_b_gate=e_b_gate[:, :, None, :], e_w_up=e_w_up, e_b_up=e_b_up[:, :, None, :],
              e_w_down=e_w_down, e_b_down=e_b_down[:, :, None, :])
    bp, tp, _ = x_prompt.shape
    bs, ts, _ = x_sample.shape
    depth = w_in.shape[0]
    n_p = bp * tp
    cos_p, sin_p = _rope_tables(jnp.arange(tp), ROW_TILE)
    cos_s, sin_s = _rope_tables(PAST_LEN + jnp.arange(ts), ROW_TILE)
    h = jnp.concatenate([x_prompt.reshape(n_p, D_MODEL), x_sample.reshape(bs * ts, D_MODEL)])
    st_p, st_s = [], []
    for l in range(depth):
        lw = _layer_weights(l, params)
        mix_p, s_p = _token_mix(h, 0, bp, tp, lw, cos_p, sin_p, None)
        mix_s, s_s = _token_mix(h, n_p, bs, ts, lw, cos_s, sin_s,
                                (cache_a_k[l], cache_a_v[l], state_b_s[l], cache_c_kv[l], cache_c_kr[l]))
        h, xn, logits = _outproj(mix_p, mix_s, h, lw)
        h = _moe(h, xn, logits, l, ew)
        st_p.append(s_p)
        st_s.append(s_s)
    stack = lambda sts, i: jnp.stack([s[i] for s in sts])
    return (h[:n_p].reshape(bp, tp, D_MODEL), h[n_p:].reshape(bs, ts, D_MODEL),
            stack(st_p, 0), stack(st_p, 1), stack(st_p, 2), stack(st_p, 3), stack(st_p, 4),
            stack(st_s, 0), stack(st_s, 1), stack(st_s, 2), stack(st_s, 3), stack(st_s, 4))
```

```python
import functools
import math

import jax
import jax.numpy as jnp
import numpy as np
from jax import lax
from jax.experimental import pallas as pl
from jax.experimental.pallas import tpu as pltpu

f32 = jnp.float32
bf16 = jnp.bfloat16
i32 = jnp.int32

D_MODEL = 1024
PAST_LEN = 2048
CHUNK = 64
A_HEADS = 6
A_HEAD_DIM = 64
A_BAND_CHUNKS = 8
A_PAST = A_BAND_CHUNKS * CHUNK
A_BAND = A_PAST + CHUNK
A_REL_CLIP = 128
B_HEADS = 4
B_KEY_DIM = 64
B_VAL_DIM = 96
B_GATE_RANK = 16
B_GATE_TAU = 16.0
C_HEADS = 4
C_Q_RANK = 192
C_KV_RANK = 128
C_NOPE_DIM = 64
C_ROPE_DIM = 32
C_V_DIM = 64
ROPE_THETA = 10000.0
N_EXPERTS = 32
TOP_K = 4
D_EXPERT = 1024
SWIGLU_LIMIT = 7.0
SWIGLU_ALPHA = 1.702
EPS = 1e-6
NEG = -1e30
LOG2E = 1.0 / math.log(2.0)

LANES = 128
ROW_TILES = D_MODEL // LANES
C_PAD_DIM = LANES
V_EXT = LANES
A_QSCALE = A_HEAD_DIM ** -0.5 * LOG2E
C_QSCALE = (C_NOPE_DIM + C_ROPE_DIM) ** -0.5 * LOG2E

A_W = A_HEADS * A_HEAD_DIM
B_W = B_HEADS * B_VAL_DIM
C_W = C_HEADS * C_V_DIM
B_KW = B_HEADS * B_KEY_DIM
C_QW = C_HEADS * C_PAD_DIM
A_VW = A_HEADS * V_EXT
C_VW = C_HEADS * V_EXT

P_AQ, P_AK, P_AV = 0, 384, 768
P_BQ, P_BK, P_BV, P_BG = 1152, 1408, 1664, 2048
P_CQ, P_CKV, P_CKR, P_BLR = 2432, 2688, 2816, 2912
IN_P = 2944
S_BQ, S_BK, S_BV, S_LA, S_BG, S_W = 0, 256, 512, 896, 1152, 1536

ROW_TILE = 512
BAND_ROWS = 256
GLA_CHUNKS = 4
MLA_TILE = 512
MOE_BLOCK = 512
COMBINE_TILE = 128
ROW_TOK_CHUNK = 4096
VMEM_LIMIT = 56 * 1024 * 1024


def _cparams(*sem):
    return pltpu.CompilerParams(dimension_semantics=sem, vmem_limit_bytes=VMEM_LIMIT)


def _rms(x, g):
    return x * lax.rsqrt(jnp.mean(x * x, axis=-1, keepdims=True) + EPS) * g


def _rope_lanes(x, cos_t, sin_t):
    w = x.shape[1]
    lane = lax.broadcasted_iota(i32, x.shape, 1) % LANES
    first_half = jnp.logical_and(lane >= C_NOPE_DIM, lane < C_NOPE_DIM + C_ROPE_DIM // 2)
    swapped = jnp.where(first_half, pltpu.roll(x, w - C_ROPE_DIM // 2, 1), pltpu.roll(x, C_ROPE_DIM // 2, 1))
    return x * cos_t + swapped * sin_t


def _dot(a, b):
    return jnp.dot(a, b, preferred_element_type=f32)


def _dot_nt(a, b):
    return lax.dot_general(a, b, (((1,), (1,)), ((), ())), preferred_element_type=f32)


def _dot_tn(a, b):
    return lax.dot_general(a, b, (((0,), (0,)), ((), ())), preferred_element_type=f32)


def _sigmoid(x):
    return 1.0 / (1.0 + jnp.exp(-x))


def _ones_column(rows):
    lane = lax.broadcasted_iota(i32, (rows, V_EXT - C_V_DIM), 1)
    return jnp.where(lane == 0, 1.0, 0.0).astype(bf16)


def _group_mean_sq(x, g):
    sq = x * x
    hi = sq.astype(bf16)
    lo = (sq - hi.astype(f32)).astype(bf16)
    return _dot(hi, g) + _dot(lo, g)


def _group_rms(x, g, gain):
    return x * lax.rsqrt(_group_mean_sq(x, g) + EPS) * gain


def _softmax_pv(s, v_blocks, widths):
    p = jnp.exp2(s - jnp.max(s, axis=-1, keepdims=True)).astype(bf16)
    acc, lo = None, 0
    for v, w in zip(v_blocks, widths):
        part = _dot(p[:, lo:lo + w], v)
        acc = part if acc is None else acc + part
        lo += w
    return acc[:, :C_V_DIM] / acc[:, C_V_DIM:C_V_DIM + 1]


def _proj_body(x_ref, g1_ref, w_ref, ga_ref, aqg_ref, akg_ref, w2_ref, gb_ref, qag_ref, wuq_ref,
               gc_ref, cqg_ref, kvag_ref, krg_ref, cos_ref, sin_ref,
               aq_o, ak_o, av_o, akb_o, avb_o, b_o, cq_o, ckv_o, ckr_o, h_sc):
    tm = x_ref.shape[0]
    xn = _rms(x_ref[...], g1_ref[...])
    h_sc[...] = _dot(xn.astype(bf16), w_ref[...])
    cos_t = cos_ref[...]
    sin_t = sin_ref[...]
    q = _group_rms(h_sc[:, P_AQ:P_AQ + A_W], ga_ref[...], aqg_ref[...])
    aq_o[...] = (q * A_QSCALE).astype(bf16)
    k = _group_rms(h_sc[:, P_AK:P_AK + A_W], ga_ref[...], akg_ref[...])
    ak_o[...] = k
    akb_o[...] = k.astype(bf16)
    av_o[...] = h_sc[:, P_AV:P_AV + A_W]
    ones = _ones_column(tm)
    for hd in range(A_HEADS):
        lo = P_AV + hd * A_HEAD_DIM
        avb_o[:, hd * V_EXT:hd * V_EXT + A_HEAD_DIM] = h_sc[:, lo:lo + A_HEAD_DIM].astype(bf16)
        avb_o[:, hd * V_EXT + A_HEAD_DIM:(hd + 1) * V_EXT] = ones
    b_o[:, S_BQ:S_LA] = h_sc[:, P_BQ:P_BG]
    z = jnp.dot(h_sc[:, P_BLR:P_BLR + B_GATE_RANK], w2_ref[...], preferred_element_type=f32,
                precision=lax.Precision.HIGHEST) + gb_ref[...]
    log_sig = jnp.minimum(z, 0.0) - jnp.log1p(jnp.exp(-jnp.abs(z)))
    b_o[:, S_LA:S_BG] = log_sig * (1.0 / B_GATE_TAU)
    b_o[:, S_BG:S_W] = h_sc[:, P_BG:P_BG + B_W]
    cql = _rms(h_sc[:, P_CQ:P_CQ + C_Q_RANK], qag_ref[...])
    cq = _group_rms(_dot(cql.astype(bf16), wuq_ref[...]), gc_ref[...], cqg_ref[...])
    cos4 = jnp.concatenate([cos_t] * C_HEADS, axis=1)
    sin4 = jnp.concatenate([sin_t] * C_HEADS, axis=1)
    cq_o[...] = (_rope_lanes(cq, cos4, sin4) * C_QSCALE).astype(bf16)
    ckv_o[...] = _rms(h_sc[:, P_CKV:P_CKV + C_KV_RANK], kvag_ref[...])
    kr = _group_rms(h_sc[:, P_CKR:P_CKR + LANES], gc_ref[0:LANES, 0:LANES], krg_ref[...])
    ckr_o[...] = _rope_lanes(kr, cos_t, sin_t)[:, C_NOPE_DIM:C_NOPE_DIM + C_ROPE_DIM]


def _proj(h_all, row0, n, lw, cos, sin):
    tm = ROW_TILE
    blk0 = row0 // tm
    npos = cos.shape[0] // tm
    row = lambda w: pl.BlockSpec((tm, w), lambda i: (i, 0))
    const = lambda a: pl.BlockSpec(a.shape, lambda i: (0,) * a.ndim)
    pos = pl.BlockSpec((tm, LANES), lambda i: (i % npos, 0))
    consts = (lw['ln1_g'], lw['w_in'], lw['g_a'], lw['a_q_g'], lw['a_k_g'], lw['b_gate_w2'], lw['b_gate_b'],
              lw['c_qa_g'], lw['c_w_uq'], lw['g_c'], lw['c_q_gain'], lw['c_kva_g'], lw['c_kr_gain'])
    out_w = ((A_W, bf16), (A_W, f32), (A_W, f32), (A_W, bf16), (A_VW, bf16), (S_W, f32),
             (C_QW, bf16), (C_KV_RANK, f32), (C_ROPE_DIM, f32))
    return pl.pallas_call(
        _proj_body,
        grid=(n // tm,),
        in_specs=[pl.BlockSpec((tm, D_MODEL), lambda i: (i + blk0, 0))] + [const(a) for a in consts] + [pos, pos],
        out_specs=[row(w) for w, _ in out_w],
        out_shape=[jax.ShapeDtypeStruct((n, w), dt) for w, dt in out_w],
        scratch_shapes=[pltpu.VMEM((tm, IN_P), f32)],
        compiler_params=_cparams("parallel"),
        name="proj",
    )(h_all, *consts, cos, sin)


def _expand_keys(kv, kr, g_k, k_gain, place, k_dst, v_dst, rows):
    kn = _group_rms(kv[:, :C_HEADS * C_NOPE_DIM], g_k, k_gain).astype(bf16)
    k_dst[rows, :] = _dot(jnp.concatenate([kn, kr], axis=1), place).astype(bf16)
    v = kv[:, C_HEADS * C_NOPE_DIM:]
    lane = lax.broadcasted_iota(i32, v.shape, 1) % LANES
    v_dst[rows, :] = jnp.where(lane == C_V_DIM, 1.0, v).astype(bf16)


def _mla_keys_body(ckv_ref, ckr_ref, w_ref, gk_ref, kg_ref, pl_ref, ck_o, cv_o):
    kv = _dot(ckv_ref[...].astype(bf16), w_ref[...])
    _expand_keys(kv, ckr_ref[...].astype(bf16), gk_ref[...], kg_ref[...], pl_ref[...], ck_o, cv_o, slice(None))


def _mla_keys(ckv, ckr, lw):
    n = ckv.shape[0]
    tm = ROW_TILE
    row = lambda w: pl.BlockSpec((tm, w), lambda i: (i, 0))
    const = lambda a: pl.BlockSpec(a.shape, lambda i: (0,) * a.ndim)
    consts = (lw['c_w_ukv'], lw['g_k'], lw['c_k_gain'], lw['k_place'])
    return pl.pallas_call(
        _mla_keys_body,
        grid=(n // tm,),
        in_specs=[row(C_KV_RANK), row(C_ROPE_DIM)] + [const(a) for a in consts],
        out_specs=[row(C_QW), row(C_VW)],
        out_shape=[jax.ShapeDtypeStruct((n, C_QW), bf16), jax.ShapeDtypeStruct((n, C_VW), bf16)],
        compiler_params=_cparams("parallel"),
        name="mla_keys",
    )(ckv, ckr, *consts)


def _band_body(q_ref, *refs, n_kb, past_from_seq):
    k_refs, v_refs = refs[:n_kb], refs[n_kb:2 * n_kb]
    bias_ref, o_ref = refs[2 * n_kb], refs[2 * n_kb + 1]
    rows = q_ref.shape[0]
    widths = [k.shape[0] for k in k_refs]
    if past_from_seq:
        col = lax.broadcasted_iota(i32, (rows, sum(widths)), 1)
        exists = col + (pl.program_id(1) * rows - A_PAST) >= 0
    for hd in range(A_HEADS):
        qs = slice(hd * A_HEAD_DIM, (hd + 1) * A_HEAD_DIM)
        vs = slice(hd * V_EXT, (hd + 1) * V_EXT)
        q = q_ref[:, qs]
        s = jnp.concatenate([_dot_nt(q, k[:, qs]) for k in k_refs], axis=1) + bias_ref[hd]
        if past_from_seq:
            s = jnp.where(exists, s, NEG)
        o_ref[:, qs] = _softmax_pv(s, [v[:, vs] for v in v_refs], widths).astype(bf16)


def _band(q, k, v, k_past, v_past, bias):
    b_, t, _ = q.shape
    from_seq = k_past is None
    rows = BAND_ROWS if from_seq else t
    n_back = A_PAST // rows if from_seq else 1
    cur = lambda w: pl.BlockSpec((None, rows, w), lambda b, i: (b, i, 0))
    if from_seq:
        back = lambda w, d: pl.BlockSpec((None, rows, w), lambda b, i: (b, jnp.maximum(i - d, 0), 0))
        k_specs = [back(A_W, d) for d in range(n_back, 0, -1)] + [cur(A_W)]
        v_specs = [back(A_VW, d) for d in range(n_back, 0, -1)] + [cur(A_VW)]
        k_args, v_args = [k] * (n_back + 1), [v] * (n_back + 1)
    else:
        past = lambda w: pl.BlockSpec((None, A_PAST, w), lambda b, i: (b, 0, 0))
        k_specs, v_specs = [past(A_W), cur(A_W)], [past(A_VW), cur(A_VW)]
        k_args, v_args = [k_past, k], [v_past, v]
    return pl.pallas_call(
        functools.partial(_band_body, n_kb=len(k_specs), past_from_seq=from_seq),
        grid=(b_, t // rows),
        in_specs=[cur(A_W)] + k_specs + v_specs + [pl.BlockSpec(bias.shape, lambda b, i: (0, 0, 0))],
        out_specs=cur(A_W),
        out_shape=jax.ShapeDtypeStruct((b_, t, A_W), bf16),
        compiler_params=_cparams("parallel", "parallel"),
        name="band",
    )(q, *k_args, *v_args, bias)


def _gla_body(b_ref, s0_ref, gain_ref, o_ref, st_ref, *, chunks):
    @pl.when(pl.program_id(1) == 0)
    def _():
        st_ref[...] = s0_ref[...]

    rows = chunks * CHUNK
    r = lax.broadcasted_iota(i32, (rows, rows), 0)
    c = lax.broadcasted_iota(i32, (rows, rows), 1)
    causal = jnp.logical_and(r >= c, r // CHUNK == c // CHUNK)
    tril = (lax.broadcasted_iota(i32, (CHUNK, CHUNK), 0) >= lax.broadcasted_iota(i32, (CHUNK, CHUNK), 1)).astype(f32)
    bcum, btot = [], []
    for ch in range(chunks):
        la = b_ref[ch * CHUNK:(ch + 1) * CHUNK, S_LA:S_LA + B_KW]
        cum = jnp.dot(tril, la, preferred_element_type=f32, precision=lax.Precision.HIGHEST)
        bcum.append(cum)
        btot.append(jnp.broadcast_to(cum[CHUNK - 1:CHUNK, :], cum.shape))
    bcum = jnp.concatenate(bcum, axis=0)
    btot = jnp.concatenate(btot, axis=0)
    k = b_ref[:, S_BK:S_BK + B_KW]
    qt = b_ref[:, S_BQ:S_BQ + B_KW] * (B_KEY_DIM ** -0.5) * jnp.exp(bcum)
    kt = k * jnp.exp(-bcum)
    kd = k * jnp.exp(btot - bcum)
    decay = jnp.exp(btot)
    for hd in range(B_HEADS):
        ks = slice(hd * B_KEY_DIM, (hd + 1) * B_KEY_DIM)
        vs = slice(hd * B_VAL_DIM, (hd + 1) * B_VAL_DIM)
        v = b_ref[:, S_BV + vs.start:S_BV + vs.stop]
        att = jnp.where(causal, _dot_nt(qt[:, ks], kt[:, ks]), 0.0)
        o_intra = _dot(att, v)
        st = st_ref[hd]
        outs = []
        for ch in range(chunks):
            rs = slice(ch * CHUNK, (ch + 1) * CHUNK)
            outs.append(o_intra[rs] + _dot_nt(qt[rs, ks], st))
            st = st * decay[ch * CHUNK:ch * CHUNK + 1, ks] + _dot_tn(v[rs], kd[rs, ks])
        st_ref[hd] = st
        g = b_ref[:, S_BG + vs.start:S_BG + vs.stop]
        o = _rms(jnp.concatenate(outs, axis=0), gain_ref[...]) * (g * _sigmoid(g))
        o_ref[:, vs] = o.astype(bf16)


def _gla(slab, s0_t, gain):
    b_, t, _ = slab.shape
    chunks = min(GLA_CHUNKS, t // CHUNK)
    rows = chunks * CHUNK
    st_spec = pl.BlockSpec((None, B_HEADS, B_VAL_DIM, B_KEY_DIM), lambda b, i: (b, 0, 0, 0))
    return pl.pallas_call(
        functools.partial(_gla_body, chunks=chunks),
        grid=(b_, t // rows),
        in_specs=[pl.BlockSpec((None, rows, S_W), lambda b, i: (b, i, 0)), st_spec,
                  pl.BlockSpec(gain.shape, lambda b, i: (0, 0))],
        out_specs=[pl.BlockSpec((None, rows, B_W), lambda b, i: (b, i, 0)), st_spec],
        out_shape=[jax.ShapeDtypeStruct((b_, t, B_W), bf16),
                   jax.ShapeDtypeStruct((b_, B_HEADS, B_VAL_DIM, B_KEY_DIM), f32)],
        compiler_params=_cparams("parallel", "arbitrary"),
        name="gla",
    )(slab, s0_t, gain)


def _mla_causal_body(q_ref, k_ref, v_ref, o_ref, m_sc, acc_sc):
    i = pl.program_id(1)
    t = q_ref.shape[0]
    m_sc[...] = jnp.full(m_sc.shape, NEG, f32)
    acc_sc[...] = jnp.zeros(acc_sc.shape, f32)

    def tile(j, diagonal):
        rows = pl.ds(pl.multiple_of(j * t, t), t)
        if diagonal:
            visible = (lax.broadcasted_iota(i32, (t, t), 1) // CHUNK) <= (lax.broadcasted_iota(i32, (t, t), 0) // CHUNK)
        for hd in range(C_HEADS):
            hs = slice(hd * C_PAD_DIM, (hd + 1) * C_PAD_DIM)
            s = _dot_nt(q_ref[:, hs], k_ref[rows, hs])
            if diagonal:
                s = jnp.where(visible, s, NEG)
            m_old = m_sc[hd]
            m_new = jnp.maximum(m_old, jnp.max(s, axis=-1, keepdims=True))
            p = jnp.exp2(s - jnp.concatenate([m_new] * (t // LANES), axis=1))
            acc_sc[hd] = jnp.exp2(m_old - m_new) * acc_sc[hd] + _dot(p.astype(bf16), v_ref[rows, hs])
            m_sc[hd] = m_new

    def earlier(j, carry):
        tile(j, False)
        return carry

    lax.fori_loop(0, i, earlier, 0)
    tile(i, True)
    for hd in range(C_HEADS):
        acc = acc_sc[hd]
        o_ref[:, hd * C_V_DIM:(hd + 1) * C_V_DIM] = (acc[:, :C_V_DIM] / acc[:, C_V_DIM:C_V_DIM + 1]).astype(bf16)


def _mla_causal(q, k, v):
    b_, t, _ = q.shape
    tq = MLA_TILE
    whole = lambda w: pl.BlockSpec((None, t, w), lambda b, i: (b, 0, 0))
    return pl.pallas_call(
        _mla_causal_body,
        grid=(b_, t // tq),
        in_specs=[pl.BlockSpec((None, tq, C_QW), lambda b, i: (b, i, 0)), whole(C_QW), whole(C_VW)],
        out_specs=pl.BlockSpec((None, tq, C_W), lambda b, i: (b, i, 0)),
        out_shape=jax.ShapeDtypeStruct((b_, t, C_W), bf16),
        scratch_shapes=[pltpu.VMEM((C_HEADS, tq, LANES), f32), pltpu.VMEM((C_HEADS, tq, V_EXT), f32)],
        compiler_params=_cparams("parallel", "arbitrary"),
        name="mla_causal",
    )(q, k, v)


def _mla_sample_body(q_ref, ckv_ref, ckr_ref, kn_ref, vn_ref, w_ref, gk_ref, kg_ref, pl_ref, o_ref, k_sc, v_sc):
    past = ckv_ref.shape[0]
    kv = _dot(ckv_ref[...].astype(bf16), w_ref[...])
    _expand_keys(kv, ckr_ref[...].astype(bf16), gk_ref[...], kg_ref[...], pl_ref[...], k_sc, v_sc, slice(0, past))
    k_sc[past:, :] = kn_ref[...]
    v_sc[past:, :] = vn_ref[...]
    total = k_sc.shape[0]
    for hd in range(C_HEADS):
        hs = slice(hd * C_PAD_DIM, (hd + 1) * C_PAD_DIM)
        s = _dot_nt(q_ref[:, hs], k_sc[:, hs])
        o_ref[:, hd * C_V_DIM:(hd + 1) * C_V_DIM] = _softmax_pv(s, [v_sc[:, hs]], [total]).astype(bf16)


def _mla_sample(q, c_kv, c_kr, k_new, v_new, lw):
    b_, s_len, _ = q.shape
    past = c_kv.shape[1]
    per_b = lambda r, w: pl.BlockSpec((None, r, w), lambda b: (b, 0, 0))
    const = lambda a: pl.BlockSpec(a.shape, lambda b: (0,) * a.ndim)
    consts = (lw['c_w_ukv'], lw['g_k'], lw['c_k_gain'], lw['k_place'])
    return pl.pallas_call(
        _mla_sample_body,
        grid=(b_,),
        in_specs=[per_b(s_len, C_QW), per_b(past, C_KV_RANK), per_b(past, C_ROPE_DIM), per_b(s_len, C_QW),
                  per_b(s_len, C_VW)] + [const(a) for a in consts],
        out_specs=per_b(s_len, C_W),
        out_shape=jax.ShapeDtypeStruct((b_, s_len, C_W), bf16),
        scratch_shapes=[pltpu.VMEM((past + s_len, C_QW), bf16), pltpu.VMEM((past + s_len, C_VW), bf16)],
        compiler_params=_cparams("parallel"),
        name="mla_sample",
    )(q, c_kv, c_kr, k_new, v_new, *consts)


def _outproj_body(ap_ref, bp_ref, cp_ref, as_ref, bs_ref, cs_ref, h_ref, w_ref, g2_ref, rwh_ref, rwl_ref, rb_ref,
                  h_o, xn_o, lg_o, *, prompt_tiles):
    def run(a_ref, b_ref, c_ref):
        y = _dot(a_ref[...], w_ref[0:A_W, :])
        y += _dot(b_ref[...], w_ref[A_W:A_W + B_W, :])
        y += _dot(c_ref[...], w_ref[A_W + B_W:, :])
        h = h_ref[...] + y
        h_o[...] = h
        xn = _rms(h, g2_ref[...])
        for c in range(ROW_TILES):
            xn_o[pl.ds(c, h.shape[0], stride=ROW_TILES), :] = xn[:, c * LANES:(c + 1) * LANES]
        xh = xn.astype(bf16)
        xl = (xn - xh.astype(f32)).astype(bf16)
        lg_o[...] = _dot(xh, rwh_ref[...]) + _dot(xl, rwh_ref[...]) + _dot(xh, rwl_ref[...]) + rb_ref[...]

    i = pl.program_id(0)
    pl.when(i < prompt_tiles)(functools.partial(run, ap_ref, bp_ref, cp_ref))
    pl.when(i >= prompt_tiles)(functools.partial(run, as_ref, bs_ref, cs_ref))


def _outproj(mix_p, mix_s, h_all, lw):
    n = h_all.shape[0]
    tm = ROW_TILE
    pt = mix_p[0].shape[0] // tm
    row = lambda w: pl.BlockSpec((tm, w), lambda i: (i, 0))
    first = lambda w: pl.BlockSpec((tm, w), lambda i: (jnp.minimum(i, pt - 1), 0))
    second = lambda w: pl.BlockSpec((tm, w), lambda i: (jnp.maximum(i - pt, 0), 0))
    const = lambda a: pl.BlockSpec(a.shape, lambda i: (0,) * a.ndim)
    consts = (lw['w_out'], lw['ln2_g'], lw['router_w_hi'], lw['router_w_lo'], lw['router_b'])
    return pl.pallas_call(
        functools.partial(_outproj_body, prompt_tiles=pt),
        grid=(n // tm,),
        in_specs=[first(A_W), first(B_W), first(C_W), second(A_W), second(B_W), second(C_W), row(D_MODEL)]
        + [const(a) for a in consts],
        out_specs=[row(D_MODEL), pl.BlockSpec((tm * ROW_TILES, LANES), lambda i: (i, 0)), row(N_EXPERTS)],
        out_shape=[jax.ShapeDtypeStruct((n, D_MODEL), f32), jax.ShapeDtypeStruct((n * ROW_TILES, LANES), f32),
                   jax.ShapeDtypeStruct((n, N_EXPERTS), f32)],
        compiler_params=_cparams("parallel"),
        name="outproj",
    )(*mix_p, *mix_s, h_all, *consts)


def _route_body(lg_ref, e_o, g_o, r_o, cnt_o):
    @pl.when(pl.program_id(0) == 0)
    def _():
        cnt_o[...] = jnp.zeros(cnt_o.shape, f32)

    tm = lg_ref.shape[0]
    lane = lax.broadcasted_iota(i32, (tm, N_EXPERTS), 1)
    slot = lax.broadcasted_iota(i32, (tm, TOP_K), 1)
    work = lg_ref[...]
    onehots, vals = [], []
    e_out = jnp.zeros((tm, TOP_K), i32)
    for k in range(TOP_K):
        m = jnp.max(work, axis=-1, keepdims=True)
        idx = jnp.min(jnp.where(work == m, lane, N_EXPERTS), axis=-1, keepdims=True)
        oh = lane == idx
        work = jnp.where(oh, -jnp.inf, work)
        onehots.append(oh)
        vals.append(m)
        e_out = jnp.where(slot == k, idx, e_out)
    ex = [jnp.exp(v - vals[0]) for v in vals]
    denom = ex[0] + ex[1] + ex[2] + ex[3]
    g_out = jnp.zeros((tm, TOP_K), f32)
    sel = jnp.zeros((tm, N_EXPERTS), f32)
    for k in range(TOP_K):
        g_out = jnp.where(slot == k, ex[k] / denom, g_out)
        sel = sel + onehots[k].astype(f32)
    r = lax.broadcasted_iota(i32, (tm, tm), 0)
    c = lax.broadcasted_iota(i32, (tm, tm), 1)
    before = (c < r).astype(bf16)
    rank = _dot(before, sel.astype(bf16)) + cnt_o[...]
    r_out = jnp.zeros((tm, TOP_K), f32)
    for k in range(TOP_K):
        rk = jnp.sum(jnp.where(onehots[k], rank, 0.0), axis=-1, keepdims=True)
        r_out = jnp.where(slot == k, rk, r_out)
    e_o[...] = e_out
    g_o[...] = g_out
    r_o[...] = r_out.astype(i32)
    cnt_o[...] += jnp.sum(sel, axis=0, keepdims=True)


def _route(logits):
    n = logits.shape[0]
    tm = ROW_TILE
    row = lambda w: pl.BlockSpec((tm, w), lambda i: (i, 0))
    return pl.pallas_call(
        _route_body,
        grid=(n // tm,),
        in_specs=[row(N_EXPERTS)],
        out_specs=[row(TOP_K), row(TOP_K), row(TOP_K), pl.BlockSpec((1, N_EXPERTS), lambda i: (0, 0))],
        out_shape=[jax.ShapeDtypeStruct((n, TOP_K), i32), jax.ShapeDtypeStruct((n, TOP_K), f32),
                   jax.ShapeDtypeStruct((n, TOP_K), i32), jax.ShapeDtypeStruct((1, N_EXPERTS), f32)],
        compiler_params=_cparams("arbitrary"),
        name="route",
    )(logits)


def _row_copy(src_hbm, tok, dst, slot, j, sem):
    return pltpu.make_async_copy(src_hbm.at[pl.ds(pl.multiple_of(tok * ROW_TILES, ROW_TILES), ROW_TILES), :],
                                 dst.at[slot, pl.ds(j * ROW_TILES, ROW_TILES), :], sem.at[slot])


def _experts_body(blk_e_ref, row_tok_ref, used_ref, x_hbm, wg_ref, bg_ref, wu_ref, bu_ref, wd_ref, bd_ref, y_o,
                  xbuf, xb, wbf, sem):
    b = pl.program_id(0)
    bm = MOE_BLOCK
    last = used_ref[0] - 1
    slot = b % 2

    def wait_block(s):
        pltpu.make_async_copy(x_hbm.at[pl.ds(0, bm * ROW_TILES), :], xbuf.at[s], sem.at[s]).wait()

    @pl.when(b == 0)
    def _():
        def issue(j, carry):
            _row_copy(x_hbm, row_tok_ref[j], xbuf, 0, j, sem).start()
            return carry
        lax.fori_loop(0, bm, issue, 0)

    @pl.when(b <= last)
    def _():
        wait_block(slot)
        xb[...] = jnp.concatenate([xbuf[slot, pl.ds(c, bm, stride=ROW_TILES), :] for c in range(ROW_TILES)],
                                  axis=1).astype(bf16)
        nxt = jnp.minimum(b + 1, last) * bm
        for s in range(2):
            @pl.when(slot != s)
            def _(s=s):
                for j in range(bm):
                    _row_copy(x_hbm, row_tok_ref[nxt + j], xbuf, s, j, sem).start(priority=j % 2)

        @pl.when(jnp.logical_or(b == 0, blk_e_ref[b] != blk_e_ref[jnp.maximum(b - 1, 0)]))
        def _():
            wbf[0] = wg_ref[...].astype(bf16)
            wbf[1] = wu_ref[...].astype(bf16)
            wbf[2] = wd_ref[...].astype(bf16)

        x = xb[...]
        g = _dot(x, wbf[0]) + bg_ref[...]
        u = _dot(x, wbf[1]) + bu_ref[...]
        g = jnp.minimum(g, SWIGLU_LIMIT)
        u = jnp.clip(u, -SWIGLU_LIMIT, SWIGLU_LIMIT)
        hdn = (u + 1.0) * (g * _sigmoid(g * SWIGLU_ALPHA))
        y_o[...] = _dot(hdn.astype(bf16), wbf[2]) + bd_ref[...]

    @pl.when(b == last)
    def _():
        wait_block(1 - slot)

    @pl.when(b > last)
    def _():
        y_o[...] = jnp.zeros(y_o.shape, f32)


def _experts(x, blk_e, row_tok, n_used, layer, ew):
    nblk = blk_e.shape[0]
    bm = MOE_BLOCK
    wspec = lambda: pl.BlockSpec((None, None, D_MODEL, D_EXPERT), lambda b, be, rt, nu: (layer, be[b], 0, 0))
    bspec = lambda: pl.BlockSpec((None, None, 1, D_EXPERT), lambda b, be, rt, nu: (layer, be[b], 0, 0))
    grid_spec = pltpu.PrefetchScalarGridSpec(
        num_scalar_prefetch=3,
        grid=(nblk,),
        in_specs=[pl.BlockSpec(memory_space=pl.ANY), wspec(), bspec(), wspec(), bspec(), wspec(), bspec()],
        out_specs=pl.BlockSpec((bm, D_MODEL), lambda b, be, rt, nu: (b, 0)),
        scratch_shapes=[pltpu.VMEM((2, bm * ROW_TILES, LANES), f32), pltpu.VMEM((bm, D_MODEL), bf16),
                        pltpu.VMEM((3, D_MODEL, D_EXPERT), bf16), pltpu.SemaphoreType.DMA((2,))],
    )
    return pl.pallas_call(
        _experts_body,
        grid_spec=grid_spec,
        out_shape=jax.ShapeDtypeStruct((nblk * bm, D_MODEL), f32),
        compiler_params=_cparams("arbitrary"),
        name="experts",
    )(blk_e, row_tok, n_used, x, ew['e_w_gate'], ew['e_b_gate'], ew['e_w_up'], ew['e_b_up'],
      ew['e_w_down'], ew['e_b_down'])


def _combine_body(dest_ref, y_hbm, h_ref, g_ref, o_ref, ybuf, sem):
    i = pl.program_id(0)
    nt = pl.num_programs(0)
    tt = COMBINE_TILE

    def row_copy(tile, slot, j, k):
        return pltpu.make_async_copy(y_hbm.at[pl.ds(dest_ref[(tile * tt + j) * TOP_K + k], 1), :],
                                     ybuf.at[slot, k, pl.ds(j, 1), :], sem.at[slot])

    def wait_tile(slot):
        for k in range(TOP_K):
            pltpu.make_async_copy(y_hbm.at[pl.ds(0, tt), :], ybuf.at[slot, k], sem.at[slot]).wait()

    slot = i % 2

    @pl.when(i == 0)
    def _():
        def issue(j, carry):
            for k in range(TOP_K):
                row_copy(0, 0, j, k).start(priority=k % 2)
            return carry
        lax.fori_loop(0, tt, issue, 0)

    for s in range(2):
        @pl.when(jnp.logical_and(i + 1 < nt, slot != s))
        def _(s=s):
            for j in range(tt):
                for k in range(TOP_K):
                    row_copy(i + 1, s, j, k).start(priority=k % 2)

    wait_tile(slot)
    g = g_ref[...]
    acc = h_ref[...]
    for k in range(TOP_K):
        acc = acc + ybuf[slot, k] * g[:, k:k + 1]
    o_ref[...] = acc


def _combine(y_rows, dest, h, gates):
    n = h.shape[0]
    tt = COMBINE_TILE
    grid_spec = pltpu.PrefetchScalarGridSpec(
        num_scalar_prefetch=1,
        grid=(n // tt,),
        in_specs=[pl.BlockSpec(memory_space=pl.ANY),
                  pl.BlockSpec((tt, D_MODEL), lambda i, d: (i, 0)),
                  pl.BlockSpec((tt, TOP_K), lambda i, d: (i, 0))],
        out_specs=pl.BlockSpec((tt, D_MODEL), lambda i, d: (i, 0)),
        scratch_shapes=[pltpu.VMEM((2, TOP_K, tt, D_MODEL), f32), pltpu.SemaphoreType.DMA((2,))],
    )
    return pl.pallas_call(
        _combine_body,
        grid_spec=grid_spec,
        out_shape=jax.ShapeDtypeStruct((n, D_MODEL), f32),
        compiler_params=_cparams("arbitrary"),
        name="combine",
    )(dest, y_rows, h, gates)


def _row_tok_body(dest_hbm, zeros_hbm, out_ref, buf0, buf1, sem, zsem):
    ch = buf0.shape[0]
    nchunks = dest_hbm.shape[0] // ch
    bufs = (buf0, buf1)
    fill = pltpu.make_async_copy(zeros_hbm, out_ref, zsem)
    fill.start()
    fill.wait()

    def copy(c, slot):
        return pltpu.make_async_copy(dest_hbm.at[pl.ds(pl.multiple_of(c * ch, ch), ch)], bufs[slot], sem.at[slot])

    copy(0, 0).start()

    def per_pair(p, carry):
        for slot in range(2):
            c = 2 * p + slot

            @pl.when(c + 1 < nchunks)
            def _():
                copy(c + 1, 1 - slot).start()

            copy(c, slot).wait()
            base = c * (ch // TOP_K)

            def scatter(t, cc):
                for k in range(TOP_K):
                    out_ref[bufs[slot][TOP_K * t + k]] = base + t
                return cc

            lax.fori_loop(0, ch // TOP_K, scatter, 0, unroll=8)
        return carry

    lax.fori_loop(0, nchunks // 2, per_pair, 0)


def _row_tok(dest, rows):
    ch = min(ROW_TOK_CHUNK, dest.shape[0] // 2)
    assert dest.shape[0] % (2 * ch) == 0
    return pl.pallas_call(
        _row_tok_body,
        in_specs=[pl.BlockSpec(memory_space=pl.ANY), pl.BlockSpec(memory_space=pl.ANY)],
        out_specs=pl.BlockSpec(memory_space=pltpu.SMEM),
        out_shape=jax.ShapeDtypeStruct((rows,), i32),
        scratch_shapes=[pltpu.SMEM((ch,), i32), pltpu.SMEM((ch,), i32), pltpu.SemaphoreType.DMA((2,)),
                        pltpu.SemaphoreType.DMA(())],
        name="row_tok",
    )(dest, jnp.zeros((rows,), i32))


def _moe(h, xn, logits, layer, ew):
    n = h.shape[0]
    bm = MOE_BLOCK
    top_e, gates, rank, counts = _route(logits)
    counts = counts[0].astype(i32)
    padded = ((counts + bm - 1) // bm) * bm
    pend = jnp.cumsum(padded)
    pstart = pend - padded
    expert = jnp.arange(N_EXPERTS, dtype=i32)
    dest = jnp.sum(jnp.where(top_e[:, :, None] == expert, pstart, 0), axis=-1) + rank
    nblk = (n * TOP_K) // bm + N_EXPERTS
    row_tok = _row_tok(dest.reshape(-1), nblk * bm)
    blk_start = jnp.arange(nblk, dtype=i32) * bm
    blk_e = jnp.minimum(jnp.sum((pend[None, :] <= blk_start[:, None]).astype(i32), axis=1), N_EXPERTS - 1)
    y_rows = _experts(xn, blk_e, row_tok, pend[-1:] // bm, layer, ew)
    return _combine(y_rows, dest.reshape(-1), h, gates)


def _rope_tables(pos, tile_rows):
    half = C_ROPE_DIM // 2
    inv = ROPE_THETA ** (-jnp.arange(half, dtype=f32) / half)
    ang = pos.astype(f32)[:, None] * inv[None, :]
    cos, sin = jnp.cos(ang), jnp.sin(ang)
    n = ang.shape[0]
    tail = jnp.zeros((n, LANES - C_NOPE_DIM - C_ROPE_DIM), f32)
    cos_t = jnp.concatenate([jnp.ones((n, C_NOPE_DIM), f32), cos, cos, tail], axis=1)
    sin_t = jnp.concatenate([jnp.zeros((n, C_NOPE_DIM), f32), -sin, sin, tail], axis=1)
    reps = max(1, tile_rows // n)
    return jnp.tile(cos_t, (reps, 1)), jnp.tile(sin_t, (reps, 1))


def _band_bias(rel_bias, rows):
    cols = A_PAST + rows
    period = cols + rows
    x = jnp.arange(period)
    x = jnp.where(x < cols, x, x - period)
    idx = jnp.clip(A_PAST - x, -A_REL_CLIP, A_REL_CLIP) + A_REL_CLIP
    v = rel_bias[:, idx].astype(f32) * LOG2E
    skew = jnp.tile(v, (1, rows))[:, :rows * (period - 1)].reshape(-1, rows, period - 1)
    r = jnp.arange(rows)[:, None]
    c = jnp.arange(cols)[None, :]
    lo = (r // CHUNK) * CHUNK
    in_band = jnp.logical_and(c >= lo, c < lo + A_BAND)
    return jnp.where(in_band[None], skew[:, :, :cols], NEG)


def _group_matrix(width, groups):
    g = np.zeros((width, width), np.float32)
    for start, size in groups:
        g[start:start + size, start:start + size] = 1.0 / size
    return jnp.asarray(g, bf16)


def _layer_weights(l, p):
    w = p['w_in'][l]
    seg = lambda o, n: w[:, o:o + n]
    zc = lambda n: jnp.zeros((D_MODEL, n), f32)
    o_aq, o_ak, o_av, o_bq, o_bk, o_bv = 0, 384, 768, 1152, 1408, 1664
    o_blr, o_bg, o_cq, o_ckv, o_ckr = 2048, 2064, 2448, 2640, 2768
    w_in = jnp.concatenate([
        seg(o_aq, 384), seg(o_ak, 384), seg(o_av, 384), seg(o_bq, 256), seg(o_bk, 256), seg(o_bv, 384),
        seg(o_bg, 384), seg(o_cq, C_Q_RANK), zc(P_CKV - P_CQ - C_Q_RANK), seg(o_ckv, 128),
        zc(C_NOPE_DIM), seg(o_ckr, 32), seg(o_blr, 16), zc(IN_P - P_BLR - B_GATE_RANK)], axis=1).astype(bf16)
    hq = C_NOPE_DIM + C_ROPE_DIM
    pad_heads = lambda a, n: jnp.pad(a.reshape(a.shape[0], C_HEADS, n), ((0, 0), (0, 0), (0, C_PAD_DIM - n))
                                     ).reshape(a.shape[0], C_QW)
    wuq = pad_heads(p['c_w_uq'][l], hq).astype(bf16)
    wukv = p['c_w_ukv'][l].reshape(C_KV_RANK, C_HEADS, C_NOPE_DIM + C_V_DIM)
    wukv = jnp.concatenate([wukv[:, :, :C_NOPE_DIM].reshape(C_KV_RANK, -1),
                            pad_heads(wukv[:, :, C_NOPE_DIM:].reshape(C_KV_RANK, -1), C_V_DIM)], axis=1).astype(bf16)
    r2 = lambda a: a.reshape(1, -1)
    z = lambda n: jnp.zeros((n,), f32)
    tail = C_PAD_DIM - hq
    head_groups = [(hd * C_PAD_DIM + o, n) for hd in range(C_HEADS) for o, n in ((0, C_NOPE_DIM), (C_NOPE_DIM, C_ROPE_DIM))]
    place = np.zeros((C_HEADS * C_NOPE_DIM + C_ROPE_DIM, C_QW), np.float32)
    for hd in range(C_HEADS):
        place[hd * C_NOPE_DIM + np.arange(C_NOPE_DIM), hd * C_PAD_DIM + np.arange(C_NOPE_DIM)] = 1.0
        place[C_HEADS * C_NOPE_DIM + np.arange(C_ROPE_DIM), hd * C_PAD_DIM + C_NOPE_DIM + np.arange(C_ROPE_DIM)] = 1.0
    rw = p['router_w'][l]
    rw_hi = rw.astype(bf16)
    return {
        'ln1_g': r2(p['ln1_g'][l]), 'ln2_g': r2(p['ln2_g'][l]), 'w_in': w_in,
        'g_a': _group_matrix(A_W, [(hd * A_HEAD_DIM, A_HEAD_DIM) for hd in range(A_HEADS)]),
        'a_q_g': r2(jnp.tile(p['a_q_g'][l], A_HEADS)), 'a_k_g': r2(jnp.tile(p['a_k_g'][l], A_HEADS)),
        'bias_p': _band_bias(p['a_rel_bias'][l], BAND_ROWS), 'bias_s': _band_bias(p['a_rel_bias'][l], CHUNK),
        'b_gate_w2': p['b_gate_w2'][l], 'b_gate_b': r2(p['b_gate_b'][l]), 'b_out_g': r2(p['b_out_g'][l]),
        'c_qa_g': r2(p['c_qa_g'][l]), 'c_w_uq': wuq, 'c_kva_g': r2(p['c_kva_g'][l]), 'c_w_ukv': wukv,
        'g_c': _group_matrix(C_QW, head_groups),
        'c_q_gain': r2(jnp.tile(jnp.concatenate([p['c_qn_g'][l], p['c_qr_g'][l], z(tail)]), C_HEADS)),
        'c_k_gain': r2(jnp.tile(p['c_kn_g'][l], C_HEADS)),
        'g_k': _group_matrix(C_HEADS * C_NOPE_DIM, [(hd * C_NOPE_DIM, C_NOPE_DIM) for hd in range(C_HEADS)]),
        'c_kr_gain': r2(jnp.concatenate([z(C_NOPE_DIM), p['c_kr_g'][l], z(tail)])),
        'k_place': jnp.asarray(place, bf16),
        'w_out': p['w_out'][l].astype(bf16),
        'router_w_hi': rw_hi, 'router_w_lo': (rw - rw_hi.astype(f32)).astype(bf16), 'router_b': r2(p['router_b'][l]),
    }


def _token_mix(h_all, row0, b_, t, lw, cos, sin, past):
    aq, ak, av, akb, avb, slab, cq, ckv, ckr = _proj(h_all, row0, b_ * t, lw, cos, sin)
    r3 = lambda a: a.reshape(b_, t, a.shape[-1])
    ck, cv = _mla_keys(ckv, ckr, lw)
    if past is None:
        a_o = _band(r3(aq), r3(akb), r3(avb), None, None, lw['bias_p'])
        s0_t = jnp.zeros((b_, B_HEADS, B_VAL_DIM, B_KEY_DIM), f32)
        c_o = _mla_causal(r3(cq), r3(ck), r3(cv))
    else:
        a_k, a_v, b_s, c_kv, c_kr = past
        lp = a_k.shape[1]
        v_past = jnp.concatenate([a_v.astype(bf16), jnp.ones(a_v.shape[:-1] + (1,), bf16),
                                  jnp.zeros(a_v.shape[:-1] + (V_EXT - A_HEAD_DIM - 1,), bf16)], axis=-1)
        a_o = _band(r3(aq), r3(akb), r3(avb), a_k.reshape(b_, lp, A_W).astype(bf16),
                    v_past.reshape(b_, lp, A_VW), lw['bias_s'])
        s0_t = jnp.swapaxes(b_s, -1, -2)
        c_o = _mla_sample(r3(cq), c_kv, c_kr, r3(ck), r3(cv), lw)
    b_o, s_t = _gla(r3(slab), s0_t, lw['b_out_g'])
    flat = lambda a: a.reshape(b_ * t, a.shape[-1])
    la = min(A_PAST, t)
    state = (r3(ak)[:, t - la:].reshape(b_, la, A_HEADS, A_HEAD_DIM),
             r3(av)[:, t - la:].reshape(b_, la, A_HEADS, A_HEAD_DIM),
             jnp.swapaxes(s_t, -1, -2), r3(ckv), r3(ckr))
    return (flat(a_o), flat(b_o), flat(c_o)), state


def kernel(x_prompt, x_sample, cache_a_k, cache_a_v, state_b_s, cache_c_kv, cache_c_kr, ln1_g, ln2_g, w_in, a_q_g, a_k_g, a_rel_bias, b_gate_w2, b_gate_b, b_out_g, c_qa_g, c_w_uq, c_kva_g, c_w_ukv, c_qn_g, c_qr_g, c_kn_g, c_kr_g, w_out, router_w, router_b, e_w_gate, e_b_gate, e_w_up, e_b_up, e_w_down, e_b_down):
    params = dict(ln1_g=ln1_g, ln2_g=ln2_g, w_in=w_in, a_q_g=a_q_g, a_k_g=a_k_g, a_rel_bias=a_rel_bias,
                  b_gate_w2=b_gate_w2, b_gate_b=b_gate_b, b_out_g=b_out_g, c_qa_g=c_qa_g, c_w_uq=c_w_uq,
                  c_kva_g=c_kva_g, c_w_ukv=c_w_ukv, c_qn_g=c_qn_g, c_qr_g=c_qr_g, c_kn_g=c_kn_g,
                  c_kr_g=c_kr_g, w_out=w_out, router_w=router_w, router_b=router_b)
    ew = dict(e_w_gate=e_w_gate, e_b_gate=e_b_gate[:, :, None, :], e_w_up=e_w_up, e_b_up=e_b_up[:, :, None, :],
              e_w_down=e_w_down, e_b_down=e_b_down[:, :, None, :])
    bp, tp, _ = x_prompt.shape
    bs, ts, _ = x_sample.shape
    depth = w_in.shape[0]
    n_p = bp * tp
    cos_p, sin_p = _rope_tables(jnp.arange(tp), ROW_TILE)
    cos_s, sin_s = _rope_tables(PAST_LEN + jnp.arange(ts), ROW_TILE)
    h = jnp.concatenate([x_prompt.reshape(n_p, D_MODEL), x_sample.reshape(bs * ts, D_MODEL)])
    st_p, st_s = [], []
    for l in range(depth):
        lw = _layer_weights(l, params)
        mix_p, s_p = _token_mix(h, 0, bp, tp, lw, cos_p, sin_p, None)
        mix_s, s_s = _token_mix(h, n_p, bs, ts, lw, cos_s, sin_s,
                                (cache_a_k[l], cache_a_v[l], state_b_s[l], cache_c_kv[l], cache_c_kr[l]))
        h, xn, logits = _outproj(mix_p, mix_s, h, lw)
        h = _moe(h, xn, logits, l, ew)
        st_p.append(s_p)
        st_s.append(s_s)
    stack = lambda sts, i: jnp.stack([s[i] for s in sts])
    return (h[:n_p].reshape(bp, tp, D_MODEL), h[n_p:].reshape(bs, ts, D_MODEL),
            stack(st_p, 0), stack(st_p, 1), stack(st_p, 2), stack(st_p, 3), stack(st_p, 4),
            stack(st_s, 0), stack(st_s, 1), stack(st_s, 2), stack(st_s, 3), stack(st_s, 4))
```

```python
import functools
import math

import jax
import jax.numpy as jnp
import numpy as np
from jax import lax
from jax.experimental import pallas as pl
from jax.experimental.pallas import tpu as pltpu

f32 = jnp.float32
bf16 = jnp.bfloat16
i32 = jnp.int32

D_MODEL = 1024
PAST_LEN = 2048
CHUNK = 64
A_HEADS = 6
A_HEAD_DIM = 64
A_BAND_CHUNKS = 8
A_PAST = A_BAND_CHUNKS * CHUNK
A_BAND = A_PAST + CHUNK
A_REL_CLIP = 128
B_HEADS = 4
B_KEY_DIM = 64
B_VAL_DIM = 96
B_GATE_RANK = 16
B_GATE_TAU = 16.0
C_HEADS = 4
C_Q_RANK = 192
C_KV_RANK = 128
C_NOPE_DIM = 64
C_ROPE_DIM = 32
C_V_DIM = 64
ROPE_THETA = 10000.0
N_EXPERTS = 32
TOP_K = 4
D_EXPERT = 1024
SWIGLU_LIMIT = 7.0
SWIGLU_ALPHA = 1.702
EPS = 1e-6
NEG = -1e30
LOG2E = 1.0 / math.log(2.0)

LANES = 128
ROW_TILES = D_MODEL // LANES
C_PAD_DIM = LANES
V_EXT = LANES
A_QSCALE = A_HEAD_DIM ** -0.5 * LOG2E
C_QSCALE = (C_NOPE_DIM + C_ROPE_DIM) ** -0.5 * LOG2E

A_W = A_HEADS * A_HEAD_DIM
B_W = B_HEADS * B_VAL_DIM
C_W = C_HEADS * C_V_DIM
B_KW = B_HEADS * B_KEY_DIM
C_QW = C_HEADS * C_PAD_DIM
A_VW = A_HEADS * V_EXT
C_VW = C_HEADS * V_EXT

P_AQ, P_AK, P_AV = 0, 384, 768
P_BQ, P_BK, P_BV, P_BG = 1152, 1408, 1664, 2048
P_CQ, P_CKV, P_CKR, P_BLR = 2432, 2688, 2816, 2912
IN_P = 2944
S_BQ, S_BK, S_BV, S_LA, S_BG, S_W = 0, 256, 512, 896, 1152, 1536

ROW_TILE = 512
BAND_ROWS = 256
GLA_CHUNKS = 4
MLA_TILE = 512
MOE_BLOCK = 256
COMBINE_TILE = 128
ROW_TOK_CHUNK = 4096
VMEM_LIMIT = 56 * 1024 * 1024


def _cparams(*sem):
    return pltpu.CompilerParams(dimension_semantics=sem, vmem_limit_bytes=VMEM_LIMIT)


def _rms(x, g):
    return x * lax.rsqrt(jnp.mean(x * x, axis=-1, keepdims=True) + EPS) * g


def _rope_lanes(x, cos_t, sin_t):
    w = x.shape[1]
    lane = lax.broadcasted_iota(i32, x.shape, 1) % LANES
    first_half = jnp.logical_and(lane >= C_NOPE_DIM, lane < C_NOPE_DIM + C_ROPE_DIM // 2)
    swapped = jnp.where(first_half, pltpu.roll(x, w - C_ROPE_DIM // 2, 1), pltpu.roll(x, C_ROPE_DIM // 2, 1))
    return x * cos_t + swapped * sin_t


def _dot(a, b):
    return jnp.dot(a, b, preferred_element_type=f32)


def _dot_nt(a, b):
    return lax.dot_general(a, b, (((1,), (1,)), ((), ())), preferred_element_type=f32)


def _dot_tn(a, b):
    return lax.dot_general(a, b, (((0,), (0,)), ((), ())), preferred_element_type=f32)


def _sigmoid(x):
    return 1.0 / (1.0 + jnp.exp(-x))


def _ones_column(rows):
    lane = lax.broadcasted_iota(i32, (rows, V_EXT - C_V_DIM), 1)
    return jnp.where(lane == 0, 1.0, 0.0).astype(bf16)


def _group_mean_sq(x, g):
    sq = x * x
    hi = sq.astype(bf16)
    lo = (sq - hi.astype(f32)).astype(bf16)
    return _dot(hi, g) + _dot(lo, g)


def _group_rms(x, g, gain):
    return x * lax.rsqrt(_group_mean_sq(x, g) + EPS) * gain


def _softmax_pv(s, v_blocks, widths):
    p = jnp.exp2(s - jnp.max(s, axis=-1, keepdims=True)).astype(bf16)
    acc, lo = None, 0
    for v, w in zip(v_blocks, widths):
        part = _dot(p[:, lo:lo + w], v)
        acc = part if acc is None else acc + part
        lo += w
    return acc[:, :C_V_DIM] / acc[:, C_V_DIM:C_V_DIM + 1]


def _proj_body(x_ref, g1_ref, w_ref, ga_ref, aqg_ref, akg_ref, w2_ref, gb_ref, qag_ref, wuq_ref,
               gc_ref, cqg_ref, kvag_ref, krg_ref, cos_ref, sin_ref,
               aq_o, ak_o, av_o, akb_o, avb_o, b_o, cq_o, ckv_o, ckr_o, h_sc):
    tm = x_ref.shape[0]
    xn = _rms(x_ref[...], g1_ref[...])
    h_sc[...] = _dot(xn.astype(bf16), w_ref[...])
    cos_t = cos_ref[...]
    sin_t = sin_ref[...]
    q = _group_rms(h_sc[:, P_AQ:P_AQ + A_W], ga_ref[...], aqg_ref[...])
    aq_o[...] = (q * A_QSCALE).astype(bf16)
    k = _group_rms(h_sc[:, P_AK:P_AK + A_W], ga_ref[...], akg_ref[...])
    ak_o[...] = k
    akb_o[...] = k.astype(bf16)
    av_o[...] = h_sc[:, P_AV:P_AV + A_W]
    ones = _ones_column(tm)
    for hd in range(A_HEADS):
        lo = P_AV + hd * A_HEAD_DIM
        avb_o[:, hd * V_EXT:hd * V_EXT + A_HEAD_DIM] = h_sc[:, lo:lo + A_HEAD_DIM].astype(bf16)
        avb_o[:, hd * V_EXT + A_HEAD_DIM:(hd + 1) * V_EXT] = ones
    b_o[:, S_BQ:S_LA] = h_sc[:, P_BQ:P_BG]
    z = jnp.dot(h_sc[:, P_BLR:P_BLR + B_GATE_RANK], w2_ref[...], preferred_element_type=f32,
                precision=lax.Precision.HIGHEST) + gb_ref[...]
    log_sig = jnp.minimum(z, 0.0) - jnp.log1p(jnp.exp(-jnp.abs(z)))
    b_o[:, S_LA:S_BG] = log_sig * (1.0 / B_GATE_TAU)
    b_o[:, S_BG:S_W] = h_sc[:, P_BG:P_BG + B_W]
    cql = _rms(h_sc[:, P_CQ:P_CQ + C_Q_RANK], qag_ref[...])
    cq = _group_rms(_dot(cql.astype(bf16), wuq_ref[...]), gc_ref[...], cqg_ref[...])
    cos4 = jnp.concatenate([cos_t] * C_HEADS, axis=1)
    sin4 = jnp.concatenate([sin_t] * C_HEADS, axis=1)
    cq_o[...] = (_rope_lanes(cq, cos4, sin4) * C_QSCALE).astype(bf16)
    ckv_o[...] = _rms(h_sc[:, P_CKV:P_CKV + C_KV_RANK], kvag_ref[...])
    kr = _group_rms(h_sc[:, P_CKR:P_CKR + LANES], gc_ref[0:LANES, 0:LANES], krg_ref[...])
    ckr_o[...] = _rope_lanes(kr, cos_t, sin_t)[:, C_NOPE_DIM:C_NOPE_DIM + C_ROPE_DIM]


def _proj(h_all, row0, n, lw, cos, sin):
    tm = ROW_TILE
    blk0 = row0 // tm
    npos = cos.shape[0] // tm
    row = lambda w: pl.BlockSpec((tm, w), lambda i: (i, 0))
    const = lambda a: pl.BlockSpec(a.shape, lambda i: (0,) * a.ndim)
    pos = pl.BlockSpec((tm, LANES), lambda i: (i % npos, 0))
    consts = (lw['ln1_g'], lw['w_in'], lw['g_a'], lw['a_q_g'], lw['a_k_g'], lw['b_gate_w2'], lw['b_gate_b'],
              lw['c_qa_g'], lw['c_w_uq'], lw['g_c'], lw['c_q_gain'], lw['c_kva_g'], lw['c_kr_gain'])
    out_w = ((A_W, bf16), (A_W, f32), (A_W, f32), (A_W, bf16), (A_VW, bf16), (S_W, f32),
             (C_QW, bf16), (C_KV_RANK, f32), (C_ROPE_DIM, f32))
    return pl.pallas_call(
        _proj_body,
        grid=(n // tm,),
        in_specs=[pl.BlockSpec((tm, D_MODEL), lambda i: (i + blk0, 0))] + [const(a) for a in consts] + [pos, pos],
        out_specs=[row(w) for w, _ in out_w],
        out_shape=[jax.ShapeDtypeStruct((n, w), dt) for w, dt in out_w],
        scratch_shapes=[pltpu.VMEM((tm, IN_P), f32)],
        compiler_params=_cparams("parallel"),
        name="proj",
    )(h_all, *consts, cos, sin)


def _expand_keys(kv, kr, g_k, k_gain, place, k_dst, v_dst, rows):
    kn = _group_rms(kv[:, :C_HEADS * C_NOPE_DIM], g_k, k_gain).astype(bf16)
    k_dst[rows, :] = _dot(jnp.concatenate([kn, kr], axis=1), place).astype(bf16)
    v = kv[:, C_HEADS * C_NOPE_DIM:]
    lane = lax.broadcasted_iota(i32, v.shape, 1) % LANES
    v_dst[rows, :] = jnp.where(lane == C_V_DIM, 1.0, v).astype(bf16)


def _mla_keys_body(ckv_ref, ckr_ref, w_ref, gk_ref, kg_ref, pl_ref, ck_o, cv_o):
    kv = _dot(ckv_ref[...].astype(bf16), w_ref[...])
    _expand_keys(kv, ckr_ref[...].astype(bf16), gk_ref[...], kg_ref[...], pl_ref[...], ck_o, cv_o, slice(None))


def _mla_keys(ckv, ckr, lw):
    n = ckv.shape[0]
    tm = ROW_TILE
    row = lambda w: pl.BlockSpec((tm, w), lambda i: (i, 0))
    const = lambda a: pl.BlockSpec(a.shape, lambda i: (0,) * a.ndim)
    consts = (lw['c_w_ukv'], lw['g_k'], lw['c_k_gain'], lw['k_place'])
    return pl.pallas_call(
        _mla_keys_body,
        grid=(n // tm,),
        in_specs=[row(C_KV_RANK), row(C_ROPE_DIM)] + [const(a) for a in consts],
        out_specs=[row(C_QW), row(C_VW)],
        out_shape=[jax.ShapeDtypeStruct((n, C_QW), bf16), jax.ShapeDtypeStruct((n, C_VW), bf16)],
        compiler_params=_cparams("parallel"),
        name="mla_keys",
    )(ckv, ckr, *consts)


def _band_body(q_ref, *refs, n_kb, past_from_seq):
    k_refs, v_refs = refs[:n_kb], refs[n_kb:2 * n_kb]
    bias_ref, o_ref = refs[2 * n_kb], refs[2 * n_kb + 1]
    rows = q_ref.shape[0]
    widths = [k.shape[0] for k in k_refs]
    if past_from_seq:
        col = lax.broadcasted_iota(i32, (rows, sum(widths)), 1)
        exists = col + (pl.program_id(1) * rows - A_PAST) >= 0
    for hd in range(A_HEADS):
        qs = slice(hd * A_HEAD_DIM, (hd + 1) * A_HEAD_DIM)
        vs = slice(hd * V_EXT, (hd + 1) * V_EXT)
        q = q_ref[:, qs]
        s = jnp.concatenate([_dot_nt(q, k[:, qs]) for k in k_refs], axis=1) + bias_ref[hd]
        if past_from_seq:
            s = jnp.where(exists, s, NEG)
        o_ref[:, qs] = _softmax_pv(s, [v[:, vs] for v in v_refs], widths).astype(bf16)


def _band(q, k, v, k_past, v_past, bias):
    b_, t, _ = q.shape
    from_seq = k_past is None
    rows = BAND_ROWS if from_seq else t
    n_back = A_PAST // rows if from_seq else 1
    cur = lambda w: pl.BlockSpec((None, rows, w), lambda b, i: (b, i, 0))
    if from_seq:
        back = lambda w, d: pl.BlockSpec((None, rows, w), lambda b, i: (b, jnp.maximum(i - d, 0), 0))
        k_specs = [back(A_W, d) for d in range(n_back, 0, -1)] + [cur(A_W)]
        v_specs = [back(A_VW, d) for d in range(n_back, 0, -1)] + [cur(A_VW)]
        k_args, v_args = [k] * (n_back + 1), [v] * (n_back + 1)
    else:
        past = lambda w: pl.BlockSpec((None, A_PAST, w), lambda b, i: (b, 0, 0))
        k_specs, v_specs = [past(A_W), cur(A_W)], [past(A_VW), cur(A_VW)]
        k_args, v_args = [k_past, k], [v_past, v]
    return pl.pallas_call(
        functools.partial(_band_body, n_kb=len(k_specs), past_from_seq=from_seq),
        grid=(b_, t // rows),
        in_specs=[cur(A_W)] + k_specs + v_specs + [pl.BlockSpec(bias.shape, lambda b, i: (0, 0, 0))],
        out_specs=cur(A_W),
        out_shape=jax.ShapeDtypeStruct((b_, t, A_W), bf16),
        compiler_params=_cparams("parallel", "parallel"),
        name="band",
    )(q, *k_args, *v_args, bias)


def _gla_body(b_ref, s0_ref, gain_ref, o_ref, st_ref, *, chunks):
    @pl.when(pl.program_id(1) == 0)
    def _():
        st_ref[...] = s0_ref[...]

    rows = chunks * CHUNK
    r = lax.broadcasted_iota(i32, (rows, rows), 0)
    c = lax.broadcasted_iota(i32, (rows, rows), 1)
    causal = jnp.logical_and(r >= c, r // CHUNK == c // CHUNK)
    tril = (lax.broadcasted_iota(i32, (CHUNK, CHUNK), 0) >= lax.broadcasted_iota(i32, (CHUNK, CHUNK), 1)).astype(f32)
    bcum, btot = [], []
    for ch in range(chunks):
        la = b_ref[ch * CHUNK:(ch + 1) * CHUNK, S_LA:S_LA + B_KW]
        cum = jnp.dot(tril, la, preferred_element_type=f32, precision=lax.Precision.HIGHEST)
        bcum.append(cum)
        btot.append(jnp.broadcast_to(cum[CHUNK - 1:CHUNK, :], cum.shape))
    bcum = jnp.concatenate(bcum, axis=0)
    btot = jnp.concatenate(btot, axis=0)
    k = b_ref[:, S_BK:S_BK + B_KW]
    qt = b_ref[:, S_BQ:S_BQ + B_KW] * (B_KEY_DIM ** -0.5) * jnp.exp(bcum)
    kt = k * jnp.exp(-bcum)
    kd = k * jnp.exp(btot - bcum)
    decay = jnp.exp(btot)
    for hd in range(B_HEADS):
        ks = slice(hd * B_KEY_DIM, (hd + 1) * B_KEY_DIM)
        vs = slice(hd * B_VAL_DIM, (hd + 1) * B_VAL_DIM)
        v = b_ref[:, S_BV + vs.start:S_BV + vs.stop]
        att = jnp.where(causal, _dot_nt(qt[:, ks], kt[:, ks]), 0.0)
        o_intra = _dot(att, v)
        st = st_ref[hd]
        outs = []
        for ch in range(chunks):
            rs = slice(ch * CHUNK, (ch + 1) * CHUNK)
            outs.append(o_intra[rs] + _dot_nt(qt[rs, ks], st))
            st = st * decay[ch * CHUNK:ch * CHUNK + 1, ks] + _dot_tn(v[rs], kd[rs, ks])
        st_ref[hd] = st
        g = b_ref[:, S_BG + vs.start:S_BG + vs.stop]
        o = _rms(jnp.concatenate(outs, axis=0), gain_ref[...]) * (g * _sigmoid(g))
        o_ref[:, vs] = o.astype(bf16)


def _gla(slab, s0_t, gain):
    b_, t, _ = slab.shape
    chunks = min(GLA_CHUNKS, t // CHUNK)
    rows = chunks * CHUNK
    st_spec = pl.BlockSpec((None, B_HEADS, B_VAL_DIM, B_KEY_DIM), lambda b, i: (b, 0, 0, 0))
    return pl.pallas_call(
        functools.partial(_gla_body, chunks=chunks),
        grid=(b_, t // rows),
        in_specs=[pl.BlockSpec((None, rows, S_W), lambda b, i: (b, i, 0)), st_spec,
                  pl.BlockSpec(gain.shape, lambda b, i: (0, 0))],
        out_specs=[pl.BlockSpec((None, rows, B_W), lambda b, i: (b, i, 0)), st_spec],
        out_shape=[jax.ShapeDtypeStruct((b_, t, B_W), bf16),
                   jax.ShapeDtypeStruct((b_, B_HEADS, B_VAL_DIM, B_KEY_DIM), f32)],
        compiler_params=_cparams("parallel", "arbitrary"),
        name="gla",
    )(slab, s0_t, gain)


def _mla_causal_body(q_ref, k_ref, v_ref, o_ref, m_sc, acc_sc):
    i = pl.program_id(1)
    t = q_ref.shape[0]
    m_sc[...] = jnp.full(m_sc.shape, NEG, f32)
    acc_sc[...] = jnp.zeros(acc_sc.shape, f32)

    def tile(j, diagonal):
        rows = pl.ds(pl.multiple_of(j * t, t), t)
        if diagonal:
            visible = (lax.broadcasted_iota(i32, (t, t), 1) // CHUNK) <= (lax.broadcasted_iota(i32, (t, t), 0) // CHUNK)
        for hd in range(C_HEADS):
            hs = slice(hd * C_PAD_DIM, (hd + 1) * C_PAD_DIM)
            s = _dot_nt(q_ref[:, hs], k_ref[rows, hs])
            if diagonal:
                s = jnp.where(visible, s, NEG)
            m_old = m_sc[hd]
            m_new = jnp.maximum(m_old, jnp.max(s, axis=-1, keepdims=True))
            p = jnp.exp2(s - jnp.concatenate([m_new] * (t // LANES), axis=1))
            acc_sc[hd] = jnp.exp2(m_old - m_new) * acc_sc[hd] + _dot(p.astype(bf16), v_ref[rows, hs])
            m_sc[hd] = m_new

    def earlier(j, carry):
        tile(j, False)
        return carry

    lax.fori_loop(0, i, earlier, 0)
    tile(i, True)
    for hd in range(C_HEADS):
        acc = acc_sc[hd]
        o_ref[:, hd * C_V_DIM:(hd + 1) * C_V_DIM] = (acc[:, :C_V_DIM] / acc[:, C_V_DIM:C_V_DIM + 1]).astype(bf16)


def _mla_causal(q, k, v):
    b_, t, _ = q.shape
    tq = MLA_TILE
    whole = lambda w: pl.BlockSpec((None, t, w), lambda b, i: (b, 0, 0))
    return pl.pallas_call(
        _mla_causal_body,
        grid=(b_, t // tq),
        in_specs=[pl.BlockSpec((None, tq, C_QW), lambda b, i: (b, i, 0)), whole(C_QW), whole(C_VW)],
        out_specs=pl.BlockSpec((None, tq, C_W), lambda b, i: (b, i, 0)),
        out_shape=jax.ShapeDtypeStruct((b_, t, C_W), bf16),
        scratch_shapes=[pltpu.VMEM((C_HEADS, tq, LANES), f32), pltpu.VMEM((C_HEADS, tq, V_EXT), f32)],
        compiler_params=_cparams("parallel", "arbitrary"),
        name="mla_causal",
    )(q, k, v)


def _mla_sample_body(q_ref, ckv_ref, ckr_ref, kn_ref, vn_ref, w_ref, gk_ref, kg_ref, pl_ref, o_ref, k_sc, v_sc):
    past = ckv_ref.shape[0]
    kv = _dot(ckv_ref[...].astype(bf16), w_ref[...])
    _expand_keys(kv, ckr_ref[...].astype(bf16), gk_ref[...], kg_ref[...], pl_ref[...], k_sc, v_sc, slice(0, past))
    k_sc[past:, :] = kn_ref[...]
    v_sc[past:, :] = vn_ref[...]
    total = k_sc.shape[0]
    for hd in range(C_HEADS):
        hs = slice(hd * C_PAD_DIM, (hd + 1) * C_PAD_DIM)
        s = _dot_nt(q_ref[:, hs], k_sc[:, hs])
        o_ref[:, hd * C_V_DIM:(hd + 1) * C_V_DIM] = _softmax_pv(s, [v_sc[:, hs]], [total]).astype(bf16)


def _mla_sample(q, c_kv, c_kr, k_new, v_new, lw):
    b_, s_len, _ = q.shape
    past = c_kv.shape[1]
    per_b = lambda r, w: pl.BlockSpec((None, r, w), lambda b: (b, 0, 0))
    const = lambda a: pl.BlockSpec(a.shape, lambda b: (0,) * a.ndim)
    consts = (lw['c_w_ukv'], lw['g_k'], lw['c_k_gain'], lw['k_place'])
    return pl.pallas_call(
        _mla_sample_body,
        grid=(b_,),
        in_specs=[per_b(s_len, C_QW), per_b(past, C_KV_RANK), per_b(past, C_ROPE_DIM), per_b(s_len, C_QW),
                  per_b(s_len, C_VW)] + [const(a) for a in consts],
        out_specs=per_b(s_len, C_W),
        out_shape=jax.ShapeDtypeStruct((b_, s_len, C_W), bf16),
        scratch_shapes=[pltpu.VMEM((past + s_len, C_QW), bf16), pltpu.VMEM((past + s_len, C_VW), bf16)],
        compiler_params=_cparams("parallel"),
        name="mla_sample",
    )(q, c_kv, c_kr, k_new, v_new, *consts)


def _outproj_body(ap_ref, bp_ref, cp_ref, as_ref, bs_ref, cs_ref, h_ref, w_ref, g2_ref, rwh_ref, rwl_ref, rb_ref,
                  h_o, xn_o, lg_o, *, prompt_tiles):
    def run(a_ref, b_ref, c_ref):
        y = _dot(a_ref[...], w_ref[0:A_W, :])
        y += _dot(b_ref[...], w_ref[A_W:A_W + B_W, :])
        y += _dot(c_ref[...], w_ref[A_W + B_W:, :])
        h = h_ref[...] + y
        h_o[...] = h
        xn = _rms(h, g2_ref[...])
        for c in range(ROW_TILES):
            xn_o[pl.ds(c, h.shape[0], stride=ROW_TILES), :] = xn[:, c * LANES:(c + 1) * LANES]
        xh = xn.astype(bf16)
        xl = (xn - xh.astype(f32)).astype(bf16)
        lg_o[...] = _dot(xh, rwh_ref[...]) + _dot(xl, rwh_ref[...]) + _dot(xh, rwl_ref[...]) + rb_ref[...]

    i = pl.program_id(0)
    pl.when(i < prompt_tiles)(functools.partial(run, ap_ref, bp_ref, cp_ref))
    pl.when(i >= prompt_tiles)(functools.partial(run, as_ref, bs_ref, cs_ref))


def _outproj(mix_p, mix_s, h_all, lw):
    n = h_all.shape[0]
    tm = ROW_TILE
    pt = mix_p[0].shape[0] // tm
    row = lambda w: pl.BlockSpec((tm, w), lambda i: (i, 0))
    first = lambda w: pl.BlockSpec((tm, w), lambda i: (jnp.minimum(i, pt - 1), 0))
    second = lambda w: pl.BlockSpec((tm, w), lambda i: (jnp.maximum(i - pt, 0), 0))
    const = lambda a: pl.BlockSpec(a.shape, lambda i: (0,) * a.ndim)
    consts = (lw['w_out'], lw['ln2_g'], lw['router_w_hi'], lw['router_w_lo'], lw['router_b'])
    return pl.pallas_call(
        functools.partial(_outproj_body, prompt_tiles=pt),
        grid=(n // tm,),
        in_specs=[first(A_W), first(B_W), first(C_W), second(A_W), second(B_W), second(C_W), row(D_MODEL)]
        + [const(a) for a in consts],
        out_specs=[row(D_MODEL), pl.BlockSpec((tm * ROW_TILES, LANES), lambda i: (i, 0)), row(N_EXPERTS)],
        out_shape=[jax.ShapeDtypeStruct((n, D_MODEL), f32), jax.ShapeDtypeStruct((n * ROW_TILES, LANES), f32),
                   jax.ShapeDtypeStruct((n, N_EXPERTS), f32)],
        compiler_params=_cparams("parallel"),
        name="outproj",
    )(*mix_p, *mix_s, h_all, *consts)


def _route_body(lg_ref, e_o, g_o, r_o, cnt_o):
    @pl.when(pl.program_id(0) == 0)
    def _():
        cnt_o[...] = jnp.zeros(cnt_o.shape, f32)

    tm = lg_ref.shape[0]
    lane = lax.broadcasted_iota(i32, (tm, N_EXPERTS), 1)
    slot = lax.broadcasted_iota(i32, (tm, TOP_K), 1)
    work = lg_ref[...]
    onehots, vals = [], []
    e_out = jnp.zeros((tm, TOP_K), i32)
    for k in range(TOP_K):
        m = jnp.max(work, axis=-1, keepdims=True)
        idx = jnp.min(jnp.where(work == m, lane, N_EXPERTS), axis=-1, keepdims=True)
        oh = lane == idx
        work = jnp.where(oh, -jnp.inf, work)
        onehots.append(oh)
        vals.append(m)
        e_out = jnp.where(slot == k, idx, e_out)
    ex = [jnp.exp(v - vals[0]) for v in vals]
    denom = ex[0] + ex[1] + ex[2] + ex[3]
    g_out = jnp.zeros((tm, TOP_K), f32)
    sel = jnp.zeros((tm, N_EXPERTS), f32)
    for k in range(TOP_K):
        g_out = jnp.where(slot == k, ex[k] / denom, g_out)
        sel = sel + onehots[k].astype(f32)
    r = lax.broadcasted_iota(i32, (tm, tm), 0)
    c = lax.broadcasted_iota(i32, (tm, tm), 1)
    before = (c < r).astype(bf16)
    rank = _dot(before, sel.astype(bf16)) + cnt_o[...]
    r_out = jnp.zeros((tm, TOP_K), f32)
    for k in range(TOP_K):
        rk = jnp.sum(jnp.where(onehots[k], rank, 0.0), axis=-1, keepdims=True)
        r_out = jnp.where(slot == k, rk, r_out)
    e_o[...] = e_out
    g_o[...] = g_out
    r_o[...] = r_out.astype(i32)
    cnt_o[...] += jnp.sum(sel, axis=0, keepdims=True)


def _route(logits):
    n = logits.shape[0]
    tm = ROW_TILE
    row = lambda w: pl.BlockSpec((tm, w), lambda i: (i, 0))
    return pl.pallas_call(
        _route_body,
        grid=(n // tm,),
        in_specs=[row(N_EXPERTS)],
        out_specs=[row(TOP_K), row(TOP_K), row(TOP_K), pl.BlockSpec((1, N_EXPERTS), lambda i: (0, 0))],
        out_shape=[jax.ShapeDtypeStruct((n, TOP_K), i32), jax.ShapeDtypeStruct((n, TOP_K), f32),
                   jax.ShapeDtypeStruct((n, TOP_K), i32), jax.ShapeDtypeStruct((1, N_EXPERTS), f32)],
        compiler_params=_cparams("arbitrary"),
        name="route",
    )(logits)


def _row_copy(src_hbm, tok, dst, slot, j, sem):
    return pltpu.make_async_copy(src_hbm.at[pl.ds(pl.multiple_of(tok * ROW_TILES, ROW_TILES), ROW_TILES), :],
                                 dst.at[slot, pl.ds(j * ROW_TILES, ROW_TILES), :], sem.at[slot])


def _experts_body(blk_e_ref, row_tok_ref, used_ref, x_hbm, wg_ref, bg_ref, wu_ref, bu_ref, wd_ref, bd_ref, y_o,
                  xbuf, xb, hb, wbf, sem):
    b = pl.program_id(0)
    bm = MOE_BLOCK
    last = used_ref[0] - 1
    slot = b % 2

    def wait_block(s):
        pltpu.make_async_copy(x_hbm.at[pl.ds(0, bm * ROW_TILES), :], xbuf.at[s], sem.at[s]).wait()

    @pl.when(b == 0)
    def _():
        def issue(j, carry):
            _row_copy(x_hbm, row_tok_ref[j], xbuf, 0, j, sem).start()
            return carry
        lax.fori_loop(0, bm, issue, 0)

    @pl.when(b <= last)
    def _():
        wait_block(slot)
        xb[...] = jnp.concatenate([xbuf[slot, pl.ds(c, bm, stride=ROW_TILES), :] for c in range(ROW_TILES)],
                                  axis=1).astype(bf16)
        nxt = jnp.minimum(b + 1, last) * bm

        def issue(lo, hi):
            for s in range(2):
                @pl.when(slot != s)
                def _(s=s):
                    for j in range(lo, hi):
                        _row_copy(x_hbm, row_tok_ref[nxt + j], xbuf, s, j, sem).start(priority=j % 2)

        @pl.when(jnp.logical_or(b == 0, blk_e_ref[b] != blk_e_ref[jnp.maximum(b - 1, 0)]))
        def _():
            wbf[0] = wg_ref[...].astype(bf16)
            wbf[1] = wu_ref[...].astype(bf16)
            wbf[2] = wd_ref[...].astype(bf16)

        issue(0, bm // 2)
        x = xb[...]
        g = _dot(x, wbf[0]) + bg_ref[...]
        u = _dot(x, wbf[1]) + bu_ref[...]
        g = jnp.minimum(g, SWIGLU_LIMIT)
        u = jnp.clip(u, -SWIGLU_LIMIT, SWIGLU_LIMIT)
        hb[...] = ((u + 1.0) * (g * _sigmoid(g * SWIGLU_ALPHA))).astype(bf16)
        issue(bm // 2, bm)
        y_o[...] = _dot(hb[...], wbf[2]) + bd_ref[...]

    @pl.when(b == last)
    def _():
        wait_block(1 - slot)

    @pl.when(b > last)
    def _():
        y_o[...] = jnp.zeros(y_o.shape, f32)


def _experts(x, blk_e, row_tok, n_used, layer, ew):
    nblk = blk_e.shape[0]
    bm = MOE_BLOCK
    wspec = lambda: pl.BlockSpec((None, None, D_MODEL, D_EXPERT), lambda b, be, rt, nu: (layer, be[b], 0, 0))
    bspec = lambda: pl.BlockSpec((None, None, 1, D_EXPERT), lambda b, be, rt, nu: (layer, be[b], 0, 0))
    grid_spec = pltpu.PrefetchScalarGridSpec(
        num_scalar_prefetch=3,
        grid=(nblk,),
        in_specs=[pl.BlockSpec(memory_space=pl.ANY), wspec(), bspec(), wspec(), bspec(), wspec(), bspec()],
        out_specs=pl.BlockSpec((bm, D_MODEL), lambda b, be, rt, nu: (b, 0)),
        scratch_shapes=[pltpu.VMEM((2, bm * ROW_TILES, LANES), f32), pltpu.VMEM((bm, D_MODEL), bf16), pltpu.VMEM((bm, D_EXPERT), bf16),
                        pltpu.VMEM((3, D_MODEL, D_EXPERT), bf16), pltpu.SemaphoreType.DMA((2,))],
    )
    return pl.pallas_call(
        _experts_body,
        grid_spec=grid_spec,
        out_shape=jax.ShapeDtypeStruct((nblk * bm, D_MODEL), f32),
        compiler_params=_cparams("arbitrary"),
        name="experts",
    )(blk_e, row_tok, n_used, x, ew['e_w_gate'], ew['e_b_gate'], ew['e_w_up'], ew['e_b_up'],
      ew['e_w_down'], ew['e_b_down'])


def _combine_body(dest_ref, y_hbm, h_ref, g_ref, o_ref, ybuf, sem):
    i = pl.program_id(0)
    nt = pl.num_programs(0)
    tt = COMBINE_TILE

    def row_copy(tile, slot, j, k):
        return pltpu.make_async_copy(y_hbm.at[pl.ds(dest_ref[(tile * tt + j) * TOP_K + k], 1), :],
                                     ybuf.at[slot, k, pl.ds(j, 1), :], sem.at[slot])

    def wait_tile(slot):
        for k in range(TOP_K):
            pltpu.make_async_copy(y_hbm.at[pl.ds(0, tt), :], ybuf.at[slot, k], sem.at[slot]).wait()

    slot = i % 2

    @pl.when(i == 0)
    def _():
        def issue(j, carry):
            for k in range(TOP_K):
                row_copy(0, 0, j, k).start(priority=k % 2)
            return carry
        lax.fori_loop(0, tt, issue, 0)

    for s in range(2):
        @pl.when(jnp.logical_and(i + 1 < nt, slot != s))
        def _(s=s):
            for j in range(tt):
                for k in range(TOP_K):
                    row_copy(i + 1, s, j, k).start(priority=k % 2)

    wait_tile(slot)
    g = g_ref[...]
    acc = h_ref[...]
    for k in range(TOP_K):
        acc = acc + ybuf[slot, k] * g[:, k:k + 1]
    o_ref[...] = acc


def _combine(y_rows, dest, h, gates):
    n = h.shape[0]
    tt = COMBINE_TILE
    grid_spec = pltpu.PrefetchScalarGridSpec(
        num_scalar_prefetch=1,
        grid=(n // tt,),
        in_specs=[pl.BlockSpec(memory_space=pl.ANY),
                  pl.BlockSpec((tt, D_MODEL), lambda i, d: (i, 0)),
                  pl.BlockSpec((tt, TOP_K), lambda i, d: (i, 0))],
        out_specs=pl.BlockSpec((tt, D_MODEL), lambda i, d: (i, 0)),
        scratch_shapes=[pltpu.VMEM((2, TOP_K, tt, D_MODEL), f32), pltpu.SemaphoreType.DMA((2,))],
    )
    return pl.pallas_call(
        _combine_body,
        grid_spec=grid_spec,
        out_shape=jax.ShapeDtypeStruct((n, D_MODEL), f32),
        compiler_params=_cparams("arbitrary"),
        name="combine",
    )(dest, y_rows, h, gates)


def _row_tok_body(dest_hbm, zeros_hbm, out_ref, buf0, buf1, sem, zsem):
    ch = buf0.shape[0]
    nchunks = dest_hbm.shape[0] // ch
    bufs = (buf0, buf1)
    fill = pltpu.make_async_copy(zeros_hbm, out_ref, zsem)
    fill.start()
    fill.wait()

    def copy(c, slot):
        return pltpu.make_async_copy(dest_hbm.at[pl.ds(pl.multiple_of(c * ch, ch), ch)], bufs[slot], sem.at[slot])

    copy(0, 0).start()

    def per_pair(p, carry):
        for slot in range(2):
            c = 2 * p + slot

            @pl.when(c + 1 < nchunks)
            def _():
                copy(c + 1, 1 - slot).start()

            copy(c, slot).wait()
            base = c * (ch // TOP_K)

            def scatter(t, cc):
                for k in range(TOP_K):
                    out_ref[bufs[slot][TOP_K * t + k]] = base + t
                return cc

            lax.fori_loop(0, ch // TOP_K, scatter, 0, unroll=8)
        return carry

    lax.fori_loop(0, nchunks // 2, per_pair, 0)


def _row_tok(dest, rows):
    ch = min(ROW_TOK_CHUNK, dest.shape[0] // 2)
    assert dest.shape[0] % (2 * ch) == 0
    return pl.pallas_call(
        _row_tok_body,
        in_specs=[pl.BlockSpec(memory_space=pl.ANY), pl.BlockSpec(memory_space=pl.ANY)],
        out_specs=pl.BlockSpec(memory_space=pltpu.SMEM),
        out_shape=jax.ShapeDtypeStruct((rows,), i32),
        scratch_shapes=[pltpu.SMEM((ch,), i32), pltpu.SMEM((ch,), i32), pltpu.SemaphoreType.DMA((2,)),
                        pltpu.SemaphoreType.DMA(())],
        name="row_tok",
    )(dest, jnp.zeros((rows,), i32))


def _moe(h, xn, logits, layer, ew):
    n = h.shape[0]
    bm = MOE_BLOCK
    top_e, gates, rank, counts = _route(logits)
    counts = counts[0].astype(i32)
    padded = ((counts + bm - 1) // bm) * bm
    pend = jnp.cumsum(padded)
    pstart = pend - padded
    expert = jnp.arange(N_EXPERTS, dtype=i32)
    dest = jnp.sum(jnp.where(top_e[:, :, None] == expert, pstart, 0), axis=-1) + rank
    nblk = (n * TOP_K) // bm + N_EXPERTS
    row_tok = _row_tok(dest.reshape(-1), nblk * bm)
    blk_start = jnp.arange(nblk, dtype=i32) * bm
    blk_e = jnp.minimum(jnp.sum((pend[None, :] <= blk_start[:, None]).astype(i32), axis=1), N_EXPERTS - 1)
    y_rows = _experts(xn, blk_e, row_tok, pend[-1:] // bm, layer, ew)
    return _combine(y_rows, dest.reshape(-1), h, gates)


def _rope_tables(pos, tile_rows):
    half = C_ROPE_DIM // 2
    inv = ROPE_THETA ** (-jnp.arange(half, dtype=f32) / half)
    ang = pos.astype(f32)[:, None] * inv[None, :]
    cos, sin = jnp.cos(ang), jnp.sin(ang)
    n = ang.shape[0]
    tail = jnp.zeros((n, LANES - C_NOPE_DIM - C_ROPE_DIM), f32)
    cos_t = jnp.concatenate([jnp.ones((n, C_NOPE_DIM), f32), cos, cos, tail], axis=1)
    sin_t = jnp.concatenate([jnp.zeros((n, C_NOPE_DIM), f32), -sin, sin, tail], axis=1)
    reps = max(1, tile_rows // n)
    return jnp.tile(cos_t, (reps, 1)), jnp.tile(sin_t, (reps, 1))


def _band_bias(rel_bias, rows):
    cols = A_PAST + rows
    period = cols + rows
    x = jnp.arange(period)
    x = jnp.where(x < cols, x, x - period)
    idx = jnp.clip(A_PAST - x, -A_REL_CLIP, A_REL_CLIP) + A_REL_CLIP
    v = rel_bias[:, idx].astype(f32) * LOG2E
    skew = jnp.tile(v, (1, rows))[:, :rows * (period - 1)].reshape(-1, rows, period - 1)
    r = jnp.arange(rows)[:, None]
    c = jnp.arange(cols)[None, :]
    lo = (r // CHUNK) * CHUNK
    in_band = jnp.logical_and(c >= lo, c < lo + A_BAND)
    return jnp.where(in_band[None], skew[:, :, :cols], NEG)


def _group_matrix(width, groups):
    g = np.zeros((width, width), np.float32)
    for start, size in groups:
        g[start:start + size, start:start + size] = 1.0 / size
    return jnp.asarray(g, bf16)


def _layer_weights(l, p):
    w = p['w_in'][l]
    seg = lambda o, n: w[:, o:o + n]
    zc = lambda n: jnp.zeros((D_MODEL, n), f32)
    o_aq, o_ak, o_av, o_bq, o_bk, o_bv = 0, 384, 768, 1152, 1408, 1664
    o_blr, o_bg, o_cq, o_ckv, o_ckr = 2048, 2064, 2448, 2640, 2768
    w_in = jnp.concatenate([
        seg(o_aq, 384), seg(o_ak, 384), seg(o_av, 384), seg(o_bq, 256), seg(o_bk, 256), seg(o_bv, 384),
        seg(o_bg, 384), seg(o_cq, C_Q_RANK), zc(P_CKV - P_CQ - C_Q_RANK), seg(o_ckv, 128),
        zc(C_NOPE_DIM), seg(o_ckr, 32), seg(o_blr, 16), zc(IN_P - P_BLR - B_GATE_RANK)], axis=1).astype(bf16)
    hq = C_NOPE_DIM + C_ROPE_DIM
    pad_heads = lambda a, n: jnp.pad(a.reshape(a.shape[0], C_HEADS, n), ((0, 0), (0, 0), (0, C_PAD_DIM - n))
                                     ).reshape(a.shape[0], C_QW)
    wuq = pad_heads(p['c_w_uq'][l], hq).astype(bf16)
    wukv = p['c_w_ukv'][l].reshape(C_KV_RANK, C_HEADS, C_NOPE_DIM + C_V_DIM)
    wukv = jnp.concatenate([wukv[:, :, :C_NOPE_DIM].reshape(C_KV_RANK, -1),
                            pad_heads(wukv[:, :, C_NOPE_DIM:].reshape(C_KV_RANK, -1), C_V_DIM)], axis=1).astype(bf16)
    r2 = lambda a: a.reshape(1, -1)
    z = lambda n: jnp.zeros((n,), f32)
    tail = C_PAD_DIM - hq
    head_groups = [(hd * C_PAD_DIM + o, n) for hd in range(C_HEADS) for o, n in ((0, C_NOPE_DIM), (C_NOPE_DIM, C_ROPE_DIM))]
    place = np.zeros((C_HEADS * C_NOPE_DIM + C_ROPE_DIM, C_QW), np.float32)
    for hd in range(C_HEADS):
        place[hd * C_NOPE_DIM + np.arange(C_NOPE_DIM), hd * C_PAD_DIM + np.arange(C_NOPE_DIM)] = 1.0
        place[C_HEADS * C_NOPE_DIM + np.arange(C_ROPE_DIM), hd * C_PAD_DIM + C_NOPE_DIM + np.arange(C_ROPE_DIM)] = 1.0
    rw = p['router_w'][l]
    rw_hi = rw.astype(bf16)
    return {
        'ln1_g': r2(p['ln1_g'][l]), 'ln2_g': r2(p['ln2_g'][l]), 'w_in': w_in,
        'g_a': _group_matrix(A_W, [(hd * A_HEAD_DIM, A_HEAD_DIM) for hd in range(A_HEADS)]),
        'a_q_g': r2(jnp.tile(p['a_q_g'][l], A_HEADS)), 'a_k_g': r2(jnp.tile(p['a_k_g'][l], A_HEADS)),
        'bias_p': _band_bias(p['a_rel_bias'][l], BAND_ROWS), 'bias_s': _band_bias(p['a_rel_bias'][l], CHUNK),
        'b_gate_w2': p['b_gate_w2'][l], 'b_gate_b': r2(p['b_gate_b'][l]), 'b_out_g': r2(p['b_out_g'][l]),
        'c_qa_g': r2(p['c_qa_g'][l]), 'c_w_uq': wuq, 'c_kva_g': r2(p['c_kva_g'][l]), 'c_w_ukv': wukv,
        'g_c': _group_matrix(C_QW, head_groups),
        'c_q_gain': r2(jnp.tile(jnp.concatenate([p['c_qn_g'][l], p['c_qr_g'][l], z(tail)]), C_HEADS)),
        'c_k_gain': r2(jnp.tile(p['c_kn_g'][l], C_HEADS)),
        'g_k': _group_matrix(C_HEADS * C_NOPE_DIM, [(hd * C_NOPE_DIM, C_NOPE_DIM) for hd in range(C_HEADS)]),
        'c_kr_gain': r2(jnp.concatenate([z(C_NOPE_DIM), p['c_kr_g'][l], z(tail)])),
        'k_place': jnp.asarray(place, bf16),
        'w_out': p['w_out'][l].astype(bf16),
        'router_w_hi': rw_hi, 'router_w_lo': (rw - rw_hi.astype(f32)).astype(bf16), 'router_b': r2(p['router_b'][l]),
    }


def _token_mix(h_all, row0, b_, t, lw, cos, sin, past):
    aq, ak, av, akb, avb, slab, cq, ckv, ckr = _proj(h_all, row0, b_ * t, lw, cos, sin)
    r3 = lambda a: a.reshape(b_, t, a.shape[-1])
    ck, cv = _mla_keys(ckv, ckr, lw)
    if past is None:
        a_o = _band(r3(aq), r3(akb), r3(avb), None, None, lw['bias_p'])
        s0_t = jnp.zeros((b_, B_HEADS, B_VAL_DIM, B_KEY_DIM), f32)
        c_o = _mla_causal(r3(cq), r3(ck), r3(cv))
    else:
        a_k, a_v, b_s, c_kv, c_kr = past
        lp = a_k.shape[1]
        v_past = jnp.concatenate([a_v.astype(bf16), jnp.ones(a_v.shape[:-1] + (1,), bf16),
                                  jnp.zeros(a_v.shape[:-1] + (V_EXT - A_HEAD_DIM - 1,), bf16)], axis=-1)
        a_o = _band(r3(aq), r3(akb), r3(avb), a_k.reshape(b_, lp, A_W).astype(bf16),
                    v_past.reshape(b_, lp, A_VW), lw['bias_s'])
        s0_t = jnp.swapaxes(b_s, -1, -2)
        c_o = _mla_sample(r3(cq), c_kv, c_kr, r3(ck), r3(cv), lw)
    b_o, s_t = _gla(r3(slab), s0_t, lw['b_out_g'])
    flat = lambda a: a.reshape(b_ * t, a.shape[-1])
    la = min(A_PAST, t)
    state = (r3(ak)[:, t - la:].reshape(b_, la, A_HEADS, A_HEAD_DIM),
             r3(av)[:, t - la:].reshape(b_, la, A_HEADS, A_HEAD_DIM),
             jnp.swapaxes(s_t, -1, -2), r3(ckv), r3(ckr))
    return (flat(a_o), flat(b_o), flat(c_o)), state


def kernel(x_prompt, x_sample, cache_a_k, cache_a_v, state_b_s, cache_c_kv, cache_c_kr, ln1_g, ln2_g, w_in, a_q_g, a_k_g, a_rel_bias, b_gate_w2, b_gate_b, b_out_g, c_qa_g, c_w_uq, c_kva_g, c_w_ukv, c_qn_g, c_qr_g, c_kn_g, c_kr_g, w_out, router_w, router_b, e_w_gate, e_b_gate, e_w_up, e_b_up, e_w_down, e_b_down):
    params = dict(ln1_g=ln1_g, ln2_g=ln2_g, w_in=w_in, a_q_g=a_q_g, a_k_g=a_k_g, a_rel_bias=a_rel_bias,
                  b_gate_w2=b_gate_w2, b_gate_b=b_gate_b, b_out_g=b_out_g, c_qa_g=c_qa_g, c_w_uq=c_w_uq,
                  c_kva_g=c_kva_g, c_w_ukv=c_w_ukv, c_qn_g=c_qn_g, c_qr_g=c_qr_g, c_kn_g=c_kn_g,
                  c_kr_g=c_kr_g, w_out=w_out, router_w=router_w, router_b=router_b)
    ew = dict(e_w_gate=e_w_gate, e_b_gate=e_b_gate[:, :, None, :], e_w_up=e_w_up, e_b_up=e_b_up[:, :, None, :],
              e_w_down=e_w_down, e_b_down=e_b_down[:, :, None, :])
    bp, tp, _ = x_prompt.shape
    bs, ts, _ = x_sample.shape
    depth = w_in.shape[0]
    n_p = bp * tp
    cos_p, sin_p = _rope_tables(jnp.arange(tp), ROW_TILE)
    cos_s, sin_s = _rope_tables(PAST_LEN + jnp.arange(ts), ROW_TILE)
    h = jnp.concatenate([x_prompt.reshape(n_p, D_MODEL), x_sample.reshape(bs * ts, D_MODEL)])
    st_p, st_s = [], []
    for l in range(depth):
        lw = _layer_weights(l, params)
        mix_p, s_p = _token_mix(h, 0, bp, tp, lw, cos_p, sin_p, None)
        mix_s, s_s = _token_mix(h, n_p, bs, ts, lw, cos_s, sin_s,
                                (cache_a_k[l], cache_a_v[l], state_b_s[l], cache_c_kv[l], cache_c_kr[l]))
        h, xn, logits = _outproj(mix_p, mix_s, h, lw)
        h = _moe(h, xn, logits, l, ew)
        st_p.append(s_p)
        st_s.append(s_s)
    stack = lambda sts, i: jnp.stack([s[i] for s in sts])
    return (h[:n_p].reshape(bp, tp, D_MODEL), h[n_p:].reshape(bs, ts, D_MODEL),
            stack(st_p, 0), stack(st_p, 1), stack(st_p, 2), stack(st_p, 3), stack(st_p, 4),
            stack(st_s, 0), stack(st_s, 1), stack(st_s, 2), stack(st_s, 3), stack(st_s, 4))
```

```python
import functools
import math

import jax
import jax.numpy as jnp
import numpy as np
from jax import lax
from jax.experimental import pallas as pl
from jax.experimental.pallas import tpu as pltpu

f32 = jnp.float32
bf16 = jnp.bfloat16
i32 = jnp.int32

D_MODEL = 1024
PAST_LEN = 2048
CHUNK = 64
A_HEADS = 6
A_HEAD_DIM = 64
A_BAND_CHUNKS = 8
A_PAST = A_BAND_CHUNKS * CHUNK
A_BAND = A_PAST + CHUNK
A_REL_CLIP = 128
B_HEADS = 4
B_KEY_DIM = 64
B_VAL_DIM = 96
B_GATE_RANK = 16
B_GATE_TAU = 16.0
C_HEADS = 4
C_Q_RANK = 192
C_KV_RANK = 128
C_NOPE_DIM = 64
C_ROPE_DIM = 32
C_V_DIM = 64
ROPE_THETA = 10000.0
N_EXPERTS = 32
TOP_K = 4
D_EXPERT = 1024
SWIGLU_LIMIT = 7.0
SWIGLU_ALPHA = 1.702
EPS = 1e-6
NEG = -1e30
LOG2E = 1.0 / math.log(2.0)

LANES = 128
ROW_TILES = D_MODEL // LANES
C_PAD_DIM = LANES
V_EXT = LANES
A_QSCALE = A_HEAD_DIM ** -0.5 * LOG2E
C_QSCALE = (C_NOPE_DIM + C_ROPE_DIM) ** -0.5 * LOG2E

A_W = A_HEADS * A_HEAD_DIM
B_W = B_HEADS * B_VAL_DIM
C_W = C_HEADS * C_V_DIM
B_KW = B_HEADS * B_KEY_DIM
C_QW = C_HEADS * C_PAD_DIM
A_VW = A_HEADS * V_EXT
C_VW = C_HEADS * V_EXT

P_AQ, P_AK, P_AV = 0, 384, 768
P_BQ, P_BK, P_BV, P_BG = 1152, 1408, 1664, 2048
P_CQ, P_CKV, P_CKR, P_BLR = 2432, 2688, 2816, 2912
IN_P = 2944
S_BQ, S_BK, S_BV, S_LA, S_BG, S_W = 0, 256, 512, 896, 1152, 1536

ROW_TILE = 512
BAND_ROWS = 256
GLA_CHUNKS = 4
MLA_TILE = 512
MOE_BLOCK = 256
COMBINE_TILE = 128
ROW_TOK_CHUNK = 4096
VMEM_LIMIT = 56 * 1024 * 1024


def _cparams(*sem):
    return pltpu.CompilerParams(dimension_semantics=sem, vmem_limit_bytes=VMEM_LIMIT)


def _rms(x, g):
    return x * lax.rsqrt(jnp.mean(x * x, axis=-1, keepdims=True) + EPS) * g


def _rope_lanes(x, cos_t, sin_t):
    w = x.shape[1]
    lane = lax.broadcasted_iota(i32, x.shape, 1) % LANES
    first_half = jnp.logical_and(lane >= C_NOPE_DIM, lane < C_NOPE_DIM + C_ROPE_DIM // 2)
    swapped = jnp.where(first_half, pltpu.roll(x, w - C_ROPE_DIM // 2, 1), pltpu.roll(x, C_ROPE_DIM // 2, 1))
    return x * cos_t + swapped * sin_t


def _dot(a, b):
    return jnp.dot(a, b, preferred_element_type=f32)


def _dot_nt(a, b):
    return lax.dot_general(a, b, (((1,), (1,)), ((), ())), preferred_element_type=f32)


def _dot_tn(a, b):
    return lax.dot_general(a, b, (((0,), (0,)), ((), ())), preferred_element_type=f32)


def _sigmoid(x):
    return 1.0 / (1.0 + jnp.exp(-x))


def _ones_column(rows):
    lane = lax.broadcasted_iota(i32, (rows, V_EXT - C_V_DIM), 1)
    return jnp.where(lane == 0, 1.0, 0.0).astype(bf16)


def _group_mean_sq(x, g):
    sq = x * x
    hi = sq.astype(bf16)
    lo = (sq - hi.astype(f32)).astype(bf16)
    return _dot(hi, g) + _dot(lo, g)


def _group_rms(x, g, gain):
    return x * lax.rsqrt(_group_mean_sq(x, g) + EPS) * gain


def _softmax_pv(s, v_blocks, widths):
    p = jnp.exp2(s - jnp.max(s, axis=-1, keepdims=True)).astype(bf16)
    acc, lo = None, 0
    for v, w in zip(v_blocks, widths):
        part = _dot(p[:, lo:lo + w], v)
        acc = part if acc is None else acc + part
        lo += w
    return acc[:, :C_V_DIM] / acc[:, C_V_DIM:C_V_DIM + 1]


def _proj_body(x_ref, g1_ref, w_ref, ga_ref, aqg_ref, akg_ref, w2_ref, gb_ref, qag_ref, wuq_ref,
               gc_ref, cqg_ref, kvag_ref, krg_ref, cos_ref, sin_ref,
               aq_o, ak_o, av_o, akb_o, avb_o, b_o, cq_o, ckv_o, ckr_o, h_sc):
    tm = x_ref.shape[0]
    xn = _rms(x_ref[...], g1_ref[...])
    h_sc[...] = _dot(xn.astype(bf16), w_ref[...])
    cos_t = cos_ref[...]
    sin_t = sin_ref[...]
    q = _group_rms(h_sc[:, P_AQ:P_AQ + A_W], ga_ref[...], aqg_ref[...])
    aq_o[...] = (q * A_QSCALE).astype(bf16)
    k = _group_rms(h_sc[:, P_AK:P_AK + A_W], ga_ref[...], akg_ref[...])
    ak_o[...] = k
    akb_o[...] = k.astype(bf16)
    av_o[...] = h_sc[:, P_AV:P_AV + A_W]
    ones = _ones_column(tm)
    for hd in range(A_HEADS):
        lo = P_AV + hd * A_HEAD_DIM
        avb_o[:, hd * V_EXT:hd * V_EXT + A_HEAD_DIM] = h_sc[:, lo:lo + A_HEAD_DIM].astype(bf16)
        avb_o[:, hd * V_EXT + A_HEAD_DIM:(hd + 1) * V_EXT] = ones
    b_o[:, S_BQ:S_LA] = h_sc[:, P_BQ:P_BG]
    z = jnp.dot(h_sc[:, P_BLR:P_BLR + B_GATE_RANK], w2_ref[...], preferred_element_type=f32,
                precision=lax.Precision.HIGHEST) + gb_ref[...]
    log_sig = jnp.minimum(z, 0.0) - jnp.log1p(jnp.exp(-jnp.abs(z)))
    b_o[:, S_LA:S_BG] = log_sig * (1.0 / B_GATE_TAU)
    b_o[:, S_BG:S_W] = h_sc[:, P_BG:P_BG + B_W]
    cql = _rms(h_sc[:, P_CQ:P_CQ + C_Q_RANK], qag_ref[...])
    cq = _group_rms(_dot(cql.astype(bf16), wuq_ref[...]), gc_ref[...], cqg_ref[...])
    cos4 = jnp.concatenate([cos_t] * C_HEADS, axis=1)
    sin4 = jnp.concatenate([sin_t] * C_HEADS, axis=1)
    cq_o[...] = (_rope_lanes(cq, cos4, sin4) * C_QSCALE).astype(bf16)
    ckv_o[...] = _rms(h_sc[:, P_CKV:P_CKV + C_KV_RANK], kvag_ref[...])
    kr = _group_rms(h_sc[:, P_CKR:P_CKR + LANES], gc_ref[0:LANES, 0:LANES], krg_ref[...])
    ckr_o[...] = _rope_lanes(kr, cos_t, sin_t)[:, C_NOPE_DIM:C_NOPE_DIM + C_ROPE_DIM]


def _proj(h_all, row0, n, lw, cos, sin):
    tm = ROW_TILE
    blk0 = row0 // tm
    npos = cos.shape[0] // tm
    row = lambda w: pl.BlockSpec((tm, w), lambda i: (i, 0))
    const = lambda a: pl.BlockSpec(a.shape, lambda i: (0,) * a.ndim)
    pos = pl.BlockSpec((tm, LANES), lambda i: (i % npos, 0))
    consts = (lw['ln1_g'], lw['w_in'], lw['g_a'], lw['a_q_g'], lw['a_k_g'], lw['b_gate_w2'], lw['b_gate_b'],
              lw['c_qa_g'], lw['c_w_uq'], lw['g_c'], lw['c_q_gain'], lw['c_kva_g'], lw['c_kr_gain'])
    out_w = ((A_W, bf16), (A_W, f32), (A_W, f32), (A_W, bf16), (A_VW, bf16), (S_W, f32),
             (C_QW, bf16), (C_KV_RANK, f32), (C_ROPE_DIM, f32))
    return pl.pallas_call(
        _proj_body,
        grid=(n // tm,),
        in_specs=[pl.BlockSpec((tm, D_MODEL), lambda i: (i + blk0, 0))] + [const(a) for a in consts] + [pos, pos],
        out_specs=[row(w) for w, _ in out_w],
        out_shape=[jax.ShapeDtypeStruct((n, w), dt) for w, dt in out_w],
        scratch_shapes=[pltpu.VMEM((tm, IN_P), f32)],
        compiler_params=_cparams("parallel"),
        name="proj",
    )(h_all, *consts, cos, sin)


def _expand_keys(kv, kr, g_k, k_gain, place, k_dst, v_dst, rows):
    kn = _group_rms(kv[:, :C_HEADS * C_NOPE_DIM], g_k, k_gain).astype(bf16)
    k_dst[rows, :] = _dot(jnp.concatenate([kn, kr], axis=1), place).astype(bf16)
    v = kv[:, C_HEADS * C_NOPE_DIM:]
    lane = lax.broadcasted_iota(i32, v.shape, 1) % LANES
    v_dst[rows, :] = jnp.where(lane == C_V_DIM, 1.0, v).astype(bf16)


def _mla_keys_body(ckv_ref, ckr_ref, w_ref, gk_ref, kg_ref, pl_ref, ck_o, cv_o):
    kv = _dot(ckv_ref[...].astype(bf16), w_ref[...])
    _expand_keys(kv, ckr_ref[...].astype(bf16), gk_ref[...], kg_ref[...], pl_ref[...], ck_o, cv_o, slice(None))


def _mla_keys(ckv, ckr, lw):
    n = ckv.shape[0]
    tm = ROW_TILE
    row = lambda w: pl.BlockSpec((tm, w), lambda i: (i, 0))
    const = lambda a: pl.BlockSpec(a.shape, lambda i: (0,) * a.ndim)
    consts = (lw['c_w_ukv'], lw['g_k'], lw['c_k_gain'], lw['k_place'])
    return pl.pallas_call(
        _mla_keys_body,
        grid=(n // tm,),
        in_specs=[row(C_KV_RANK), row(C_ROPE_DIM)] + [const(a) for a in consts],
        out_specs=[row(C_QW), row(C_VW)],
        out_shape=[jax.ShapeDtypeStruct((n, C_QW), bf16), jax.ShapeDtypeStruct((n, C_VW), bf16)],
        compiler_params=_cparams("parallel"),
        name="mla_keys",
    )(ckv, ckr, *consts)


def _band_body(q_ref, *refs, n_kb, past_from_seq):
    k_refs, v_refs = refs[:n_kb], refs[n_kb:2 * n_kb]
    bias_ref, o_ref = refs[2 * n_kb], refs[2 * n_kb + 1]
    rows = q_ref.shape[0]
    widths = [k.shape[0] for k in k_refs]
    if past_from_seq:
        col = lax.broadcasted_iota(i32, (rows, sum(widths)), 1)
        exists = col + (pl.program_id(1) * rows - A_PAST) >= 0
    for hd in range(A_HEADS):
        qs = slice(hd * A_HEAD_DIM, (hd + 1) * A_HEAD_DIM)
        vs = slice(hd * V_EXT, (hd + 1) * V_EXT)
        q = q_ref[:, qs]
        s = jnp.concatenate([_dot_nt(q, k[:, qs]) for k in k_refs], axis=1) + bias_ref[hd]
        if past_from_seq:
            s = jnp.where(exists, s, NEG)
        o_ref[:, qs] = _softmax_pv(s, [v[:, vs] for v in v_refs], widths).astype(bf16)


def _band(q, k, v, k_past, v_past, bias):
    b_, t, _ = q.shape
    from_seq = k_past is None
    rows = BAND_ROWS if from_seq else t
    n_back = A_PAST // rows if from_seq else 1
    cur = lambda w: pl.BlockSpec((None, rows, w), lambda b, i: (b, i, 0))
    if from_seq:
        back = lambda w, d: pl.BlockSpec((None, rows, w), lambda b, i: (b, jnp.maximum(i - d, 0), 0))
        k_specs = [back(A_W, d) for d in range(n_back, 0, -1)] + [cur(A_W)]
        v_specs = [back(A_VW, d) for d in range(n_back, 0, -1)] + [cur(A_VW)]
        k_args, v_args = [k] * (n_back + 1), [v] * (n_back + 1)
    else:
        past = lambda w: pl.BlockSpec((None, A_PAST, w), lambda b, i: (b, 0, 0))
        k_specs, v_specs = [past(A_W), cur(A_W)], [past(A_VW), cur(A_VW)]
        k_args, v_args = [k_past, k], [v_past, v]
    return pl.pallas_call(
        functools.partial(_band_body, n_kb=len(k_specs), past_from_seq=from_seq),
        grid=(b_, t // rows),
        in_specs=[cur(A_W)] + k_specs + v_specs + [pl.BlockSpec(bias.shape, lambda b, i: (0, 0, 0))],
        out_specs=cur(A_W),
        out_shape=jax.ShapeDtypeStruct((b_, t, A_W), bf16),
        compiler_params=_cparams("parallel", "parallel"),
        name="band",
    )(q, *k_args, *v_args, bias)


def _gla_body(b_ref, s0_ref, gain_ref, o_ref, st_ref, *, chunks):
    @pl.when(pl.program_id(1) == 0)
    def _():
        st_ref[...] = s0_ref[...]

    rows = chunks * CHUNK
    r = lax.broadcasted_iota(i32, (rows, rows), 0)
    c = lax.broadcasted_iota(i32, (rows, rows), 1)
    causal = jnp.logical_and(r >= c, r // CHUNK == c // CHUNK)
    tril = (lax.broadcasted_iota(i32, (CHUNK, CHUNK), 0) >= lax.broadcasted_iota(i32, (CHUNK, CHUNK), 1)).astype(f32)
    bcum, btot = [], []
    for ch in range(chunks):
        la = b_ref[ch * CHUNK:(ch + 1) * CHUNK, S_LA:S_LA + B_KW]
        cum = jnp.dot(tril, la, preferred_element_type=f32, precision=lax.Precision.HIGHEST)
        bcum.append(cum)
        btot.append(jnp.broadcast_to(cum[CHUNK - 1:CHUNK, :], cum.shape))
    bcum = jnp.concatenate(bcum, axis=0)
    btot = jnp.concatenate(btot, axis=0)
    k = b_ref[:, S_BK:S_BK + B_KW]
    qt = b_ref[:, S_BQ:S_BQ + B_KW] * (B_KEY_DIM ** -0.5) * jnp.exp(bcum)
    kt = k * jnp.exp(-bcum)
    kd = k * jnp.exp(btot - bcum)
    decay = jnp.exp(btot)
    for hd in range(B_HEADS):
        ks = slice(hd * B_KEY_DIM, (hd + 1) * B_KEY_DIM)
        vs = slice(hd * B_VAL_DIM, (hd + 1) * B_VAL_DIM)
        v = b_ref[:, S_BV + vs.start:S_BV + vs.stop]
        att = jnp.where(causal, _dot_nt(qt[:, ks], kt[:, ks]), 0.0)
        o_intra = _dot(att, v)
        st = st_ref[hd]
        outs = []
        for ch in range(chunks):
            rs = slice(ch * CHUNK, (ch + 1) * CHUNK)
            outs.append(o_intra[rs] + _dot_nt(qt[rs, ks], st))
            st = st * decay[ch * CHUNK:ch * CHUNK + 1, ks] + _dot_tn(v[rs], kd[rs, ks])
        st_ref[hd] = st
        g = b_ref[:, S_BG + vs.start:S_BG + vs.stop]
        o = _rms(jnp.concatenate(outs, axis=0), gain_ref[...]) * (g * _sigmoid(g))
        o_ref[:, vs] = o.astype(bf16)


def _gla(slab, s0_t, gain):
    b_, t, _ = slab.shape
    chunks = min(GLA_CHUNKS, t // CHUNK)
    rows = chunks * CHUNK
    st_spec = pl.BlockSpec((None, B_HEADS, B_VAL_DIM, B_KEY_DIM), lambda b, i: (b, 0, 0, 0))
    return pl.pallas_call(
        functools.partial(_gla_body, chunks=chunks),
        grid=(b_, t // rows),
        in_specs=[pl.BlockSpec((None, rows, S_W), lambda b, i: (b, i, 0)), st_spec,
                  pl.BlockSpec(gain.shape, lambda b, i: (0, 0))],
        out_specs=[pl.BlockSpec((None, rows, B_W), lambda b, i: (b, i, 0)), st_spec],
        out_shape=[jax.ShapeDtypeStruct((b_, t, B_W), bf16),
                   jax.ShapeDtypeStruct((b_, B_HEADS, B_VAL_DIM, B_KEY_DIM), f32)],
        compiler_params=_cparams("parallel", "arbitrary"),
        name="gla",
    )(slab, s0_t, gain)


def _mla_causal_body(q_ref, k_ref, v_ref, o_ref, m_sc, acc_sc):
    i = pl.program_id(1)
    t = q_ref.shape[0]
    m_sc[...] = jnp.full(m_sc.shape, NEG, f32)
    acc_sc[...] = jnp.zeros(acc_sc.shape, f32)

    def tile(j, diagonal):
        rows = pl.ds(pl.multiple_of(j * t, t), t)
        if diagonal:
            visible = (lax.broadcasted_iota(i32, (t, t), 1) // CHUNK) <= (lax.broadcasted_iota(i32, (t, t), 0) // CHUNK)
        for hd in range(C_HEADS):
            hs = slice(hd * C_PAD_DIM, (hd + 1) * C_PAD_DIM)
            s = _dot_nt(q_ref[:, hs], k_ref[rows, hs])
            if diagonal:
                s = jnp.where(visible, s, NEG)
            m_old = m_sc[hd]
            m_new = jnp.maximum(m_old, jnp.max(s, axis=-1, keepdims=True))
            p = jnp.exp2(s - jnp.concatenate([m_new] * (t // LANES), axis=1))
            acc_sc[hd] = jnp.exp2(m_old - m_new) * acc_sc[hd] + _dot(p.astype(bf16), v_ref[rows, hs])
            m_sc[hd] = m_new

    def earlier(j, carry):
        tile(j, False)
        return carry

    lax.fori_loop(0, i, earlier, 0)
    tile(i, True)
    for hd in range(C_HEADS):
        acc = acc_sc[hd]
        o_ref[:, hd * C_V_DIM:(hd + 1) * C_V_DIM] = (acc[:, :C_V_DIM] / acc[:, C_V_DIM:C_V_DIM + 1]).astype(bf16)


def _mla_causal(q, k, v):
    b_, t, _ = q.shape
    tq = MLA_TILE
    whole = lambda w: pl.BlockSpec((None, t, w), lambda b, i: (b, 0, 0))
    return pl.pallas_call(
        _mla_causal_body,
        grid=(b_, t // tq),
        in_specs=[pl.BlockSpec((None, tq, C_QW), lambda b, i: (b, i, 0)), whole(C_QW), whole(C_VW)],
        out_specs=pl.BlockSpec((None, tq, C_W), lambda b, i: (b, i, 0)),
        out_shape=jax.ShapeDtypeStruct((b_, t, C_W), bf16),
        scratch_shapes=[pltpu.VMEM((C_HEADS, tq, LANES), f32), pltpu.VMEM((C_HEADS, tq, V_EXT), f32)],
        compiler_params=_cparams("parallel", "arbitrary"),
        name="mla_causal",
    )(q, k, v)


def _mla_sample_body(q_ref, ckv_ref, ckr_ref, kn_ref, vn_ref, w_ref, gk_ref, kg_ref, pl_ref, o_ref, k_sc, v_sc):
    past = ckv_ref.shape[0]
    kv = _dot(ckv_ref[...].astype(bf16), w_ref[...])
    _expand_keys(kv, ckr_ref[...].astype(bf16), gk_ref[...], kg_ref[...], pl_ref[...], k_sc, v_sc, slice(0, past))
    k_sc[past:, :] = kn_ref[...]
    v_sc[past:, :] = vn_ref[...]
    total = k_sc.shape[0]
    for hd in range(C_HEADS):
        hs = slice(hd * C_PAD_DIM, (hd + 1) * C_PAD_DIM)
        s = _dot_nt(q_ref[:, hs], k_sc[:, hs])
        o_ref[:, hd * C_V_DIM:(hd + 1) * C_V_DIM] = _softmax_pv(s, [v_sc[:, hs]], [total]).astype(bf16)


def _mla_sample(q, c_kv, c_kr, k_new, v_new, lw):
    b_, s_len, _ = q.shape
    past = c_kv.shape[1]
    per_b = lambda r, w: pl.BlockSpec((None, r, w), lambda b: (b, 0, 0))
    const = lambda a: pl.BlockSpec(a.shape, lambda b: (0,) * a.ndim)
    consts = (lw['c_w_ukv'], lw['g_k'], lw['c_k_gain'], lw['k_place'])
    return pl.pallas_call(
        _mla_sample_body,
        grid=(b_,),
        in_specs=[per_b(s_len, C_QW), per_b(past, C_KV_RANK), per_b(past, C_ROPE_DIM), per_b(s_len, C_QW),
                  per_b(s_len, C_VW)] + [const(a) for a in consts],
        out_specs=per_b(s_len, C_W),
        out_shape=jax.ShapeDtypeStruct((b_, s_len, C_W), bf16),
        scratch_shapes=[pltpu.VMEM((past + s_len, C_QW), bf16), pltpu.VMEM((past + s_len, C_VW), bf16)],
        compiler_params=_cparams("parallel"),
        name="mla_sample",
    )(q, c_kv, c_kr, k_new, v_new, *consts)


def _outproj_body(ap_ref, bp_ref, cp_ref, as_ref, bs_ref, cs_ref, h_ref, w_ref, g2_ref, rwh_ref, rwl_ref, rb_ref,
                  h_o, xn_o, lg_o, *, prompt_tiles):
    def run(a_ref, b_ref, c_ref):
        y = _dot(a_ref[...], w_ref[0:A_W, :])
        y += _dot(b_ref[...], w_ref[A_W:A_W + B_W, :])
        y += _dot(c_ref[...], w_ref[A_W + B_W:, :])
        h = h_ref[...] + y
        h_o[...] = h
        xn = _rms(h, g2_ref[...])
        for c in range(ROW_TILES):
            xn_o[pl.ds(c, h.shape[0], stride=ROW_TILES), :] = xn[:, c * LANES:(c + 1) * LANES]
        xh = xn.astype(bf16)
        xl = (xn - xh.astype(f32)).astype(bf16)
        lg_o[...] = _dot(xh, rwh_ref[...]) + _dot(xl, rwh_ref[...]) + _dot(xh, rwl_ref[...]) + rb_ref[...]

    i = pl.program_id(0)
    pl.when(i < prompt_tiles)(functools.partial(run, ap_ref, bp_ref, cp_ref))
    pl.when(i >= prompt_tiles)(functools.partial(run, as_ref, bs_ref, cs_ref))


def _outproj(mix_p, mix_s, h_all, lw):
    n = h_all.shape[0]
    tm = ROW_TILE
    pt = mix_p[0].shape[0] // tm
    row = lambda w: pl.BlockSpec((tm, w), lambda i: (i, 0))
    first = lambda w: pl.BlockSpec((tm, w), lambda i: (jnp.minimum(i, pt - 1), 0))
    second = lambda w: pl.BlockSpec((tm, w), lambda i: (jnp.maximum(i - pt, 0), 0))
    const = lambda a: pl.BlockSpec(a.shape, lambda i: (0,) * a.ndim)
    consts = (lw['w_out'], lw['ln2_g'], lw['router_w_hi'], lw['router_w_lo'], lw['router_b'])
    return pl.pallas_call(
        functools.partial(_outproj_body, prompt_tiles=pt),
        grid=(n // tm,),
        in_specs=[first(A_W), first(B_W), first(C_W), second(A_W), second(B_W), second(C_W), row(D_MODEL)]
        + [const(a) for a in consts],
        out_specs=[row(D_MODEL), pl.BlockSpec((tm * ROW_TILES, LANES), lambda i: (i, 0)), row(N_EXPERTS)],
        out_shape=[jax.ShapeDtypeStruct((n, D_MODEL), f32), jax.ShapeDtypeStruct((n * ROW_TILES, LANES), f32),
                   jax.ShapeDtypeStruct((n, N_EXPERTS), f32)],
        compiler_params=_cparams("parallel"),
        name="outproj",
    )(*mix_p, *mix_s, h_all, *consts)


def _route_body(lg_ref, e_o, g_o, r_o, cnt_o):
    @pl.when(pl.program_id(0) == 0)
    def _():
        cnt_o[...] = jnp.zeros(cnt_o.shape, f32)

    tm = lg_ref.shape[0]
    lane = lax.broadcasted_iota(i32, (tm, N_EXPERTS), 1)
    slot = lax.broadcasted_iota(i32, (tm, TOP_K), 1)
    work = lg_ref[...]
    onehots, vals = [], []
    e_out = jnp.zeros((tm, TOP_K), i32)
    for k in range(TOP_K):
        m = jnp.max(work, axis=-1, keepdims=True)
        idx = jnp.min(jnp.where(work == m, lane, N_EXPERTS), axis=-1, keepdims=True)
        oh = lane == idx
        work = jnp.where(oh, -jnp.inf, work)
        onehots.append(oh)
        vals.append(m)
        e_out = jnp.where(slot == k, idx, e_out)
    ex = [jnp.exp(v - vals[0]) for v in vals]
    denom = ex[0] + ex[1] + ex[2] + ex[3]
    g_out = jnp.zeros((tm, TOP_K), f32)
    sel = jnp.zeros((tm, N_EXPERTS), f32)
    for k in range(TOP_K):
        g_out = jnp.where(slot == k, ex[k] / denom, g_out)
        sel = sel + onehots[k].astype(f32)
    r = lax.broadcasted_iota(i32, (tm, tm), 0)
    c = lax.broadcasted_iota(i32, (tm, tm), 1)
    before = (c < r).astype(bf16)
    rank = _dot(before, sel.astype(bf16)) + cnt_o[...]
    r_out = jnp.zeros((tm, TOP_K), f32)
    for k in range(TOP_K):
        rk = jnp.sum(jnp.where(onehots[k], rank, 0.0), axis=-1, keepdims=True)
        r_out = jnp.where(slot == k, rk, r_out)
    e_o[...] = e_out
    g_o[...] = g_out
    r_o[...] = r_out.astype(i32)
    cnt_o[...] += jnp.sum(sel, axis=0, keepdims=True)


def _route(logits):
    n = logits.shape[0]
    tm = ROW_TILE
    row = lambda w: pl.BlockSpec((tm, w), lambda i: (i, 0))
    return pl.pallas_call(
        _route_body,
        grid=(n // tm,),
        in_specs=[row(N_EXPERTS)],
        out_specs=[row(TOP_K), row(TOP_K), row(TOP_K), pl.BlockSpec((1, N_EXPERTS), lambda i: (0, 0))],
        out_shape=[jax.ShapeDtypeStruct((n, TOP_K), i32), jax.ShapeDtypeStruct((n, TOP_K), f32),
                   jax.ShapeDtypeStruct((n, TOP_K), i32), jax.ShapeDtypeStruct((1, N_EXPERTS), f32)],
        compiler_params=_cparams("arbitrary"),
        name="route",
    )(logits)


def _row_copy(src_hbm, tok, dst, slot, j, sem):
    return pltpu.make_async_copy(src_hbm.at[pl.ds(pl.multiple_of(tok * ROW_TILES, ROW_TILES), ROW_TILES), :],
                                 dst.at[slot, pl.ds(j * ROW_TILES, ROW_TILES), :], sem.at[slot])


def _experts_body(blk_e_ref, row_tok_ref, used_ref, x_hbm, wg_ref, bg_ref, wu_ref, bu_ref, wd_ref, bd_ref, y_o,
                  xbuf, xb, wbf, sem):
    b = pl.program_id(0)
    bm = MOE_BLOCK
    last = used_ref[0] - 1
    slot = b % 2

    def wait_block(s):
        pltpu.make_async_copy(x_hbm.at[pl.ds(0, bm * ROW_TILES), :], xbuf.at[s], sem.at[s]).wait()

    @pl.when(b == 0)
    def _():
        def issue(j, carry):
            _row_copy(x_hbm, row_tok_ref[j], xbuf, 0, j, sem).start()
            return carry
        lax.fori_loop(0, bm, issue, 0)

    @pl.when(b <= last)
    def _():
        wait_block(slot)
        xb[...] = jnp.concatenate([xbuf[slot, pl.ds(c, bm, stride=ROW_TILES), :] for c in range(ROW_TILES)],
                                  axis=1).astype(bf16)
        nxt = jnp.minimum(b + 1, last) * bm
        for s in range(2):
            @pl.when(slot != s)
            def _(s=s):
                for j in range(bm):
                    _row_copy(x_hbm, row_tok_ref[nxt + j], xbuf, s, j, sem).start(priority=j % 2)

        @pl.when(jnp.logical_or(b == 0, blk_e_ref[b] != blk_e_ref[jnp.maximum(b - 1, 0)]))
        def _():
            wbf[0] = wg_ref[...].astype(bf16)
            wbf[1] = wu_ref[...].astype(bf16)
            wbf[2] = wd_ref[...].astype(bf16)

        x = xb[...]
        g = _dot(x, wbf[0]) + bg_ref[...]
        u = _dot(x, wbf[1]) + bu_ref[...]
        g = jnp.minimum(g, SWIGLU_LIMIT)
        u = jnp.clip(u, -SWIGLU_LIMIT, SWIGLU_LIMIT)
        hdn = (u + 1.0) * (g * _sigmoid(g * SWIGLU_ALPHA))
        y_o[...] = _dot(hdn.astype(bf16), wbf[2]) + bd_ref[...]

    @pl.when(b == last)
    def _():
        wait_block(1 - slot)

    @pl.when(b > last)
    def _():
        y_o[...] = jnp.zeros(y_o.shape, f32)


def _experts(x, blk_e, row_tok, n_used, layer, ew):
    nblk = blk_e.shape[0]
    bm = MOE_BLOCK
    wspec = lambda: pl.BlockSpec((None, None, D_MODEL, D_EXPERT), lambda b, be, rt, nu: (layer, be[b], 0, 0))
    bspec = lambda: pl.BlockSpec((None, None, 1, D_EXPERT), lambda b, be, rt, nu: (layer, be[b], 0, 0))
    grid_spec = pltpu.PrefetchScalarGridSpec(
        num_scalar_prefetch=3,
        grid=(nblk,),
        in_specs=[pl.BlockSpec(memory_space=pl.ANY), wspec(), bspec(), wspec(), bspec(), wspec(), bspec()],
        out_specs=pl.BlockSpec((bm, D_MODEL), lambda b, be, rt, nu: (b, 0)),
        scratch_shapes=[pltpu.VMEM((2, bm * ROW_TILES, LANES), f32), pltpu.VMEM((bm, D_MODEL), bf16),
                        pltpu.VMEM((3, D_MODEL, D_EXPERT), bf16), pltpu.SemaphoreType.DMA((2,))],
    )
    return pl.pallas_call(
        _experts_body,
        grid_spec=grid_spec,
        out_shape=jax.ShapeDtypeStruct((nblk * bm, D_MODEL), f32),
        compiler_params=_cparams("arbitrary"),
        name="experts",
    )(blk_e, row_tok, n_used, x, ew['e_w_gate'], ew['e_b_gate'], ew['e_w_up'], ew['e_b_up'],
      ew['e_w_down'], ew['e_b_down'])


def _combine_body(dest_ref, y_hbm, h_ref, g_ref, o_ref, ybuf, sem):
    i = pl.program_id(0)
    nt = pl.num_programs(0)
    tt = COMBINE_TILE

    def row_copy(tile, slot, j, k):
        return pltpu.make_async_copy(y_hbm.at[pl.ds(dest_ref[(tile * tt + j) * TOP_K + k], 1), :],
                                     ybuf.at[slot, k, pl.ds(j, 1), :], sem.at[slot])

    def wait_tile(slot):
        for k in range(TOP_K):
            pltpu.make_async_copy(y_hbm.at[pl.ds(0, tt), :], ybuf.at[slot, k], sem.at[slot]).wait()

    slot = i % 2

    @pl.when(i == 0)
    def _():
        def issue(j, carry):
            for k in range(TOP_K):
                row_copy(0, 0, j, k).start(priority=k % 2)
            return carry
        lax.fori_loop(0, tt, issue, 0)

    for s in range(2):
        @pl.when(jnp.logical_and(i + 1 < nt, slot != s))
        def _(s=s):
            for j in range(tt):
                for k in range(TOP_K):
                    row_copy(i + 1, s, j, k).start(priority=k % 2)

    wait_tile(slot)
    g = g_ref[...]
    acc = h_ref[...]
    for k in range(TOP_K):
        acc = acc + ybuf[slot, k] * g[:, k:k + 1]
    o_ref[...] = acc


def _combine(y_rows, dest, h, gates):
    n = h.shape[0]
    tt = COMBINE_TILE
    grid_spec = pltpu.PrefetchScalarGridSpec(
        num_scalar_prefetch=1,
        grid=(n // tt,),
        in_specs=[pl.BlockSpec(memory_space=pl.ANY),
                  pl.BlockSpec((tt, D_MODEL), lambda i, d: (i, 0)),
                  pl.BlockSpec((tt, TOP_K), lambda i, d: (i, 0))],
        out_specs=pl.BlockSpec((tt, D_MODEL), lambda i, d: (i, 0)),
        scratch_shapes=[pltpu.VMEM((2, TOP_K, tt, D_MODEL), f32), pltpu.SemaphoreType.DMA((2,))],
    )
    return pl.pallas_call(
        _combine_body,
        grid_spec=grid_spec,
        out_shape=jax.ShapeDtypeStruct((n, D_MODEL), f32),
        compiler_params=_cparams("arbitrary"),
        name="combine",
    )(dest, y_rows, h, gates)


def _row_tok_body(dest_hbm, zeros_hbm, out_ref, buf0, buf1, sem, zsem):
    ch = buf0.shape[0]
    nchunks = dest_hbm.shape[0] // ch
    bufs = (buf0, buf1)
    fill = pltpu.make_async_copy(zeros_hbm, out_ref, zsem)
    fill.start()
    fill.wait()

    def copy(c, slot):
        return pltpu.make_async_copy(dest_hbm.at[pl.ds(pl.multiple_of(c * ch, ch), ch)], bufs[slot], sem.at[slot])

    copy(0, 0).start()

    def per_pair(p, carry):
        for slot in range(2):
            c = 2 * p + slot

            @pl.when(c + 1 < nchunks)
            def _():
                copy(c + 1, 1 - slot).start()

            copy(c, slot).wait()
            base = c * (ch // TOP_K)

            def scatter(t, cc):
                for k in range(TOP_K):
                    out_ref[bufs[slot][TOP_K * t + k]] = base + t
                return cc

            lax.fori_loop(0, ch // TOP_K, scatter, 0, unroll=8)
        return carry

    lax.fori_loop(0, nchunks // 2, per_pair, 0)


def _row_tok(dest, rows):
    ch = min(ROW_TOK_CHUNK, dest.shape[0] // 2)
    assert dest.shape[0] % (2 * ch) == 0
    return pl.pallas_call(
        _row_tok_body,
        in_specs=[pl.BlockSpec(memory_space=pl.ANY), pl.BlockSpec(memory_space=pl.ANY)],
        out_specs=pl.BlockSpec(memory_space=pltpu.SMEM),
        out_shape=jax.ShapeDtypeStruct((rows,), i32),
        scratch_shapes=[pltpu.SMEM((ch,), i32), pltpu.SMEM((ch,), i32), pltpu.SemaphoreType.DMA((2,)),
                        pltpu.SemaphoreType.DMA(())],
        name="row_tok",
    )(dest, jnp.zeros((rows,), i32))


def _moe(h, xn, logits, layer, ew):
    n = h.shape[0]
    bm = MOE_BLOCK
    top_e, gates, rank, counts = _route(logits)
    counts = counts[0].astype(i32)
    padded = ((counts + bm - 1) // bm) * bm
    pend = jnp.cumsum(padded)
    pstart = pend - padded
    expert = jnp.arange(N_EXPERTS, dtype=i32)
    dest = jnp.sum(jnp.where(top_e[:, :, None] == expert, pstart, 0), axis=-1) + rank
    nblk = (n * TOP_K) // bm + N_EXPERTS
    row_tok = _row_tok(dest.reshape(-1), nblk * bm)
    blk_start = jnp.arange(nblk, dtype=i32) * bm
    blk_e = jnp.minimum(jnp.sum((pend[None, :] <= blk_start[:, None]).astype(i32), axis=1), N_EXPERTS - 1)
    y_rows = _experts(xn, blk_e, row_tok, pend[-1:] // bm, layer, ew)
    return _combine(y_rows, dest.reshape(-1), h, gates)


def _rope_tables(pos, tile_rows):
    half = C_ROPE_DIM // 2
    inv = ROPE_THETA ** (-jnp.arange(half, dtype=f32) / half)
    ang = pos.astype(f32)[:, None] * inv[None, :]
    cos, sin = jnp.cos(ang), jnp.sin(ang)
    n = ang.shape[0]
    tail = jnp.zeros((n, LANES - C_NOPE_DIM - C_ROPE_DIM), f32)
    cos_t = jnp.concatenate([jnp.ones((n, C_NOPE_DIM), f32), cos, cos, tail], axis=1)
    sin_t = jnp.concatenate([jnp.zeros((n, C_NOPE_DIM), f32), -sin, sin, tail], axis=1)
    reps = max(1, tile_rows // n)
    return jnp.tile(cos_t, (reps, 1)), jnp.tile(sin_t, (reps, 1))


def _band_bias(rel_bias, rows):
    cols = A_PAST + rows
    period = cols + rows
    x = jnp.arange(period)
    x = jnp.where(x < cols, x, x - period)
    idx = jnp.clip(A_PAST - x, -A_REL_CLIP, A_REL_CLIP) + A_REL_CLIP
    v = rel_bias[:, idx].astype(f32) * LOG2E
    skew = jnp.tile(v, (1, rows))[:, :rows * (period - 1)].reshape(-1, rows, period - 1)
    r = jnp.arange(rows)[:, None]
    c = jnp.arange(cols)[None, :]
    lo = (r // CHUNK) * CHUNK
    in_band = jnp.logical_and(c >= lo, c < lo + A_BAND)
    return jnp.where(in_band[None], skew[:, :, :cols], NEG)


def _group_matrix(width, groups):
    g = np.zeros((width, width), np.float32)
    for start, size in groups:
        g[start:start + size, start:start + size] = 1.0 / size
    return jnp.asarray(g, bf16)


def _layer_weights(l, p):
    w = p['w_in'][l]
    seg = lambda o, n: w[:, o:o + n]
    zc = lambda n: jnp.zeros((D_MODEL, n), f32)
    o_aq, o_ak, o_av, o_bq, o_bk, o_bv = 0, 384, 768, 1152, 1408, 1664
    o_blr, o_bg, o_cq, o_ckv, o_ckr = 2048, 2064, 2448, 2640, 2768
    w_in = jnp.concatenate([
        seg(o_aq, 384), seg(o_ak, 384), seg(o_av, 384), seg(o_bq, 256), seg(o_bk, 256), seg(o_bv, 384),
        seg(o_bg, 384), seg(o_cq, C_Q_RANK), zc(P_CKV - P_CQ - C_Q_RANK), seg(o_ckv, 128),
        zc(C_NOPE_DIM), seg(o_ckr, 32), seg(o_blr, 16), zc(IN_P - P_BLR - B_GATE_RANK)], axis=1).astype(bf16)
    hq = C_NOPE_DIM + C_ROPE_DIM
    pad_heads = lambda a, n: jnp.pad(a.reshape(a.shape[0], C_HEADS, n), ((0, 0), (0, 0), (0, C_PAD_DIM - n))
                                     ).reshape(a.shape[0], C_QW)
    wuq = pad_heads(p['c_w_uq'][l], hq).astype(bf16)
    wukv = p['c_w_ukv'][l].reshape(C_KV_RANK, C_HEADS, C_NOPE_DIM + C_V_DIM)
    wukv = jnp.concatenate([wukv[:, :, :C_NOPE_DIM].reshape(C_KV_RANK, -1),
                            pad_heads(wukv[:, :, C_NOPE_DIM:].reshape(C_KV_RANK, -1), C_V_DIM)], axis=1).astype(bf16)
    r2 = lambda a: a.reshape(1, -1)
    z = lambda n: jnp.zeros((n,), f32)
    tail = C_PAD_DIM - hq
    head_groups = [(hd * C_PAD_DIM + o, n) for hd in range(C_HEADS) for o, n in ((0, C_NOPE_DIM), (C_NOPE_DIM, C_ROPE_DIM))]
    place = np.zeros((C_HEADS * C_NOPE_DIM + C_ROPE_DIM, C_QW), np.float32)
    for hd in range(C_HEADS):
        place[hd * C_NOPE_DIM + np.arange(C_NOPE_DIM), hd * C_PAD_DIM + np.arange(C_NOPE_DIM)] = 1.0
        place[C_HEADS * C_NOPE_DIM + np.arange(C_ROPE_DIM), hd * C_PAD_DIM + C_NOPE_DIM + np.arange(C_ROPE_DIM)] = 1.0
    rw = p['router_w'][l]
    rw_hi = rw.astype(bf16)
    return {
        'ln1_g': r2(p['ln1_g'][l]), 'ln2_g': r2(p['ln2_g'][l]), 'w_in': w_in,
        'g_a': _group_matrix(A_W, [(hd * A_HEAD_DIM, A_HEAD_DIM) for hd in range(A_HEADS)]),
        'a_q_g': r2(jnp.tile(p['a_q_g'][l], A_HEADS)), 'a_k_g': r2(jnp.tile(p['a_k_g'][l], A_HEADS)),
        'bias_p': _band_bias(p['a_rel_bias'][l], BAND_ROWS), 'bias_s': _band_bias(p['a_rel_bias'][l], CHUNK),
        'b_gate_w2': p['b_gate_w2'][l], 'b_gate_b': r2(p['b_gate_b'][l]), 'b_out_g': r2(p['b_out_g'][l]),
        'c_qa_g': r2(p['c_qa_g'][l]), 'c_w_uq': wuq, 'c_kva_g': r2(p['c_kva_g'][l]), 'c_w_ukv': wukv,
        'g_c': _group_matrix(C_QW, head_groups),
        'c_q_gain': r2(jnp.tile(jnp.concatenate([p['c_qn_g'][l], p['c_qr_g'][l], z(tail)]), C_HEADS)),
        'c_k_gain': r2(jnp.tile(p['c_kn_g'][l], C_HEADS)),
        'g_k': _group_matrix(C_HEADS * C_NOPE_DIM, [(hd * C_NOPE_DIM, C_NOPE_DIM) for hd in range(C_HEADS)]),
        'c_kr_gain': r2(jnp.concatenate([z(C_NOPE_DIM), p['c_kr_g'][l], z(tail)])),
        'k_place': jnp.asarray(place, bf16),
        'w_out': p['w_out'][l].astype(bf16),
        'router_w_hi': rw_hi, 'router_w_lo': (rw - rw_hi.astype(f32)).astype(bf16), 'router_b': r2(p['router_b'][l]),
    }


def _token_mix(h_all, row0, b_, t, lw, cos, sin, past):
    aq, ak, av, akb, avb, slab, cq, ckv, ckr = _proj(h_all, row0, b_ * t, lw, cos, sin)
    r3 = lambda a: a.reshape(b_, t, a.shape[-1])
    ck, cv = _mla_keys(ckv, ckr, lw)
    if past is None:
        a_o = _band(r3(aq), r3(akb), r3(avb), None, None, lw['bias_p'])
        s0_t = jnp.zeros((b_, B_HEADS, B_VAL_DIM, B_KEY_DIM), f32)
        c_o = _mla_causal(r3(cq), r3(ck), r3(cv))
    else:
        a_k, a_v, b_s, c_kv, c_kr = past
        lp = a_k.shape[1]
        v_past = jnp.concatenate([a_v.astype(bf16), jnp.ones(a_v.shape[:-1] + (1,), bf16),
                                  jnp.zeros(a_v.shape[:-1] + (V_EXT - A_HEAD_DIM - 1,), bf16)], axis=-1)
        a_o = _band(r3(aq), r3(akb), r3(avb), a_k.reshape(b_, lp, A_W).astype(bf16),
                    v_past.reshape(b_, lp, A_VW), lw['bias_s'])
        s0_t = jnp.swapaxes(b_s, -1, -2)
        c_o = _mla_sample(r3(cq), c_kv, c_kr, r3(ck), r3(cv), lw)
    b_o, s_t = _gla(r3(slab), s0_t, lw['b_out_g'])
    flat = lambda a: a.reshape(b_ * t, a.shape[-1])
    la = min(A_PAST, t)
    state = (r3(ak)[:, t - la:].reshape(b_, la, A_HEADS, A_HEAD_DIM),
             r3(av)[:, t - la:].reshape(b_, la, A_HEADS, A_HEAD_DIM),
             jnp.swapaxes(s_t, -1, -2), r3(ckv), r3(ckr))
    return (flat(a_o), flat(b_o), flat(c_o)), state


def kernel(x_prompt, x_sample, cache_a_k, cache_a_v, state_b_s, cache_c_kv, cache_c_kr, ln1_g, ln2_g, w_in, a_q_g, a_k_g, a_rel_bias, b_gate_w2, b_gate_b, b_out_g, c_qa_g, c_w_uq, c_kva_g, c_w_ukv, c_qn_g, c_qr_g, c_kn_g, c_kr_g, w_out, router_w, router_b, e_w_gate, e_b_gate, e_w_up, e_b_up, e_w_down, e_b_down):
    params = dict(ln1_g=ln1_g, ln2_g=ln2_g, w_in=w_in, a_q_g=a_q_g, a_k_g=a_k_g, a_rel_bias=a_rel_bias,
                  b_gate_w2=b_gate_w2, b_gate_b=b_gate_b, b_out_g=b_out_g, c_qa_g=c_qa_g, c_w_uq=c_w_uq,
                  c_kva_g=c_kva_g, c_w_ukv=c_w_ukv, c_qn_g=c_qn_g, c_qr_g=c_qr_g, c_kn_g=c_kn_g,
                  c_kr_g=c_kr_g, w_out=w_out, router_w=router_w, router_b=router_b)
    ew = dict(e_w_gate=e_w_gate, e_b_gate=e_b_gate[:, :, None, :], e_w_up=e_w_up, e_b_up=e_b_up[:, :, None, :],
              e_w_down=e_w_down, e_b_down=e_b_down[:, :, None, :])
    bp, tp, _ = x_prompt.shape
    bs, ts, _ = x_sample.shape
    depth = w_in.shape[0]
    n_p = bp * tp
    cos_p, sin_p = _rope_tables(jnp.arange(tp), ROW_TILE)
    cos_s, sin_s = _rope_tables(PAST_LEN + jnp.arange(ts), ROW_TILE)
    h = jnp.concatenate([x_prompt.reshape(n_p, D_MODEL), x_sample.reshape(bs * ts, D_MODEL)])
    st_p, st_s = [], []
    for l in range(depth):
        lw = _layer_weights(l, params)
        mix_p, s_p = _token_mix(h, 0, bp, tp, lw, cos_p, sin_p, None)
        mix_s, s_s = _token_mix(h, n_p, bs, ts, lw, cos_s, sin_s,
                                (cache_a_k[l], cache_a_v[l], state_b_s[l], cache_c_kv[l], cache_c_kr[l]))
        h, xn, logits = _outproj(mix_p, mix_s, h, lw)
        h = _moe(h, xn, logits, l, ew)
        st_p.append(s_p)
        st_s.append(s_s)
    stack = lambda sts, i: jnp.stack([s[i] for s in sts])
    return (h[:n_p].reshape(bp, tp, D_MODEL), h[n_p:].reshape(bs, ts, D_MODEL),
            stack(st_p, 0), stack(st_p, 1), stack(st_p, 2), stack(st_p, 3), stack(st_p, 4),
            stack(st_s, 0), stack(st_s, 1), stack(st_s, 2), stack(st_s, 3), stack(st_s, 4))
```

```python
import functools
import math

import jax
import jax.numpy as jnp
import numpy as np
from jax import lax
from jax.experimental import pallas as pl
from jax.experimental.pallas import tpu as pltpu

f32 = jnp.float32
bf16 = jnp.bfloat16
i32 = jnp.int32

D_MODEL = 1024
PAST_LEN = 2048
CHUNK = 64
A_HEADS = 6
A_HEAD_DIM = 64
A_BAND_CHUNKS = 8
A_PAST = A_BAND_CHUNKS * CHUNK
A_BAND = A_PAST + CHUNK
A_REL_CLIP = 128
B_HEADS = 4
B_KEY_DIM = 64
B_VAL_DIM = 96
B_GATE_RANK = 16
B_GATE_TAU = 16.0
C_HEADS = 4
C_Q_RANK = 192
C_KV_RANK = 128
C_NOPE_DIM = 64
C_ROPE_DIM = 32
C_V_DIM = 64
ROPE_THETA = 10000.0
N_EXPERTS = 32
TOP_K = 4
D_EXPERT = 1024
SWIGLU_LIMIT = 7.0
SWIGLU_ALPHA = 1.702
EPS = 1e-6
NEG = -1e30
LOG2E = 1.0 / math.log(2.0)

LANES = 128
ROW_TILES = D_MODEL // LANES
C_PAD_DIM = LANES
V_EXT = LANES
A_QSCALE = A_HEAD_DIM ** -0.5 * LOG2E
C_QSCALE = (C_NOPE_DIM + C_ROPE_DIM) ** -0.5 * LOG2E

A_W = A_HEADS * A_HEAD_DIM
B_W = B_HEADS * B_VAL_DIM
C_W = C_HEADS * C_V_DIM
B_KW = B_HEADS * B_KEY_DIM
C_QW = C_HEADS * C_PAD_DIM
A_VW = A_HEADS * V_EXT
C_VW = C_HEADS * V_EXT

P_AQ, P_AK, P_AV = 0, 384, 768
P_BQ, P_BK, P_BV, P_BG = 1152, 1408, 1664, 2048
P_CQ, P_CKV, P_CKR, P_BLR = 2432, 2688, 2816, 2912
IN_P = 2944
S_BQ, S_BK, S_BV, S_LA, S_BG, S_W = 0, 256, 512, 896, 1152, 1536

ROW_TILE = 512
BAND_ROWS = 256
GLA_CHUNKS = 4
MLA_TILE = 512
MOE_BLOCK = 256
COMBINE_TILE = 128
ROW_TOK_CHUNK = 4096
VMEM_LIMIT = 56 * 1024 * 1024


def _cparams(*sem):
    return pltpu.CompilerParams(dimension_semantics=sem, vmem_limit_bytes=VMEM_LIMIT)


def _rms(x, g):
    return x * lax.rsqrt(jnp.mean(x * x, axis=-1, keepdims=True) + EPS) * g


def _rope_lanes(x, cos_t, sin_t):
    w = x.shape[1]
    lane = lax.broadcasted_iota(i32, x.shape, 1) % LANES
    first_half = jnp.logical_and(lane >= C_NOPE_DIM, lane < C_NOPE_DIM + C_ROPE_DIM // 2)
    swapped = jnp.where(first_half, pltpu.roll(x, w - C_ROPE_DIM // 2, 1), pltpu.roll(x, C_ROPE_DIM // 2, 1))
    return x * cos_t + swapped * sin_t


def _dot(a, b):
    return jnp.dot(a, b, preferred_element_type=f32)


def _dot_nt(a, b):
    return lax.dot_general(a, b, (((1,), (1,)), ((), ())), preferred_element_type=f32)


def _dot_tn(a, b):
    return lax.dot_general(a, b, (((0,), (0,)), ((), ())), preferred_element_type=f32)


def _sigmoid(x):
    return 1.0 / (1.0 + jnp.exp(-x))


def _ones_column(rows):
    lane = lax.broadcasted_iota(i32, (rows, V_EXT - C_V_DIM), 1)
    return jnp.where(lane == 0, 1.0, 0.0).astype(bf16)


def _group_mean_sq(x, g):
    sq = x * x
    hi = sq.astype(bf16)
    lo = (sq - hi.astype(f32)).astype(bf16)
    return _dot(hi, g) + _dot(lo, g)


def _group_rms(x, g, gain):
    return x * lax.rsqrt(_group_mean_sq(x, g) + EPS) * gain


def _softmax_pv(s, v_blocks, widths):
    p = jnp.exp2(s - jnp.max(s, axis=-1, keepdims=True)).astype(bf16)
    acc, lo = None, 0
    for v, w in zip(v_blocks, widths):
        part = _dot(p[:, lo:lo + w], v)
        acc = part if acc is None else acc + part
        lo += w
    return acc[:, :C_V_DIM] / acc[:, C_V_DIM:C_V_DIM + 1]


def _proj_body(x_ref, g1_ref, w_ref, ga_ref, aqg_ref, akg_ref, w2_ref, gb_ref, qag_ref, wuq_ref,
               gc_ref, cqg_ref, kvag_ref, krg_ref, cos_ref, sin_ref,
               aq_o, ak_o, av_o, akb_o, avb_o, b_o, cq_o, ckv_o, ckr_o, h_sc):
    tm = x_ref.shape[0]
    xn = _rms(x_ref[...], g1_ref[...])
    h_sc[...] = _dot(xn.astype(bf16), w_ref[...])
    cos_t = cos_ref[...]
    sin_t = sin_ref[...]
    q = _group_rms(h_sc[:, P_AQ:P_AQ + A_W], ga_ref[...], aqg_ref[...])
    aq_o[...] = (q * A_QSCALE).astype(bf16)
    k = _group_rms(h_sc[:, P_AK:P_AK + A_W], ga_ref[...], akg_ref[...])
    ak_o[...] = k
    akb_o[...] = k.astype(bf16)
    av_o[...] = h_sc[:, P_AV:P_AV + A_W]
    ones = _ones_column(tm)
    for hd in range(A_HEADS):
        lo = P_AV + hd * A_HEAD_DIM
        avb_o[:, hd * V_EXT:hd * V_EXT + A_HEAD_DIM] = h_sc[:, lo:lo + A_HEAD_DIM].astype(bf16)
        avb_o[:, hd * V_EXT + A_HEAD_DIM:(hd + 1) * V_EXT] = ones
    b_o[:, S_BQ:S_LA] = h_sc[:, P_BQ:P_BG]
    z = jnp.dot(h_sc[:, P_BLR:P_BLR + B_GATE_RANK], w2_ref[...], preferred_element_type=f32,
                precision=lax.Precision.HIGHEST) + gb_ref[...]
    log_sig = jnp.minimum(z, 0.0) - jnp.log1p(jnp.exp(-jnp.abs(z)))
    b_o[:, S_LA:S_BG] = log_sig * (1.0 / B_GATE_TAU)
    b_o[:, S_BG:S_W] = h_sc[:, P_BG:P_BG + B_W]
    cql = _rms(h_sc[:, P_CQ:P_CQ + C_Q_RANK], qag_ref[...])
    cq = _group_rms(_dot(cql.astype(bf16), wuq_ref[...]), gc_ref[...], cqg_ref[...])
    cos4 = jnp.concatenate([cos_t] * C_HEADS, axis=1)
    sin4 = jnp.concatenate([sin_t] * C_HEADS, axis=1)
    cq_o[...] = (_rope_lanes(cq, cos4, sin4) * C_QSCALE).astype(bf16)
    ckv_o[...] = _rms(h_sc[:, P_CKV:P_CKV + C_KV_RANK], kvag_ref[...])
    kr = _group_rms(h_sc[:, P_CKR:P_CKR + LANES], gc_ref[0:LANES, 0:LANES], krg_ref[...])
    ckr_o[...] = _rope_lanes(kr, cos_t, sin_t)[:, C_NOPE_DIM:C_NOPE_DIM + C_ROPE_DIM]


def _proj(h_all, row0, n, lw, cos, sin):
    tm = ROW_TILE
    blk0 = row0 // tm
    npos = cos.shape[0] // tm
    row = lambda w: pl.BlockSpec((tm, w), lambda i: (i, 0))
    const = lambda a: pl.BlockSpec(a.shape, lambda i: (0,) * a.ndim)
    pos = pl.BlockSpec((tm, LANES), lambda i: (i % npos, 0))
    consts = (lw['ln1_g'], lw['w_in'], lw['g_a'], lw['a_q_g'], lw['a_k_g'], lw['b_gate_w2'], lw['b_gate_b'],
              lw['c_qa_g'], lw['c_w_uq'], lw['g_c'], lw['c_q_gain'], lw['c_kva_g'], lw['c_kr_gain'])
    out_w = ((A_W, bf16), (A_W, f32), (A_W, f32), (A_W, bf16), (A_VW, bf16), (S_W, f32),
             (C_QW, bf16), (C_KV_RANK, f32), (C_ROPE_DIM, f32))
    return pl.pallas_call(
        _proj_body,
        grid=(n // tm,),
        in_specs=[pl.BlockSpec((tm, D_MODEL), lambda i: (i + blk0, 0))] + [const(a) for a in consts] + [pos, pos],
        out_specs=[row(w) for w, _ in out_w],
        out_shape=[jax.ShapeDtypeStruct((n, w), dt) for w, dt in out_w],
        scratch_shapes=[pltpu.VMEM((tm, IN_P), f32)],
        compiler_params=_cparams("parallel"),
        name="proj",
    )(h_all, *consts, cos, sin)


def _expand_keys(kv, kr, g_k, k_gain, place, k_dst, v_dst, rows):
    kn = _group_rms(kv[:, :C_HEADS * C_NOPE_DIM], g_k, k_gain).astype(bf16)
    k_dst[rows, :] = _dot(jnp.concatenate([kn, kr], axis=1), place).astype(bf16)
    v = kv[:, C_HEADS * C_NOPE_DIM:]
    lane = lax.broadcasted_iota(i32, v.shape, 1) % LANES
    v_dst[rows, :] = jnp.where(lane == C_V_DIM, 1.0, v).astype(bf16)


def _mla_keys_body(ckv_ref, ckr_ref, w_ref, gk_ref, kg_ref, pl_ref, ck_o, cv_o):
    kv = _dot(ckv_ref[...].astype(bf16), w_ref[...])
    _expand_keys(kv, ckr_ref[...].astype(bf16), gk_ref[...], kg_ref[...], pl_ref[...], ck_o, cv_o, slice(None))


def _mla_keys(ckv, ckr, lw):
    n = ckv.shape[0]
    tm = ROW_TILE
    row = lambda w: pl.BlockSpec((tm, w), lambda i: (i, 0))
    const = lambda a: pl.BlockSpec(a.shape, lambda i: (0,) * a.ndim)
    consts = (lw['c_w_ukv'], lw['g_k'], lw['c_k_gain'], lw['k_place'])
    return pl.pallas_call(
        _mla_keys_body,
        grid=(n // tm,),
        in_specs=[row(C_KV_RANK), row(C_ROPE_DIM)] + [const(a) for a in consts],
        out_specs=[row(C_QW), row(C_VW)],
        out_shape=[jax.ShapeDtypeStruct((n, C_QW), bf16), jax.ShapeDtypeStruct((n, C_VW), bf16)],
        compiler_params=_cparams("parallel"),
        name="mla_keys",
    )(ckv, ckr, *consts)


def _band_body(q_ref, *refs, n_kb, past_from_seq):
    k_refs, v_refs = refs[:n_kb], refs[n_kb:2 * n_kb]
    bias_ref, o_ref = refs[2 * n_kb], refs[2 * n_kb + 1]
    rows = q_ref.shape[0]
    widths = [k.shape[0] for k in k_refs]
    if past_from_seq:
        col = lax.broadcasted_iota(i32, (rows, sum(widths)), 1)
        exists = col + (pl.program_id(1) * rows - A_PAST) >= 0
    for hd in range(A_HEADS):
        qs = slice(hd * A_HEAD_DIM, (hd + 1) * A_HEAD_DIM)
        vs = slice(hd * V_EXT, (hd + 1) * V_EXT)
        q = q_ref[:, qs]
        s = jnp.concatenate([_dot_nt(q, k[:, qs]) for k in k_refs], axis=1) + bias_ref[hd]
        if past_from_seq:
            s = jnp.where(exists, s, NEG)
        o_ref[:, qs] = _softmax_pv(s, [v[:, vs] for v in v_refs], widths).astype(bf16)


def _band(q, k, v, k_past, v_past, bias):
    b_, t, _ = q.shape
    from_seq = k_past is None
    rows = BAND_ROWS if from_seq else t
    n_back = A_PAST // rows if from_seq else 1
    cur = lambda w: pl.BlockSpec((None, rows, w), lambda b, i: (b, i, 0))
    if from_seq:
        back = lambda w, d: pl.BlockSpec((None, rows, w), lambda b, i: (b, jnp.maximum(i - d, 0), 0))
        k_specs = [back(A_W, d) for d in range(n_back, 0, -1)] + [cur(A_W)]
        v_specs = [back(A_VW, d) for d in range(n_back, 0, -1)] + [cur(A_VW)]
        k_args, v_args = [k] * (n_back + 1), [v] * (n_back + 1)
    else:
        past = lambda w: pl.BlockSpec((None, A_PAST, w), lambda b, i: (b, 0, 0))
        k_specs, v_specs = [past(A_W), cur(A_W)], [past(A_VW), cur(A_VW)]
        k_args, v_args = [k_past, k], [v_past, v]
    return pl.pallas_call(
        functools.partial(_band_body, n_kb=len(k_specs), past_from_seq=from_seq),
        grid=(b_, t // rows),
        in_specs=[cur(A_W)] + k_specs + v_specs + [pl.BlockSpec(bias.shape, lambda b, i: (0, 0, 0))],
        out_specs=cur(A_W),
        out_shape=jax.ShapeDtypeStruct((b_, t, A_W), bf16),
        compiler_params=_cparams("parallel", "parallel"),
        name="band",
    )(q, *k_args, *v_args, bias)


def _gla_body(b_ref, s0_ref, gain_ref, o_ref, st_ref, *, chunks):
    @pl.when(pl.program_id(1) == 0)
    def _():
        st_ref[...] = s0_ref[...]

    rows = chunks * CHUNK
    r = lax.broadcasted_iota(i32, (rows, rows), 0)
    c = lax.broadcasted_iota(i32, (rows, rows), 1)
    causal = jnp.logical_and(r >= c, r // CHUNK == c // CHUNK)
    tril = (lax.broadcasted_iota(i32, (CHUNK, CHUNK), 0) >= lax.broadcasted_iota(i32, (CHUNK, CHUNK), 1)).astype(f32)
    bcum, btot = [], []
    for ch in range(chunks):
        la = b_ref[ch * CHUNK:(ch + 1) * CHUNK, S_LA:S_LA + B_KW]
        cum = jnp.dot(tril, la, preferred_element_type=f32, precision=lax.Precision.HIGHEST)
        bcum.append(cum)
        btot.append(jnp.broadcast_to(cum[CHUNK - 1:CHUNK, :], cum.shape))
    bcum = jnp.concatenate(bcum, axis=0)
    btot = jnp.concatenate(btot, axis=0)
    k = b_ref[:, S_BK:S_BK + B_KW]
    qt = b_ref[:, S_BQ:S_BQ + B_KW] * (B_KEY_DIM ** -0.5) * jnp.exp(bcum)
    kt = k * jnp.exp(-bcum)
    kd = k * jnp.exp(btot - bcum)
    decay = jnp.exp(btot)
    for hd in range(B_HEADS):
        ks = slice(hd * B_KEY_DIM, (hd + 1) * B_KEY_DIM)
        vs = slice(hd * B_VAL_DIM, (hd + 1) * B_VAL_DIM)
        v = b_ref[:, S_BV + vs.start:S_BV + vs.stop]
        att = jnp.where(causal, _dot_nt(qt[:, ks], kt[:, ks]), 0.0)
        o_intra = _dot(att, v)
        st = st_ref[hd]
        outs = []
        for ch in range(chunks):
            rs = slice(ch * CHUNK, (ch + 1) * CHUNK)
            outs.append(o_intra[rs] + _dot_nt(qt[rs, ks], st))
            st = st * decay[ch * CHUNK:ch * CHUNK + 1, ks] + _dot_tn(v[rs], kd[rs, ks])
        st_ref[hd] = st
        g = b_ref[:, S_BG + vs.start:S_BG + vs.stop]
        o = _rms(jnp.concatenate(outs, axis=0), gain_ref[...]) * (g * _sigmoid(g))
        o_ref[:, vs] = o.astype(bf16)


def _gla(slab, s0_t, gain):
    b_, t, _ = slab.shape
    chunks = min(GLA_CHUNKS, t // CHUNK)
    rows = chunks * CHUNK
    st_spec = pl.BlockSpec((None, B_HEADS, B_VAL_DIM, B_KEY_DIM), lambda b, i: (b, 0, 0, 0))
    return pl.pallas_call(
        functools.partial(_gla_body, chunks=chunks),
        grid=(b_, t // rows),
        in_specs=[pl.BlockSpec((None, rows, S_W), lambda b, i: (b, i, 0)), st_spec,
                  pl.BlockSpec(gain.shape, lambda b, i: (0, 0))],
        out_specs=[pl.BlockSpec((None, rows, B_W), lambda b, i: (b, i, 0)), st_spec],
        out_shape=[jax.ShapeDtypeStruct((b_, t, B_W), bf16),
                   jax.ShapeDtypeStruct((b_, B_HEADS, B_VAL_DIM, B_KEY_DIM), f32)],
        compiler_params=_cparams("parallel", "arbitrary"),
        name="gla",
    )(slab, s0_t, gain)


def _mla_causal_body(q_ref, k_ref, v_ref, o_ref, m_sc, acc_sc):
    i = pl.program_id(1)
    t = q_ref.shape[0]
    m_sc[...] = jnp.full(m_sc.shape, NEG, f32)
    acc_sc[...] = jnp.zeros(acc_sc.shape, f32)

    def tile(j, diagonal):
        rows = pl.ds(pl.multiple_of(j * t, t), t)
        if diagonal:
            visible = (lax.broadcasted_iota(i32, (t, t), 1) // CHUNK) <= (lax.broadcasted_iota(i32, (t, t), 0) // CHUNK)
        for hd in range(C_HEADS):
            hs = slice(hd * C_PAD_DIM, (hd + 1) * C_PAD_DIM)
            s = _dot_nt(q_ref[:, hs], k_ref[rows, hs])
            if diagonal:
                s = jnp.where(visible, s, NEG)
            m_old = m_sc[hd]
            m_new = jnp.maximum(m_old, jnp.max(s, axis=-1, keepdims=True))
            p = jnp.exp2(s - jnp.concatenate([m_new] * (t // LANES), axis=1))
            acc_sc[hd] = jnp.exp2(m_old - m_new) * acc_sc[hd] + _dot(p.astype(bf16), v_ref[rows, hs])
            m_sc[hd] = m_new

    def earlier(j, carry):
        tile(j, False)
        return carry

    lax.fori_loop(0, i, earlier, 0)
    tile(i, True)
    for hd in range(C_HEADS):
        acc = acc_sc[hd]
        o_ref[:, hd * C_V_DIM:(hd + 1) * C_V_DIM] = (acc[:, :C_V_DIM] / acc[:, C_V_DIM:C_V_DIM + 1]).astype(bf16)


def _mla_causal(q, k, v):
    b_, t, _ = q.shape
    tq = MLA_TILE
    whole = lambda w: pl.BlockSpec((None, t, w), lambda b, i: (b, 0, 0))
    return pl.pallas_call(
        _mla_causal_body,
        grid=(b_, t // tq),
        in_specs=[pl.BlockSpec((None, tq, C_QW), lambda b, i: (b, i, 0)), whole(C_QW), whole(C_VW)],
        out_specs=pl.BlockSpec((None, tq, C_W), lambda b, i: (b, i, 0)),
        out_shape=jax.ShapeDtypeStruct((b_, t, C_W), bf16),
        scratch_shapes=[pltpu.VMEM((C_HEADS, tq, LANES), f32), pltpu.VMEM((C_HEADS, tq, V_EXT), f32)],
        compiler_params=_cparams("parallel", "arbitrary"),
        name="mla_causal",
    )(q, k, v)


def _mla_sample_body(q_ref, ckv_ref, ckr_ref, kn_ref, vn_ref, w_ref, gk_ref, kg_ref, pl_ref, o_ref, k_sc, v_sc):
    past = ckv_ref.shape[0]
    kv = _dot(ckv_ref[...].astype(bf16), w_ref[...])
    _expand_keys(kv, ckr_ref[...].astype(bf16), gk_ref[...], kg_ref[...], pl_ref[...], k_sc, v_sc, slice(0, past))
    k_sc[past:, :] = kn_ref[...]
    v_sc[past:, :] = vn_ref[...]
    total = k_sc.shape[0]
    for hd in range(C_HEADS):
        hs = slice(hd * C_PAD_DIM, (hd + 1) * C_PAD_DIM)
        s = _dot_nt(q_ref[:, hs], k_sc[:, hs])
        o_ref[:, hd * C_V_DIM:(hd + 1) * C_V_DIM] = _softmax_pv(s, [v_sc[:, hs]], [total]).astype(bf16)


def _mla_sample(q, c_kv, c_kr, layer, k_new, v_new, lw):
    b_, s_len, _ = q.shape
    past = c_kv.shape[2]
    per_b = lambda r, w: pl.BlockSpec((None, r, w), lambda b: (b, 0, 0))
    cache = lambda w: pl.BlockSpec((None, None, past, w), lambda b: (layer, b, 0, 0))
    const = lambda a: pl.BlockSpec(a.shape, lambda b: (0,) * a.ndim)
    consts = (lw['c_w_ukv'], lw['g_k'], lw['c_k_gain'], lw['k_place'])
    return pl.pallas_call(
        _mla_sample_body,
        grid=(b_,),
        in_specs=[per_b(s_len, C_QW), cache(C_KV_RANK), cache(C_ROPE_DIM), per_b(s_len, C_QW),
                  per_b(s_len, C_VW)] + [const(a) for a in consts],
        out_specs=per_b(s_len, C_W),
        out_shape=jax.ShapeDtypeStruct((b_, s_len, C_W), bf16),
        scratch_shapes=[pltpu.VMEM((past + s_len, C_QW), bf16), pltpu.VMEM((past + s_len, C_VW), bf16)],
        compiler_params=_cparams("parallel"),
        name="mla_sample",
    )(q, c_kv, c_kr, k_new, v_new, *consts)


def _outproj_body(ap_ref, bp_ref, cp_ref, hp_ref, as_ref, bs_ref, cs_ref, hs_ref, w_ref, g2_ref, rwh_ref, rwl_ref,
                  rb_ref, h_o, xn_o, lg_o, *, prompt_tiles):
    def run(a_ref, b_ref, c_ref, h_ref):
        y = _dot(a_ref[...], w_ref[0:A_W, :])
        y += _dot(b_ref[...], w_ref[A_W:A_W + B_W, :])
        y += _dot(c_ref[...], w_ref[A_W + B_W:, :])
        h = h_ref[...] + y
        h_o[...] = h
        xn = _rms(h, g2_ref[...])
        for c in range(ROW_TILES):
            xn_o[pl.ds(c, h.shape[0], stride=ROW_TILES), :] = xn[:, c * LANES:(c + 1) * LANES]
        xh = xn.astype(bf16)
        xl = (xn - xh.astype(f32)).astype(bf16)
        lg_o[...] = _dot(xh, rwh_ref[...]) + _dot(xl, rwh_ref[...]) + _dot(xh, rwl_ref[...]) + rb_ref[...]

    i = pl.program_id(0)
    pl.when(i < prompt_tiles)(functools.partial(run, ap_ref, bp_ref, cp_ref, hp_ref))
    pl.when(i >= prompt_tiles)(functools.partial(run, as_ref, bs_ref, cs_ref, hs_ref))


def _outproj(mix_p, mix_s, res_p, res_s, lw):
    tm = ROW_TILE
    pt = mix_p[0].shape[0] // tm
    n = mix_p[0].shape[0] + mix_s[0].shape[0]
    row = lambda w: pl.BlockSpec((tm, w), lambda i: (i, 0))
    first = lambda w, o=0: pl.BlockSpec((tm, w), lambda i: (jnp.minimum(i, pt - 1) + o, 0))
    second = lambda w, o=0: pl.BlockSpec((tm, w), lambda i: (jnp.maximum(i - pt, 0) + o, 0))
    const = lambda a: pl.BlockSpec(a.shape, lambda i: (0,) * a.ndim)
    consts = (lw['w_out'], lw['ln2_g'], lw['router_w_hi'], lw['router_w_lo'], lw['router_b'])
    return pl.pallas_call(
        functools.partial(_outproj_body, prompt_tiles=pt),
        grid=(n // tm,),
        in_specs=[first(A_W), first(B_W), first(C_W), first(D_MODEL, res_p[1] // tm),
                  second(A_W), second(B_W), second(C_W), second(D_MODEL, res_s[1] // tm)]
        + [const(a) for a in consts],
        out_specs=[row(D_MODEL), pl.BlockSpec((tm * ROW_TILES, LANES), lambda i: (i, 0)), row(N_EXPERTS)],
        out_shape=[jax.ShapeDtypeStruct((n, D_MODEL), f32), jax.ShapeDtypeStruct((n * ROW_TILES, LANES), f32),
                   jax.ShapeDtypeStruct((n, N_EXPERTS), f32)],
        compiler_params=_cparams("parallel"),
        name="outproj",
    )(*mix_p, res_p[0], *mix_s, res_s[0], *consts)


def _route_body(lg_ref, e_o, g_o, r_o, cnt_o):
    @pl.when(pl.program_id(0) == 0)
    def _():
        cnt_o[...] = jnp.zeros(cnt_o.shape, f32)

    tm = lg_ref.shape[0]
    lane = lax.broadcasted_iota(i32, (tm, N_EXPERTS), 1)
    slot = lax.broadcasted_iota(i32, (tm, TOP_K), 1)
    work = lg_ref[...]
    onehots, vals = [], []
    e_out = jnp.zeros((tm, TOP_K), i32)
    for k in range(TOP_K):
        m = jnp.max(work, axis=-1, keepdims=True)
        idx = jnp.min(jnp.where(work == m, lane, N_EXPERTS), axis=-1, keepdims=True)
        oh = lane == idx
        work = jnp.where(oh, -jnp.inf, work)
        onehots.append(oh)
        vals.append(m)
        e_out = jnp.where(slot == k, idx, e_out)
    ex = [jnp.exp(v - vals[0]) for v in vals]
    denom = ex[0] + ex[1] + ex[2] + ex[3]
    g_out = jnp.zeros((tm, TOP_K), f32)
    sel = jnp.zeros((tm, N_EXPERTS), f32)
    for k in range(TOP_K):
        g_out = jnp.where(slot == k, ex[k] / denom, g_out)
        sel = sel + onehots[k].astype(f32)
    r = lax.broadcasted_iota(i32, (tm, tm), 0)
    c = lax.broadcasted_iota(i32, (tm, tm), 1)
    before = (c < r).astype(bf16)
    rank = _dot(before, sel.astype(bf16)) + cnt_o[...]
    r_out = jnp.zeros((tm, TOP_K), f32)
    for k in range(TOP_K):
        rk = jnp.sum(jnp.where(onehots[k], rank, 0.0), axis=-1, keepdims=True)
        r_out = jnp.where(slot == k, rk, r_out)
    e_o[...] = e_out
    g_o[...] = g_out
    r_o[...] = r_out.astype(i32)
    cnt_o[...] += jnp.sum(sel, axis=0, keepdims=True)


def _route(logits):
    n = logits.shape[0]
    tm = ROW_TILE
    row = lambda w: pl.BlockSpec((tm, w), lambda i: (i, 0))
    return pl.pallas_call(
        _route_body,
        grid=(n // tm,),
        in_specs=[row(N_EXPERTS)],
        out_specs=[row(TOP_K), row(TOP_K), row(TOP_K), pl.BlockSpec((1, N_EXPERTS), lambda i: (0, 0))],
        out_shape=[jax.ShapeDtypeStruct((n, TOP_K), i32), jax.ShapeDtypeStruct((n, TOP_K), f32),
                   jax.ShapeDtypeStruct((n, TOP_K), i32), jax.ShapeDtypeStruct((1, N_EXPERTS), f32)],
        compiler_params=_cparams("arbitrary"),
        name="route",
    )(logits)


def _row_copy(src_hbm, tok, dst, slot, j, sem):
    return pltpu.make_async_copy(src_hbm.at[pl.ds(pl.multiple_of(tok * ROW_TILES, ROW_TILES), ROW_TILES), :],
                                 dst.at[slot, pl.ds(j * ROW_TILES, ROW_TILES), :], sem.at[slot])


def _experts_body(blk_e_ref, row_tok_ref, used_ref, x_hbm, wg_ref, bg_ref, wu_ref, bu_ref, wd_ref, bd_ref, y_o,
                  xbuf, xb, wbf, sem):
    b = pl.program_id(0)
    bm = MOE_BLOCK
    last = used_ref[0] - 1
    slot = b % 2

    def wait_block(s):
        pltpu.make_async_copy(x_hbm.at[pl.ds(0, bm * ROW_TILES), :], xbuf.at[s], sem.at[s]).wait()

    @pl.when(b == 0)
    def _():
        def issue(j, carry):
            _row_copy(x_hbm, row_tok_ref[j], xbuf, 0, j, sem).start()
            return carry
        lax.fori_loop(0, bm, issue, 0)

    @pl.when(b <= last)
    def _():
        wait_block(slot)
        xb[...] = jnp.concatenate([xbuf[slot, pl.ds(c, bm, stride=ROW_TILES), :] for c in range(ROW_TILES)],
                                  axis=1).astype(bf16)
        nxt = jnp.minimum(b + 1, last) * bm
        for s in range(2):
            @pl.when(slot != s)
            def _(s=s):
                for j in range(bm):
                    _row_copy(x_hbm, row_tok_ref[nxt + j], xbuf, s, j, sem).start(priority=j % 2)

        @pl.when(jnp.logical_or(b == 0, blk_e_ref[b] != blk_e_ref[jnp.maximum(b - 1, 0)]))
        def _():
            wbf[0] = wg_ref[...].astype(bf16)
            wbf[1] = wu_ref[...].astype(bf16)
            wbf[2] = wd_ref[...].astype(bf16)

        x = xb[...]
        g = _dot(x, wbf[0]) + bg_ref[...]
        u = _dot(x, wbf[1]) + bu_ref[...]
        g = jnp.minimum(g, SWIGLU_LIMIT)
        u = jnp.clip(u, -SWIGLU_LIMIT, SWIGLU_LIMIT)
        hdn = (u + 1.0) * (g * _sigmoid(g * SWIGLU_ALPHA))
        y_o[...] = _dot(hdn.astype(bf16), wbf[2]) + bd_ref[...]

    @pl.when(b == last)
    def _():
        wait_block(1 - slot)

    @pl.when(b > last)
    def _():
        y_o[...] = jnp.zeros(y_o.shape, f32)


def _experts(x, blk_e, row_tok, n_used, layer, ew):
    nblk = blk_e.shape[0]
    bm = MOE_BLOCK
    wspec = lambda: pl.BlockSpec((None, None, D_MODEL, D_EXPERT), lambda b, be, rt, nu: (layer, be[b], 0, 0))
    bspec = lambda: pl.BlockSpec((None, None, 1, D_EXPERT), lambda b, be, rt, nu: (layer, be[b], 0, 0))
    grid_spec = pltpu.PrefetchScalarGridSpec(
        num_scalar_prefetch=3,
        grid=(nblk,),
        in_specs=[pl.BlockSpec(memory_space=pl.ANY), wspec(), bspec(), wspec(), bspec(), wspec(), bspec()],
        out_specs=pl.BlockSpec((bm, D_MODEL), lambda b, be, rt, nu: (b, 0)),
        scratch_shapes=[pltpu.VMEM((2, bm * ROW_TILES, LANES), f32), pltpu.VMEM((bm, D_MODEL), bf16),
                        pltpu.VMEM((3, D_MODEL, D_EXPERT), bf16), pltpu.SemaphoreType.DMA((2,))],
    )
    return pl.pallas_call(
        _experts_body,
        grid_spec=grid_spec,
        out_shape=jax.ShapeDtypeStruct((nblk * bm, D_MODEL), f32),
        compiler_params=_cparams("arbitrary"),
        name="experts",
    )(blk_e, row_tok, n_used, x, ew['e_w_gate'], ew['e_b_gate'], ew['e_w_up'], ew['e_b_up'],
      ew['e_w_down'], ew['e_b_down'])


def _combine_body(dest_ref, y_hbm, h_ref, g_ref, *refs, first_tiles):
    o_refs, (ybuf, sem) = refs[:-2], refs[-2:]
    i = pl.program_id(0)
    nt = pl.num_programs(0)
    tt = COMBINE_TILE

    def row_copy(tile, slot, j, k):
        return pltpu.make_async_copy(y_hbm.at[pl.ds(dest_ref[(tile * tt + j) * TOP_K + k], 1), :],
                                     ybuf.at[slot, k, pl.ds(j, 1), :], sem.at[slot])

    def wait_tile(slot):
        for k in range(TOP_K):
            pltpu.make_async_copy(y_hbm.at[pl.ds(0, tt), :], ybuf.at[slot, k], sem.at[slot]).wait()

    slot = i % 2

    @pl.when(i == 0)
    def _():
        def issue(j, carry):
            for k in range(TOP_K):
                row_copy(0, 0, j, k).start(priority=k % 2)
            return carry
        lax.fori_loop(0, tt, issue, 0)

    for s in range(2):
        @pl.when(jnp.logical_and(i + 1 < nt, slot != s))
        def _(s=s):
            for j in range(tt):
                for k in range(TOP_K):
                    row_copy(i + 1, s, j, k).start(priority=k % 2)

    wait_tile(slot)
    g = g_ref[...]
    acc = h_ref[...]
    for k in range(TOP_K):
        acc = acc + ybuf[slot, k] * g[:, k:k + 1]
    if len(o_refs) == 1:
        o_refs[0][...] = acc
    else:
        @pl.when(i < first_tiles)
        def _():
            o_refs[0][...] = acc

        @pl.when(i >= first_tiles)
        def _():
            o_refs[1][...] = acc


def _combine(y_rows, dest, h, gates, split_rows=None):
    n = h.shape[0]
    tt = COMBINE_TILE
    blk = lambda f: pl.BlockSpec((tt, D_MODEL), f)
    if split_rows is None:
        ft = n // tt
        out_specs = blk(lambda i, d: (i, 0))
        out_shape = jax.ShapeDtypeStruct((n, D_MODEL), f32)
    else:
        ft = split_rows // tt
        out_specs = [blk(lambda i, d: (jnp.minimum(i, ft - 1), 0)), blk(lambda i, d: (jnp.maximum(i - ft, 0), 0))]
        out_shape = [jax.ShapeDtypeStruct((split_rows, D_MODEL), f32), jax.ShapeDtypeStruct((n - split_rows, D_MODEL), f32)]
    grid_spec = pltpu.PrefetchScalarGridSpec(
        num_scalar_prefetch=1,
        grid=(n // tt,),
        in_specs=[pl.BlockSpec(memory_space=pl.ANY), blk(lambda i, d: (i, 0)),
                  pl.BlockSpec((tt, TOP_K), lambda i, d: (i, 0))],
        out_specs=out_specs,
        scratch_shapes=[pltpu.VMEM((2, TOP_K, tt, D_MODEL), f32), pltpu.SemaphoreType.DMA((2,))],
    )
    return pl.pallas_call(
        functools.partial(_combine_body, first_tiles=ft),
        grid_spec=grid_spec,
        out_shape=out_shape,
        compiler_params=_cparams("arbitrary"),
        name="combine",
    )(dest, y_rows, h, gates)


def _row_tok_body(dest_hbm, zeros_hbm, out_ref, buf0, buf1, sem, zsem):
    ch = buf0.shape[0]
    nchunks = dest_hbm.shape[0] // ch
    bufs = (buf0, buf1)
    fill = pltpu.make_async_copy(zeros_hbm, out_ref, zsem)
    fill.start()
    fill.wait()

    def copy(c, slot):
        return pltpu.make_async_copy(dest_hbm.at[pl.ds(pl.multiple_of(c * ch, ch), ch)], bufs[slot], sem.at[slot])

    copy(0, 0).start()

    def per_pair(p, carry):
        for slot in range(2):
            c = 2 * p + slot

            @pl.when(c + 1 < nchunks)
            def _():
                copy(c + 1, 1 - slot).start()

            copy(c, slot).wait()
            base = c * (ch // TOP_K)

            def scatter(t, cc):
                for k in range(TOP_K):
                    out_ref[bufs[slot][TOP_K * t + k]] = base + t
                return cc

            lax.fori_loop(0, ch // TOP_K, scatter, 0, unroll=8)
        return carry

    lax.fori_loop(0, nchunks // 2, per_pair, 0)


def _row_tok(dest, rows):
    ch = min(ROW_TOK_CHUNK, dest.shape[0] // 2)
    assert dest.shape[0] % (2 * ch) == 0
    return pl.pallas_call(
        _row_tok_body,
        in_specs=[pl.BlockSpec(memory_space=pl.ANY), pl.BlockSpec(memory_space=pl.ANY)],
        out_specs=pl.BlockSpec(memory_space=pltpu.SMEM),
        out_shape=jax.ShapeDtypeStruct((rows,), i32),
        scratch_shapes=[pltpu.SMEM((ch,), i32), pltpu.SMEM((ch,), i32), pltpu.SemaphoreType.DMA((2,)),
                        pltpu.SemaphoreType.DMA(())],
        name="row_tok",
    )(dest, jnp.zeros((rows,), i32))


def _moe(h, xn, logits, layer, ew, split_rows=None):
    n = h.shape[0]
    bm = MOE_BLOCK
    top_e, gates, rank, counts = _route(logits)
    counts = counts[0].astype(i32)
    padded = ((counts + bm - 1) // bm) * bm
    pend = jnp.cumsum(padded)
    pstart = pend - padded
    expert = jnp.arange(N_EXPERTS, dtype=i32)
    dest = jnp.sum(jnp.where(top_e[:, :, None] == expert, pstart, 0), axis=-1) + rank
    nblk = (n * TOP_K) // bm + N_EXPERTS
    row_tok = _row_tok(dest.reshape(-1), nblk * bm)
    blk_start = jnp.arange(nblk, dtype=i32) * bm
    blk_e = jnp.minimum(jnp.sum((pend[None, :] <= blk_start[:, None]).astype(i32), axis=1), N_EXPERTS - 1)
    y_rows = _experts(xn, blk_e, row_tok, pend[-1:] // bm, layer, ew)
    return _combine(y_rows, dest.reshape(-1), h, gates, split_rows)


def _rope_tables(pos, tile_rows):
    half = C_ROPE_DIM // 2
    inv = ROPE_THETA ** (-jnp.arange(half, dtype=f32) / half)
    ang = pos.astype(f32)[:, None] * inv[None, :]
    cos, sin = jnp.cos(ang), jnp.sin(ang)
    n = ang.shape[0]
    tail = jnp.zeros((n, LANES - C_NOPE_DIM - C_ROPE_DIM), f32)
    cos_t = jnp.concatenate([jnp.ones((n, C_NOPE_DIM), f32), cos, cos, tail], axis=1)
    sin_t = jnp.concatenate([jnp.zeros((n, C_NOPE_DIM), f32), -sin, sin, tail], axis=1)
    reps = max(1, tile_rows // n)
    return jnp.tile(cos_t, (reps, 1)), jnp.tile(sin_t, (reps, 1))


def _band_bias(rel_bias, rows):
    cols = A_PAST + rows
    period = cols + rows
    x = jnp.arange(period)
    x = jnp.where(x < cols, x, x - period)
    idx = jnp.clip(A_PAST - x, -A_REL_CLIP, A_REL_CLIP) + A_REL_CLIP
    v = rel_bias[:, idx].astype(f32) * LOG2E
    skew = jnp.tile(v, (1, rows))[:, :rows * (period - 1)].reshape(-1, rows, period - 1)
    r = jnp.arange(rows)[:, None]
    c = jnp.arange(cols)[None, :]
    lo = (r // CHUNK) * CHUNK
    in_band = jnp.logical_and(c >= lo, c < lo + A_BAND)
    return jnp.where(in_band[None], skew[:, :, :cols], NEG)


def _group_matrix(width, groups):
    g = np.zeros((width, width), np.float32)
    for start, size in groups:
        g[start:start + size, start:start + size] = 1.0 / size
    return jnp.asarray(g, bf16)


def _layer_weights(l, p):
    w = p['w_in'][l]
    seg = lambda o, n: w[:, o:o + n]
    zc = lambda n: jnp.zeros((D_MODEL, n), f32)
    o_aq, o_ak, o_av, o_bq, o_bk, o_bv = 0, 384, 768, 1152, 1408, 1664
    o_blr, o_bg, o_cq, o_ckv, o_ckr = 2048, 2064, 2448, 2640, 2768
    w_in = jnp.concatenate([
        seg(o_aq, 384), seg(o_ak, 384), seg(o_av, 384), seg(o_bq, 256), seg(o_bk, 256), seg(o_bv, 384),
        seg(o_bg, 384), seg(o_cq, C_Q_RANK), zc(P_CKV - P_CQ - C_Q_RANK), seg(o_ckv, 128),
        zc(C_NOPE_DIM), seg(o_ckr, 32), seg(o_blr, 16), zc(IN_P - P_BLR - B_GATE_RANK)], axis=1).astype(bf16)
    hq = C_NOPE_DIM + C_ROPE_DIM
    pad_heads = lambda a, n: jnp.pad(a.reshape(a.shape[0], C_HEADS, n), ((0, 0), (0, 0), (0, C_PAD_DIM - n))
                                     ).reshape(a.shape[0], C_QW)
    wuq = pad_heads(p['c_w_uq'][l], hq).astype(bf16)
    wukv = p['c_w_ukv'][l].reshape(C_KV_RANK, C_HEADS, C_NOPE_DIM + C_V_DIM)
    wukv = jnp.concatenate([wukv[:, :, :C_NOPE_DIM].reshape(C_KV_RANK, -1),
                            pad_heads(wukv[:, :, C_NOPE_DIM:].reshape(C_KV_RANK, -1), C_V_DIM)], axis=1).astype(bf16)
    r2 = lambda a: a.reshape(1, -1)
    z = lambda n: jnp.zeros((n,), f32)
    tail = C_PAD_DIM - hq
    head_groups = [(hd * C_PAD_DIM + o, n) for hd in range(C_HEADS) for o, n in ((0, C_NOPE_DIM), (C_NOPE_DIM, C_ROPE_DIM))]
    place = np.zeros((C_HEADS * C_NOPE_DIM + C_ROPE_DIM, C_QW), np.float32)
    for hd in range(C_HEADS):
        place[hd * C_NOPE_DIM + np.arange(C_NOPE_DIM), hd * C_PAD_DIM + np.arange(C_NOPE_DIM)] = 1.0
        place[C_HEADS * C_NOPE_DIM + np.arange(C_ROPE_DIM), hd * C_PAD_DIM + C_NOPE_DIM + np.arange(C_ROPE_DIM)] = 1.0
    rw = p['router_w'][l]
    rw_hi = rw.astype(bf16)
    return {
        'ln1_g': r2(p['ln1_g'][l]), 'ln2_g': r2(p['ln2_g'][l]), 'w_in': w_in,
        'g_a': _group_matrix(A_W, [(hd * A_HEAD_DIM, A_HEAD_DIM) for hd in range(A_HEADS)]),
        'a_q_g': r2(jnp.tile(p['a_q_g'][l], A_HEADS)), 'a_k_g': r2(jnp.tile(p['a_k_g'][l], A_HEADS)),
        'bias_p': _band_bias(p['a_rel_bias'][l], BAND_ROWS), 'bias_s': _band_bias(p['a_rel_bias'][l], CHUNK),
        'b_gate_w2': p['b_gate_w2'][l], 'b_gate_b': r2(p['b_gate_b'][l]), 'b_out_g': r2(p['b_out_g'][l]),
        'c_qa_g': r2(p['c_qa_g'][l]), 'c_w_uq': wuq, 'c_kva_g': r2(p['c_kva_g'][l]), 'c_w_ukv': wukv,
        'g_c': _group_matrix(C_QW, head_groups),
        'c_q_gain': r2(jnp.tile(jnp.concatenate([p['c_qn_g'][l], p['c_qr_g'][l], z(tail)]), C_HEADS)),
        'c_k_gain': r2(jnp.tile(p['c_kn_g'][l], C_HEADS)),
        'g_k': _group_matrix(C_HEADS * C_NOPE_DIM, [(hd * C_NOPE_DIM, C_NOPE_DIM) for hd in range(C_HEADS)]),
        'c_kr_gain': r2(jnp.concatenate([z(C_NOPE_DIM), p['c_kr_g'][l], z(tail)])),
        'k_place': jnp.asarray(place, bf16),
        'w_out': p['w_out'][l].astype(bf16),
        'router_w_hi': rw_hi, 'router_w_lo': (rw - rw_hi.astype(f32)).astype(bf16), 'router_b': r2(p['router_b'][l]),
    }


def _token_mix(h_all, row0, b_, t, lw, cos, sin, past):
    aq, ak, av, akb, avb, slab, cq, ckv, ckr = _proj(h_all, row0, b_ * t, lw, cos, sin)
    r3 = lambda a: a.reshape(b_, t, a.shape[-1])
    ck, cv = _mla_keys(ckv, ckr, lw)
    if past is None:
        a_o = _band(r3(aq), r3(akb), r3(avb), None, None, lw['bias_p'])
        s0_t = jnp.zeros((b_, B_HEADS, B_VAL_DIM, B_KEY_DIM), f32)
        c_o = _mla_causal(r3(cq), r3(ck), r3(cv))
    else:
        a_k, a_v, b_s, c_kv, c_kr, layer = past
        lp = a_k.shape[1]
        v_past = jnp.concatenate([a_v.astype(bf16), jnp.ones(a_v.shape[:-1] + (1,), bf16),
                                  jnp.zeros(a_v.shape[:-1] + (V_EXT - A_HEAD_DIM - 1,), bf16)], axis=-1)
        a_o = _band(r3(aq), r3(akb), r3(avb), a_k.reshape(b_, lp, A_W).astype(bf16),
                    v_past.reshape(b_, lp, A_VW), lw['bias_s'])
        s0_t = jnp.swapaxes(b_s, -1, -2)
        c_o = _mla_sample(r3(cq), c_kv, c_kr, layer, r3(ck), r3(cv), lw)
    b_o, s_t = _gla(r3(slab), s0_t, lw['b_out_g'])
    flat = lambda a: a.reshape(b_ * t, a.shape[-1])
    la = min(A_PAST, t)
    state = (r3(ak)[:, t - la:].reshape(b_, la, A_HEADS, A_HEAD_DIM),
             r3(av)[:, t - la:].reshape(b_, la, A_HEADS, A_HEAD_DIM),
             jnp.swapaxes(s_t, -1, -2), r3(ckv), r3(ckr))
    return (flat(a_o), flat(b_o), flat(c_o)), state


def kernel(x_prompt, x_sample, cache_a_k, cache_a_v, state_b_s, cache_c_kv, cache_c_kr, ln1_g, ln2_g, w_in, a_q_g, a_k_g, a_rel_bias, b_gate_w2, b_gate_b, b_out_g, c_qa_g, c_w_uq, c_kva_g, c_w_ukv, c_qn_g, c_qr_g, c_kn_g, c_kr_g, w_out, router_w, router_b, e_w_gate, e_b_gate, e_w_up, e_b_up, e_w_down, e_b_down):
    params = dict(ln1_g=ln1_g, ln2_g=ln2_g, w_in=w_in, a_q_g=a_q_g, a_k_g=a_k_g, a_rel_bias=a_rel_bias,
                  b_gate_w2=b_gate_w2, b_gate_b=b_gate_b, b_out_g=b_out_g, c_qa_g=c_qa_g, c_w_uq=c_w_uq,
                  c_kva_g=c_kva_g, c_w_ukv=c_w_ukv, c_qn_g=c_qn_g, c_qr_g=c_qr_g, c_kn_g=c_kn_g,
                  c_kr_g=c_kr_g, w_out=w_out, router_w=router_w, router_b=router_b)
    ew = dict(e_w_gate=e_w_gate, e_b_gate=e_b_gate[:, :, None, :], e_w_up=e_w_up, e_b_up=e_b_up[:, :, None, :],
              e_w_down=e_w_down, e_b_down=e_b_down[:, :, None, :])
    bp, tp, _ = x_prompt.shape
    bs, ts, _ = x_sample.shape
    depth = w_in.shape[0]
    n_p = bp * tp
    cos_p, sin_p = _rope_tables(jnp.arange(tp), ROW_TILE)
    cos_s, sin_s = _rope_tables(PAST_LEN + jnp.arange(ts), ROW_TILE)
    x_p = x_prompt.reshape(n_p, D_MODEL)
    x_s = x_sample.reshape(bs * ts, D_MODEL)
    res_p, res_s = (x_p, 0), (x_s, 0)
    st_p, st_s = [], []
    for l in range(depth):
        lw = _layer_weights(l, params)
        mix_p, s_p = _token_mix(res_p[0], res_p[1], bp, tp, lw, cos_p, sin_p, None)
        mix_s, s_s = _token_mix(res_s[0], res_s[1], bs, ts, lw, cos_s, sin_s,
                                (cache_a_k[l], cache_a_v[l], state_b_s[l], cache_c_kv, cache_c_kr, l))
        h, xn, logits = _outproj(mix_p, mix_s, res_p, res_s, lw)
        h = _moe(h, xn, logits, l, ew, split_rows=n_p if l == depth - 1 else None)
        res_p, res_s = (h, 0), (h, n_p)
        st_p.append(s_p)
        st_s.append(s_s)
    h_p, h_s = h
    stack = lambda sts, i: jnp.stack([s[i] for s in sts])
    return (h_p.reshape(bp, tp, D_MODEL), h_s.reshape(bs, ts, D_MODEL),
            stack(st_p, 0), stack(st_p, 1), stack(st_p, 2), stack(st_p, 3), stack(st_p, 4),
            stack(st_s, 0), stack(st_s, 1), stack(st_s, 2), stack(st_s, 3), stack(st_s, 4))
```

```python
import functools
import math

import jax
import jax.numpy as jnp
import numpy as np
from jax import lax
from jax.experimental import pallas as pl
from jax.experimental.pallas import tpu as pltpu

f32 = jnp.float32
bf16 = jnp.bfloat16
i32 = jnp.int32

D_MODEL = 1024
PAST_LEN = 2048
CHUNK = 64
A_HEADS = 6
A_HEAD_DIM = 64
A_BAND_CHUNKS = 8
A_PAST = A_BAND_CHUNKS * CHUNK
A_BAND = A_PAST + CHUNK
A_REL_CLIP = 128
B_HEADS = 4
B_KEY_DIM = 64
B_VAL_DIM = 96
B_GATE_RANK = 16
B_GATE_TAU = 16.0
C_HEADS = 4
C_Q_RANK = 192
C_KV_RANK = 128
C_NOPE_DIM = 64
C_ROPE_DIM = 32
C_V_DIM = 64
ROPE_THETA = 10000.0
N_EXPERTS = 32
TOP_K = 4
D_EXPERT = 1024
SWIGLU_LIMIT = 7.0
SWIGLU_ALPHA = 1.702
EPS = 1e-6
NEG = -1e30
LOG2E = 1.0 / math.log(2.0)

LANES = 128
ROW_TILES = D_MODEL // LANES
C_PAD_DIM = LANES
V_EXT = LANES
A_QSCALE = A_HEAD_DIM ** -0.5 * LOG2E
C_QSCALE = (C_NOPE_DIM + C_ROPE_DIM) ** -0.5 * LOG2E

A_W = A_HEADS * A_HEAD_DIM
B_W = B_HEADS * B_VAL_DIM
C_W = C_HEADS * C_V_DIM
B_KW = B_HEADS * B_KEY_DIM
C_QW = C_HEADS * C_PAD_DIM
A_VW = A_HEADS * V_EXT
C_VW = C_HEADS * V_EXT

P_AQ, P_AK, P_AV = 0, 384, 768
P_BQ, P_BK, P_BV, P_BG = 1152, 1408, 1664, 2048
P_CQ, P_CKV, P_CKR, P_BLR = 2432, 2688, 2816, 2912
IN_P = 2944
S_BQ, S_BK, S_BV, S_LA, S_BG, S_W = 0, 256, 512, 896, 1152, 1536

ROW_TILE = 512
BAND_ROWS = 256
GLA_CHUNKS = 4
MLA_TILE = 512
MOE_BLOCK = 256
COMBINE_TILE = 128
ROW_TOK_CHUNK = 4096
VMEM_LIMIT = 56 * 1024 * 1024


def _cparams(*sem):
    return pltpu.CompilerParams(dimension_semantics=sem, vmem_limit_bytes=VMEM_LIMIT)


def _rms(x, g):
    return x * lax.rsqrt(jnp.mean(x * x, axis=-1, keepdims=True) + EPS) * g


def _rope_lanes(x, cos_t, sin_t):
    w = x.shape[1]
    lane = lax.broadcasted_iota(i32, x.shape, 1) % LANES
    first_half = jnp.logical_and(lane >= C_NOPE_DIM, lane < C_NOPE_DIM + C_ROPE_DIM // 2)
    swapped = jnp.where(first_half, pltpu.roll(x, w - C_ROPE_DIM // 2, 1), pltpu.roll(x, C_ROPE_DIM // 2, 1))
    return x * cos_t + swapped * sin_t


def _dot(a, b):
    return jnp.dot(a, b, preferred_element_type=f32)


def _dot_nt(a, b):
    return lax.dot_general(a, b, (((1,), (1,)), ((), ())), preferred_element_type=f32)


def _dot_tn(a, b):
    return lax.dot_general(a, b, (((0,), (0,)), ((), ())), preferred_element_type=f32)


def _sigmoid(x):
    return 1.0 / (1.0 + jnp.exp(-x))


def _ones_column(rows):
    lane = lax.broadcasted_iota(i32, (rows, V_EXT - C_V_DIM), 1)
    return jnp.where(lane == 0, 1.0, 0.0).astype(bf16)


def _group_mean_sq(x, g):
    sq = x * x
    hi = sq.astype(bf16)
    lo = (sq - hi.astype(f32)).astype(bf16)
    return _dot(hi, g) + _dot(lo, g)


def _group_rms(x, g, gain):
    return x * lax.rsqrt(_group_mean_sq(x, g) + EPS) * gain


def _softmax_pv(s, v_blocks, widths):
    p = jnp.exp2(s - jnp.max(s, axis=-1, keepdims=True)).astype(bf16)
    acc, lo = None, 0
    for v, w in zip(v_blocks, widths):
        part = _dot(p[:, lo:lo + w], v)
        acc = part if acc is None else acc + part
        lo += w
    return acc[:, :C_V_DIM] / acc[:, C_V_DIM:C_V_DIM + 1]


def _proj_body(x_ref, g1_ref, w_ref, ga_ref, aqg_ref, akg_ref, w2_ref, gb_ref, qag_ref, wuq_ref,
               gc_ref, cqg_ref, kvag_ref, krg_ref, cos_ref, sin_ref,
               aq_o, ak_o, av_o, akb_o, avb_o, b_o, cq_o, ckv_o, ckr_o, h_sc):
    tm = x_ref.shape[0]
    xn = _rms(x_ref[...], g1_ref[...])
    h_sc[...] = _dot(xn.astype(bf16), w_ref[...])
    cos_t = cos_ref[...]
    sin_t = sin_ref[...]
    q = _group_rms(h_sc[:, P_AQ:P_AQ + A_W], ga_ref[...], aqg_ref[...])
    aq_o[...] = (q * A_QSCALE).astype(bf16)
    k = _group_rms(h_sc[:, P_AK:P_AK + A_W], ga_ref[...], akg_ref[...])
    ak_o[...] = k
    akb_o[...] = k.astype(bf16)
    av_o[...] = h_sc[:, P_AV:P_AV + A_W]
    ones = _ones_column(tm)
    for hd in range(A_HEADS):
        lo = P_AV + hd * A_HEAD_DIM
        avb_o[:, hd * V_EXT:hd * V_EXT + A_HEAD_DIM] = h_sc[:, lo:lo + A_HEAD_DIM].astype(bf16)
        avb_o[:, hd * V_EXT + A_HEAD_DIM:(hd + 1) * V_EXT] = ones
    b_o[:, S_BQ:S_LA] = h_sc[:, P_BQ:P_BG]
    z = jnp.dot(h_sc[:, P_BLR:P_BLR + B_GATE_RANK], w2_ref[...], preferred_element_type=f32,
                precision=lax.Precision.HIGHEST) + gb_ref[...]
    log_sig = jnp.minimum(z, 0.0) - jnp.log1p(jnp.exp(-jnp.abs(z)))
    b_o[:, S_LA:S_BG] = log_sig * (1.0 / B_GATE_TAU)
    b_o[:, S_BG:S_W] = h_sc[:, P_BG:P_BG + B_W]
    cql = _rms(h_sc[:, P_CQ:P_CQ + C_Q_RANK], qag_ref[...])
    cq = _group_rms(_dot(cql.astype(bf16), wuq_ref[...]), gc_ref[...], cqg_ref[...])
    cos4 = jnp.concatenate([cos_t] * C_HEADS, axis=1)
    sin4 = jnp.concatenate([sin_t] * C_HEADS, axis=1)
    cq_o[...] = (_rope_lanes(cq, cos4, sin4) * C_QSCALE).astype(bf16)
    ckv_o[...] = _rms(h_sc[:, P_CKV:P_CKV + C_KV_RANK], kvag_ref[...])
    kr = _group_rms(h_sc[:, P_CKR:P_CKR + LANES], gc_ref[0:LANES, 0:LANES], krg_ref[...])
    ckr_o[...] = _rope_lanes(kr, cos_t, sin_t)[:, C_NOPE_DIM:C_NOPE_DIM + C_ROPE_DIM]


def _proj(h_all, row0, n, lw, cos, sin):
    tm = ROW_TILE
    blk0 = row0 // tm
    npos = cos.shape[0] // tm
    row = lambda w: pl.BlockSpec((tm, w), lambda i: (i, 0))
    const = lambda a: pl.BlockSpec(a.shape, lambda i: (0,) * a.ndim)
    pos = pl.BlockSpec((tm, LANES), lambda i: (i % npos, 0))
    consts = (lw['ln1_g'], lw['w_in'], lw['g_a'], lw['a_q_g'], lw['a_k_g'], lw['b_gate_w2'], lw['b_gate_b'],
              lw['c_qa_g'], lw['c_w_uq'], lw['g_c'], lw['c_q_gain'], lw['c_kva_g'], lw['c_kr_gain'])
    out_w = ((A_W, bf16), (A_W, f32), (A_W, f32), (A_W, bf16), (A_VW, bf16), (S_W, f32),
             (C_QW, bf16), (C_KV_RANK, f32), (C_ROPE_DIM, f32))
    return pl.pallas_call(
        _proj_body,
        grid=(n // tm,),
        in_specs=[pl.BlockSpec((tm, D_MODEL), lambda i: (i + blk0, 0))] + [const(a) for a in consts] + [pos, pos],
        out_specs=[row(w) for w, _ in out_w],
        out_shape=[jax.ShapeDtypeStruct((n, w), dt) for w, dt in out_w],
        scratch_shapes=[pltpu.VMEM((tm, IN_P), f32)],
        compiler_params=_cparams("parallel"),
        name="proj",
    )(h_all, *consts, cos, sin)


def _expand_keys(kv, kr, g_k, k_gain, place, k_dst, v_dst, rows):
    kn = _group_rms(kv[:, :C_HEADS * C_NOPE_DIM], g_k, k_gain).astype(bf16)
    k_dst[rows, :] = _dot(jnp.concatenate([kn, kr], axis=1), place).astype(bf16)
    v = kv[:, C_HEADS * C_NOPE_DIM:]
    lane = lax.broadcasted_iota(i32, v.shape, 1) % LANES
    v_dst[rows, :] = jnp.where(lane == C_V_DIM, 1.0, v).astype(bf16)


def _mla_keys_body(ckv_ref, ckr_ref, w_ref, gk_ref, kg_ref, pl_ref, ck_o, cv_o):
    kv = _dot(ckv_ref[...].astype(bf16), w_ref[...])
    _expand_keys(kv, ckr_ref[...].astype(bf16), gk_ref[...], kg_ref[...], pl_ref[...], ck_o, cv_o, slice(None))


def _mla_keys(ckv, ckr, lw):
    n = ckv.shape[0]
    tm = ROW_TILE
    row = lambda w: pl.BlockSpec((tm, w), lambda i: (i, 0))
    const = lambda a: pl.BlockSpec(a.shape, lambda i: (0,) * a.ndim)
    consts = (lw['c_w_ukv'], lw['g_k'], lw['c_k_gain'], lw['k_place'])
    return pl.pallas_call(
        _mla_keys_body,
        grid=(n // tm,),
        in_specs=[row(C_KV_RANK), row(C_ROPE_DIM)] + [const(a) for a in consts],
        out_specs=[row(C_QW), row(C_VW)],
        out_shape=[jax.ShapeDtypeStruct((n, C_QW), bf16), jax.ShapeDtypeStruct((n, C_VW), bf16)],
        compiler_params=_cparams("parallel"),
        name="mla_keys",
    )(ckv, ckr, *consts)


def _band_body(q_ref, *refs, n_kb, past_from_seq):
    k_refs, v_refs = refs[:n_kb], refs[n_kb:2 * n_kb]
    bias_ref, o_ref = refs[2 * n_kb], refs[2 * n_kb + 1]
    rows = q_ref.shape[0]
    widths = [k.shape[0] for k in k_refs]
    if past_from_seq:
        col = lax.broadcasted_iota(i32, (rows, sum(widths)), 1)
        exists = col + (pl.program_id(1) * rows - A_PAST) >= 0
    for hd in range(A_HEADS):
        qs = slice(hd * A_HEAD_DIM, (hd + 1) * A_HEAD_DIM)
        vs = slice(hd * V_EXT, (hd + 1) * V_EXT)
        q = q_ref[:, qs]
        s = jnp.concatenate([_dot_nt(q, k[:, qs]) for k in k_refs], axis=1) + bias_ref[hd]
        if past_from_seq:
            s = jnp.where(exists, s, NEG)
        o_ref[:, qs] = _softmax_pv(s, [v[:, vs] for v in v_refs], widths).astype(bf16)


def _band(q, k, v, k_past, v_past, bias):
    b_, t, _ = q.shape
    from_seq = k_past is None
    rows = BAND_ROWS if from_seq else t
    n_back = A_PAST // rows if from_seq else 1
    cur = lambda w: pl.BlockSpec((None, rows, w), lambda b, i: (b, i, 0))
    if from_seq:
        back = lambda w, d: pl.BlockSpec((None, rows, w), lambda b, i: (b, jnp.maximum(i - d, 0), 0))
        k_specs = [back(A_W, d) for d in range(n_back, 0, -1)] + [cur(A_W)]
        v_specs = [back(A_VW, d) for d in range(n_back, 0, -1)] + [cur(A_VW)]
        k_args, v_args = [k] * (n_back + 1), [v] * (n_back + 1)
    else:
        past = lambda w: pl.BlockSpec((None, A_PAST, w), lambda b, i: (b, 0, 0))
        k_specs, v_specs = [past(A_W), cur(A_W)], [past(A_VW), cur(A_VW)]
        k_args, v_args = [k_past, k], [v_past, v]
    return pl.pallas_call(
        functools.partial(_band_body, n_kb=len(k_specs), past_from_seq=from_seq),
        grid=(b_, t // rows),
        in_specs=[cur(A_W)] + k_specs + v_specs + [pl.BlockSpec(bias.shape, lambda b, i: (0, 0, 0))],
        out_specs=cur(A_W),
        out_shape=jax.ShapeDtypeStruct((b_, t, A_W), bf16),
        compiler_params=_cparams("parallel", "parallel"),
        name="band",
    )(q, *k_args, *v_args, bias)


def _gla_body(b_ref, s0_ref, gain_ref, o_ref, st_ref, *, chunks):
    @pl.when(pl.program_id(1) == 0)
    def _():
        st_ref[...] = s0_ref[...]

    rows = chunks * CHUNK
    r = lax.broadcasted_iota(i32, (rows, rows), 0)
    c = lax.broadcasted_iota(i32, (rows, rows), 1)
    causal = jnp.logical_and(r >= c, r // CHUNK == c // CHUNK)
    tril = (lax.broadcasted_iota(i32, (CHUNK, CHUNK), 0) >= lax.broadcasted_iota(i32, (CHUNK, CHUNK), 1)).astype(f32)
    bcum, btot = [], []
    for ch in range(chunks):
        la = b_ref[ch * CHUNK:(ch + 1) * CHUNK, S_LA:S_LA + B_KW]
        cum = jnp.dot(tril, la, preferred_element_type=f32, precision=lax.Precision.HIGHEST)
        bcum.append(cum)
        btot.append(jnp.broadcast_to(cum[CHUNK - 1:CHUNK, :], cum.shape))
    bcum = jnp.concatenate(bcum, axis=0)
    btot = jnp.concatenate(btot, axis=0)
    k = b_ref[:, S_BK:S_BK + B_KW]
    qt = b_ref[:, S_BQ:S_BQ + B_KW] * (B_KEY_DIM ** -0.5) * jnp.exp(bcum)
    kt = k * jnp.exp(-bcum)
    kd = k * jnp.exp(btot - bcum)
    decay = jnp.exp(btot)
    for hd in range(B_HEADS):
        ks = slice(hd * B_KEY_DIM, (hd + 1) * B_KEY_DIM)
        vs = slice(hd * B_VAL_DIM, (hd + 1) * B_VAL_DIM)
        v = b_ref[:, S_BV + vs.start:S_BV + vs.stop]
        att = jnp.where(causal, _dot_nt(qt[:, ks], kt[:, ks]), 0.0)
        o_intra = _dot(att, v)
        st = st_ref[hd]
        outs = []
        for ch in range(chunks):
            rs = slice(ch * CHUNK, (ch + 1) * CHUNK)
            outs.append(o_intra[rs] + _dot_nt(qt[rs, ks], st))
            st = st * decay[ch * CHUNK:ch * CHUNK + 1, ks] + _dot_tn(v[rs], kd[rs, ks])
        st_ref[hd] = st
        g = b_ref[:, S_BG + vs.start:S_BG + vs.stop]
        o = _rms(jnp.concatenate(outs, axis=0), gain_ref[...]) * (g * _sigmoid(g))
        o_ref[:, vs] = o.astype(bf16)


def _gla(slab, s0_t, gain):
    b_, t, _ = slab.shape
    chunks = min(GLA_CHUNKS, t // CHUNK)
    rows = chunks * CHUNK
    st_spec = pl.BlockSpec((None, B_HEADS, B_VAL_DIM, B_KEY_DIM), lambda b, i: (b, 0, 0, 0))
    return pl.pallas_call(
        functools.partial(_gla_body, chunks=chunks),
        grid=(b_, t // rows),
        in_specs=[pl.BlockSpec((None, rows, S_W), lambda b, i: (b, i, 0)), st_spec,
                  pl.BlockSpec(gain.shape, lambda b, i: (0, 0))],
        out_specs=[pl.BlockSpec((None, rows, B_W), lambda b, i: (b, i, 0)), st_spec],
        out_shape=[jax.ShapeDtypeStruct((b_, t, B_W), bf16),
                   jax.ShapeDtypeStruct((b_, B_HEADS, B_VAL_DIM, B_KEY_DIM), f32)],
        compiler_params=_cparams("parallel", "arbitrary"),
        name="gla",
    )(slab, s0_t, gain)


def _mla_causal_body(q_ref, k_ref, v_ref, o_ref, m_sc, acc_sc):
    i = pl.program_id(1)
    t = q_ref.shape[0]
    m_sc[...] = jnp.full(m_sc.shape, NEG, f32)
    acc_sc[...] = jnp.zeros(acc_sc.shape, f32)

    def tile(j, diagonal):
        rows = pl.ds(pl.multiple_of(j * t, t), t)
        if diagonal:
            visible = (lax.broadcasted_iota(i32, (t, t), 1) // CHUNK) <= (lax.broadcasted_iota(i32, (t, t), 0) // CHUNK)
        for hd in range(C_HEADS):
            hs = slice(hd * C_PAD_DIM, (hd + 1) * C_PAD_DIM)
            s = _dot_nt(q_ref[:, hs], k_ref[rows, hs])
            if diagonal:
                s = jnp.where(visible, s, NEG)
            m_old = m_sc[hd]
            m_new = jnp.maximum(m_old, jnp.max(s, axis=-1, keepdims=True))
            p = jnp.exp2(s - jnp.concatenate([m_new] * (t // LANES), axis=1))
            acc_sc[hd] = jnp.exp2(m_old - m_new) * acc_sc[hd] + _dot(p.astype(bf16), v_ref[rows, hs])
            m_sc[hd] = m_new

    def earlier(j, carry):
        tile(j, False)
        return carry

    lax.fori_loop(0, i, earlier, 0)
    tile(i, True)
    for hd in range(C_HEADS):
        acc = acc_sc[hd]
        o_ref[:, hd * C_V_DIM:(hd + 1) * C_V_DIM] = (acc[:, :C_V_DIM] / acc[:, C_V_DIM:C_V_DIM + 1]).astype(bf16)


def _mla_causal(q, k, v):
    b_, t, _ = q.shape
    tq = MLA_TILE
    whole = lambda w: pl.BlockSpec((None, t, w), lambda b, i: (b, 0, 0))
    return pl.pallas_call(
        _mla_causal_body,
        grid=(b_, t // tq),
        in_specs=[pl.BlockSpec((None, tq, C_QW), lambda b, i: (b, i, 0)), whole(C_QW), whole(C_VW)],
        out_specs=pl.BlockSpec((None, tq, C_W), lambda b, i: (b, i, 0)),
        out_shape=jax.ShapeDtypeStruct((b_, t, C_W), bf16),
        scratch_shapes=[pltpu.VMEM((C_HEADS, tq, LANES), f32), pltpu.VMEM((C_HEADS, tq, V_EXT), f32)],
        compiler_params=_cparams("parallel", "arbitrary"),
        name="mla_causal",
    )(q, k, v)


def _mla_sample_body(q_ref, ckv_ref, ckr_ref, kn_ref, vn_ref, w_ref, gk_ref, kg_ref, pl_ref, o_ref, k_sc, v_sc):
    past = ckv_ref.shape[0]
    kv = _dot(ckv_ref[...].astype(bf16), w_ref[...])
    _expand_keys(kv, ckr_ref[...].astype(bf16), gk_ref[...], kg_ref[...], pl_ref[...], k_sc, v_sc, slice(0, past))
    k_sc[past:, :] = kn_ref[...]
    v_sc[past:, :] = vn_ref[...]
    total = k_sc.shape[0]
    for hd in range(C_HEADS):
        hs = slice(hd * C_PAD_DIM, (hd + 1) * C_PAD_DIM)
        s = _dot_nt(q_ref[:, hs], k_sc[:, hs])
        o_ref[:, hd * C_V_DIM:(hd + 1) * C_V_DIM] = _softmax_pv(s, [v_sc[:, hs]], [total]).astype(bf16)


def _mla_sample(q, c_kv, c_kr, layer, k_new, v_new, lw):
    b_, s_len, _ = q.shape
    past = c_kv.shape[2]
    per_b = lambda r, w: pl.BlockSpec((None, r, w), lambda b: (b, 0, 0))
    cache = lambda w: pl.BlockSpec((None, None, past, w), lambda b: (layer, b, 0, 0))
    const = lambda a: pl.BlockSpec(a.shape, lambda b: (0,) * a.ndim)
    consts = (lw['c_w_ukv'], lw['g_k'], lw['c_k_gain'], lw['k_place'])
    return pl.pallas_call(
        _mla_sample_body,
        grid=(b_,),
        in_specs=[per_b(s_len, C_QW), cache(C_KV_RANK), cache(C_ROPE_DIM), per_b(s_len, C_QW),
                  per_b(s_len, C_VW)] + [const(a) for a in consts],
        out_specs=per_b(s_len, C_W),
        out_shape=jax.ShapeDtypeStruct((b_, s_len, C_W), bf16),
        scratch_shapes=[pltpu.VMEM((past + s_len, C_QW), bf16), pltpu.VMEM((past + s_len, C_VW), bf16)],
        compiler_params=_cparams("parallel"),
        name="mla_sample",
    )(q, c_kv, c_kr, k_new, v_new, *consts)


def _outproj_body(ap_ref, bp_ref, cp_ref, hp_ref, as_ref, bs_ref, cs_ref, hs_ref, w_ref, g2_ref, rwh_ref, rwl_ref,
                  rb_ref, h_o, xn_o, lg_o, *, prompt_tiles):
    def run(a_ref, b_ref, c_ref, h_ref):
        y = _dot(a_ref[...], w_ref[0:A_W, :])
        y += _dot(b_ref[...], w_ref[A_W:A_W + B_W, :])
        y += _dot(c_ref[...], w_ref[A_W + B_W:, :])
        h = h_ref[...] + y
        h_o[...] = h
        xn = _rms(h, g2_ref[...])
        for c in range(ROW_TILES):
            xn_o[pl.ds(c, h.shape[0], stride=ROW_TILES), :] = xn[:, c * LANES:(c + 1) * LANES]
        xh = xn.astype(bf16)
        xl = (xn - xh.astype(f32)).astype(bf16)
        lg_o[...] = _dot(xh, rwh_ref[...]) + _dot(xl, rwh_ref[...]) + _dot(xh, rwl_ref[...]) + rb_ref[...]

    i = pl.program_id(0)
    pl.when(i < prompt_tiles)(functools.partial(run, ap_ref, bp_ref, cp_ref, hp_ref))
    pl.when(i >= prompt_tiles)(functools.partial(run, as_ref, bs_ref, cs_ref, hs_ref))


def _outproj(mix_p, mix_s, res_p, res_s, lw):
    tm = ROW_TILE
    pt = mix_p[0].shape[0] // tm
    n = mix_p[0].shape[0] + mix_s[0].shape[0]
    row = lambda w: pl.BlockSpec((tm, w), lambda i: (i, 0))
    first = lambda w, o=0: pl.BlockSpec((tm, w), lambda i: (jnp.minimum(i, pt - 1) + o, 0))
    second = lambda w, o=0: pl.BlockSpec((tm, w), lambda i: (jnp.maximum(i - pt, 0) + o, 0))
    const = lambda a: pl.BlockSpec(a.shape, lambda i: (0,) * a.ndim)
    consts = (lw['w_out'], lw['ln2_g'], lw['router_w_hi'], lw['router_w_lo'], lw['router_b'])
    return pl.pallas_call(
        functools.partial(_outproj_body, prompt_tiles=pt),
        grid=(n // tm,),
        in_specs=[first(A_W), first(B_W), first(C_W), first(D_MODEL, res_p[1] // tm),
                  second(A_W), second(B_W), second(C_W), second(D_MODEL, res_s[1] // tm)]
        + [const(a) for a in consts],
        out_specs=[row(D_MODEL), pl.BlockSpec((tm * ROW_TILES, LANES), lambda i: (i, 0)), row(N_EXPERTS)],
        out_shape=[jax.ShapeDtypeStruct((n, D_MODEL), f32), jax.ShapeDtypeStruct((n * ROW_TILES, LANES), f32),
                   jax.ShapeDtypeStruct((n, N_EXPERTS), f32)],
        compiler_params=_cparams("parallel"),
        name="outproj",
    )(*mix_p, res_p[0], *mix_s, res_s[0], *consts)


def _route_body(lg_ref, e_o, g_o, r_o, cnt_o):
    @pl.when(pl.program_id(0) == 0)
    def _():
        cnt_o[...] = jnp.zeros(cnt_o.shape, f32)

    tm = lg_ref.shape[0]
    lane = lax.broadcasted_iota(i32, (tm, N_EXPERTS), 1)
    slot = lax.broadcasted_iota(i32, (tm, TOP_K), 1)
    work = lg_ref[...]
    onehots, vals = [], []
    e_out = jnp.zeros((tm, TOP_K), i32)
    for k in range(TOP_K):
        m = jnp.max(work, axis=-1, keepdims=True)
        idx = jnp.min(jnp.where(work == m, lane, N_EXPERTS), axis=-1, keepdims=True)
        oh = lane == idx
        work = jnp.where(oh, -jnp.inf, work)
        onehots.append(oh)
        vals.append(m)
        e_out = jnp.where(slot == k, idx, e_out)
    ex = [jnp.exp(v - vals[0]) for v in vals]
    denom = ex[0] + ex[1] + ex[2] + ex[3]
    g_out = jnp.zeros((tm, TOP_K), f32)
    sel = jnp.zeros((tm, N_EXPERTS), f32)
    for k in range(TOP_K):
        g_out = jnp.where(slot == k, ex[k] / denom, g_out)
        sel = sel + onehots[k].astype(f32)
    r = lax.broadcasted_iota(i32, (tm, tm), 0)
    c = lax.broadcasted_iota(i32, (tm, tm), 1)
    before = (c < r).astype(bf16)
    rank = _dot(before, sel.astype(bf16)) + cnt_o[...]
    r_out = jnp.zeros((tm, TOP_K), f32)
    for k in range(TOP_K):
        rk = jnp.sum(jnp.where(onehots[k], rank, 0.0), axis=-1, keepdims=True)
        r_out = jnp.where(slot == k, rk, r_out)
    e_o[...] = e_out
    g_o[...] = g_out
    r_o[...] = r_out.astype(i32)
    cnt_o[...] += jnp.sum(sel, axis=0, keepdims=True)


def _route(logits):
    n = logits.shape[0]
    tm = ROW_TILE
    row = lambda w: pl.BlockSpec((tm, w), lambda i: (i, 0))
    return pl.pallas_call(
        _route_body,
        grid=(n // tm,),
        in_specs=[row(N_EXPERTS)],
        out_specs=[row(TOP_K), row(TOP_K), row(TOP_K), pl.BlockSpec((1, N_EXPERTS), lambda i: (0, 0))],
        out_shape=[jax.ShapeDtypeStruct((n, TOP_K), i32), jax.ShapeDtypeStruct((n, TOP_K), f32),
                   jax.ShapeDtypeStruct((n, TOP_K), i32), jax.ShapeDtypeStruct((1, N_EXPERTS), f32)],
        compiler_params=_cparams("arbitrary"),
        name="route",
    )(logits)


def _row_copy(src_hbm, tok, dst, slot, j, sem):
    return pltpu.make_async_copy(src_hbm.at[pl.ds(pl.multiple_of(tok * ROW_TILES, ROW_TILES), ROW_TILES), :],
                                 dst.at[slot, pl.ds(j * ROW_TILES, ROW_TILES), :], sem.at[slot])


def _experts_body(blk_e_ref, row_tok_ref, used_ref, x_hbm, wg_ref, bg_ref, wu_ref, bu_ref, wd_ref, bd_ref, y_o,
                  xbuf, wbf, sem):
    b = pl.program_id(0)
    bm = MOE_BLOCK
    last = used_ref[0] - 1
    slot = b % 2

    def wait_block(s):
        pltpu.make_async_copy(x_hbm.at[pl.ds(0, bm * ROW_TILES), :], xbuf.at[s], sem.at[s]).wait()

    @pl.when(b == 0)
    def _():
        def issue(j, carry):
            _row_copy(x_hbm, row_tok_ref[j], xbuf, 0, j, sem).start()
            return carry
        lax.fori_loop(0, bm, issue, 0)

    @pl.when(b <= last)
    def _():
        wait_block(slot)
        nxt = jnp.minimum(b + 1, last) * bm
        for s in range(2):
            @pl.when(slot != s)
            def _(s=s):
                for j in range(bm):
                    _row_copy(x_hbm, row_tok_ref[nxt + j], xbuf, s, j, sem).start(priority=j % 2)

        @pl.when(jnp.logical_or(b == 0, blk_e_ref[b] != blk_e_ref[jnp.maximum(b - 1, 0)]))
        def _():
            wbf[0] = wg_ref[...].astype(bf16)
            wbf[1] = wu_ref[...].astype(bf16)
            wbf[2] = wd_ref[...].astype(bf16)

        x = jnp.concatenate([xbuf[slot, pl.ds(c, bm, stride=ROW_TILES), :] for c in range(ROW_TILES)],
                            axis=1).astype(bf16)
        g = _dot(x, wbf[0]) + bg_ref[...]
        u = _dot(x, wbf[1]) + bu_ref[...]
        g = jnp.minimum(g, SWIGLU_LIMIT)
        u = jnp.clip(u, -SWIGLU_LIMIT, SWIGLU_LIMIT)
        hdn = (u + 1.0) * (g * _sigmoid(g * SWIGLU_ALPHA))
        y_o[...] = _dot(hdn.astype(bf16), wbf[2]) + bd_ref[...]

    @pl.when(b == last)
    def _():
        wait_block(1 - slot)

    @pl.when(b > last)
    def _():
        y_o[...] = jnp.zeros(y_o.shape, f32)


def _experts(x, blk_e, row_tok, n_used, layer, ew):
    nblk = blk_e.shape[0]
    bm = MOE_BLOCK
    wspec = lambda: pl.BlockSpec((None, None, D_MODEL, D_EXPERT), lambda b, be, rt, nu: (layer, be[b], 0, 0))
    bspec = lambda: pl.BlockSpec((None, None, 1, D_EXPERT), lambda b, be, rt, nu: (layer, be[b], 0, 0))
    grid_spec = pltpu.PrefetchScalarGridSpec(
        num_scalar_prefetch=3,
        grid=(nblk,),
        in_specs=[pl.BlockSpec(memory_space=pl.ANY), wspec(), bspec(), wspec(), bspec(), wspec(), bspec()],
        out_specs=pl.BlockSpec((bm, D_MODEL), lambda b, be, rt, nu: (b, 0)),
        scratch_shapes=[pltpu.VMEM((2, bm * ROW_TILES, LANES), f32),
                        pltpu.VMEM((3, D_MODEL, D_EXPERT), bf16), pltpu.SemaphoreType.DMA((2,))],
    )
    return pl.pallas_call(
        _experts_body,
        grid_spec=grid_spec,
        out_shape=jax.ShapeDtypeStruct((nblk * bm, D_MODEL), f32),
        compiler_params=_cparams("arbitrary"),
        name="experts",
    )(blk_e, row_tok, n_used, x, ew['e_w_gate'], ew['e_b_gate'], ew['e_w_up'], ew['e_b_up'],
      ew['e_w_down'], ew['e_b_down'])


def _combine_body(dest_ref, y_hbm, h_ref, g_ref, *refs, first_tiles):
    o_refs, (ybuf, sem) = refs[:-2], refs[-2:]
    i = pl.program_id(0)
    nt = pl.num_programs(0)
    tt = COMBINE_TILE

    def row_copy(tile, slot, j, k):
        return pltpu.make_async_copy(y_hbm.at[pl.ds(dest_ref[(tile * tt + j) * TOP_K + k], 1), :],
                                     ybuf.at[slot, k, pl.ds(j, 1), :], sem.at[slot])

    def wait_tile(slot):
        for k in range(TOP_K):
            pltpu.make_async_copy(y_hbm.at[pl.ds(0, tt), :], ybuf.at[slot, k], sem.at[slot]).wait()

    slot = i % 2

    @pl.when(i == 0)
    def _():
        def issue(j, carry):
            for k in range(TOP_K):
                row_copy(0, 0, j, k).start(priority=k % 2)
            return carry
        lax.fori_loop(0, tt, issue, 0)

    for s in range(2):
        @pl.when(jnp.logical_and(i + 1 < nt, slot != s))
        def _(s=s):
            for j in range(tt):
                for k in range(TOP_K):
                    row_copy(i + 1, s, j, k).start(priority=k % 2)

    wait_tile(slot)
    g = g_ref[...]
    acc = h_ref[...]
    for k in range(TOP_K):
        acc = acc + ybuf[slot, k] * g[:, k:k + 1]
    if len(o_refs) == 1:
        o_refs[0][...] = acc
    else:
        @pl.when(i < first_tiles)
        def _():
            o_refs[0][...] = acc

        @pl.when(i >= first_tiles)
        def _():
            o_refs[1][...] = acc


def _combine(y_rows, dest, h, gates, split_rows=None):
    n = h.shape[0]
    tt = COMBINE_TILE
    blk = lambda f: pl.BlockSpec((tt, D_MODEL), f)
    if split_rows is None:
        ft = n // tt
        out_specs = blk(lambda i, d: (i, 0))
        out_shape = jax.ShapeDtypeStruct((n, D_MODEL), f32)
    else:
        ft = split_rows // tt
        out_specs = [blk(lambda i, d: (jnp.minimum(i, ft - 1), 0)), blk(lambda i, d: (jnp.maximum(i - ft, 0), 0))]
        out_shape = [jax.ShapeDtypeStruct((split_rows, D_MODEL), f32), jax.ShapeDtypeStruct((n - split_rows, D_MODEL), f32)]
    grid_spec = pltpu.PrefetchScalarGridSpec(
        num_scalar_prefetch=1,
        grid=(n // tt,),
        in_specs=[pl.BlockSpec(memory_space=pl.ANY), blk(lambda i, d: (i, 0)),
                  pl.BlockSpec((tt, TOP_K), lambda i, d: (i, 0))],
        out_specs=out_specs,
        scratch_shapes=[pltpu.VMEM((2, TOP_K, tt, D_MODEL), f32), pltpu.SemaphoreType.DMA((2,))],
    )
    return pl.pallas_call(
        functools.partial(_combine_body, first_tiles=ft),
        grid_spec=grid_spec,
        out_shape=out_shape,
        compiler_params=_cparams("arbitrary"),
        name="combine",
    )(dest, y_rows, h, gates)


def _row_tok_body(dest_hbm, zeros_hbm, out_ref, buf0, buf1, sem, zsem):
    ch = buf0.shape[0]
    nchunks = dest_hbm.shape[0] // ch
    bufs = (buf0, buf1)
    fill = pltpu.make_async_copy(zeros_hbm, out_ref, zsem)
    fill.start()
    fill.wait()

    def copy(c, slot):
        return pltpu.make_async_copy(dest_hbm.at[pl.ds(pl.multiple_of(c * ch, ch), ch)], bufs[slot], sem.at[slot])

    copy(0, 0).start()

    def per_pair(p, carry):
        for slot in range(2):
            c = 2 * p + slot

            @pl.when(c + 1 < nchunks)
            def _():
                copy(c + 1, 1 - slot).start()

            copy(c, slot).wait()
            base = c * (ch // TOP_K)

            def scatter(t, cc):
                for k in range(TOP_K):
                    out_ref[bufs[slot][TOP_K * t + k]] = base + t
                return cc

            lax.fori_loop(0, ch // TOP_K, scatter, 0, unroll=8)
        return carry

    lax.fori_loop(0, nchunks // 2, per_pair, 0)


def _row_tok(dest, rows):
    ch = min(ROW_TOK_CHUNK, dest.shape[0] // 2)
    assert dest.shape[0] % (2 * ch) == 0
    return pl.pallas_call(
        _row_tok_body,
        in_specs=[pl.BlockSpec(memory_space=pl.ANY), pl.BlockSpec(memory_space=pl.ANY)],
        out_specs=pl.BlockSpec(memory_space=pltpu.SMEM),
        out_shape=jax.ShapeDtypeStruct((rows,), i32),
        scratch_shapes=[pltpu.SMEM((ch,), i32), pltpu.SMEM((ch,), i32), pltpu.SemaphoreType.DMA((2,)),
                        pltpu.SemaphoreType.DMA(())],
        name="row_tok",
    )(dest, jnp.zeros((rows,), i32))


def _moe(h, xn, logits, layer, ew, split_rows=None):
    n = h.shape[0]
    bm = MOE_BLOCK
    top_e, gates, rank, counts = _route(logits)
    counts = counts[0].astype(i32)
    padded = ((counts + bm - 1) // bm) * bm
    pend = jnp.cumsum(padded)
    pstart = pend - padded
    expert = jnp.arange(N_EXPERTS, dtype=i32)
    dest = jnp.sum(jnp.where(top_e[:, :, None] == expert, pstart, 0), axis=-1) + rank
    nblk = (n * TOP_K) // bm + N_EXPERTS
    row_tok = _row_tok(dest.reshape(-1), nblk * bm)
    blk_start = jnp.arange(nblk, dtype=i32) * bm
    blk_e = jnp.minimum(jnp.sum((pend[None, :] <= blk_start[:, None]).astype(i32), axis=1), N_EXPERTS - 1)
    y_rows = _experts(xn, blk_e, row_tok, pend[-1:] // bm, layer, ew)
    return _combine(y_rows, dest.reshape(-1), h, gates, split_rows)


def _rope_tables(pos, tile_rows):
    half = C_ROPE_DIM // 2
    inv = ROPE_THETA ** (-jnp.arange(half, dtype=f32) / half)
    ang = pos.astype(f32)[:, None] * inv[None, :]
    cos, sin = jnp.cos(ang), jnp.sin(ang)
    n = ang.shape[0]
    tail = jnp.zeros((n, LANES - C_NOPE_DIM - C_ROPE_DIM), f32)
    cos_t = jnp.concatenate([jnp.ones((n, C_NOPE_DIM), f32), cos, cos, tail], axis=1)
    sin_t = jnp.concatenate([jnp.zeros((n, C_NOPE_DIM), f32), -sin, sin, tail], axis=1)
    reps = max(1, tile_rows // n)
    return jnp.tile(cos_t, (reps, 1)), jnp.tile(sin_t, (reps, 1))


def _band_bias(rel_bias, rows):
    cols = A_PAST + rows
    period = cols + rows
    x = jnp.arange(period)
    x = jnp.where(x < cols, x, x - period)
    idx = jnp.clip(A_PAST - x, -A_REL_CLIP, A_REL_CLIP) + A_REL_CLIP
    v = rel_bias[:, idx].astype(f32) * LOG2E
    skew = jnp.tile(v, (1, rows))[:, :rows * (period - 1)].reshape(-1, rows, period - 1)
    r = jnp.arange(rows)[:, None]
    c = jnp.arange(cols)[None, :]
    lo = (r // CHUNK) * CHUNK
    in_band = jnp.logical_and(c >= lo, c < lo + A_BAND)
    return jnp.where(in_band[None], skew[:, :, :cols], NEG)


def _group_matrix(width, groups):
    g = np.zeros((width, width), np.float32)
    for start, size in groups:
        g[start:start + size, start:start + size] = 1.0 / size
    return jnp.asarray(g, bf16)


def _layer_weights(l, p):
    w = p['w_in'][l]
    seg = lambda o, n: w[:, o:o + n]
    zc = lambda n: jnp.zeros((D_MODEL, n), f32)
    o_aq, o_ak, o_av, o_bq, o_bk, o_bv = 0, 384, 768, 1152, 1408, 1664
    o_blr, o_bg, o_cq, o_ckv, o_ckr = 2048, 2064, 2448, 2640, 2768
    w_in = jnp.concatenate([
        seg(o_aq, 384), seg(o_ak, 384), seg(o_av, 384), seg(o_bq, 256), seg(o_bk, 256), seg(o_bv, 384),
        seg(o_bg, 384), seg(o_cq, C_Q_RANK), zc(P_CKV - P_CQ - C_Q_RANK), seg(o_ckv, 128),
        zc(C_NOPE_DIM), seg(o_ckr, 32), seg(o_blr, 16), zc(IN_P - P_BLR - B_GATE_RANK)], axis=1).astype(bf16)
    hq = C_NOPE_DIM + C_ROPE_DIM
    pad_heads = lambda a, n: jnp.pad(a.reshape(a.shape[0], C_HEADS, n), ((0, 0), (0, 0), (0, C_PAD_DIM - n))
                                     ).reshape(a.shape[0], C_QW)
    wuq = pad_heads(p['c_w_uq'][l], hq).astype(bf16)
    wukv = p['c_w_ukv'][l].reshape(C_KV_RANK, C_HEADS, C_NOPE_DIM + C_V_DIM)
    wukv = jnp.concatenate([wukv[:, :, :C_NOPE_DIM].reshape(C_KV_RANK, -1),
                            pad_heads(wukv[:, :, C_NOPE_DIM:].reshape(C_KV_RANK, -1), C_V_DIM)], axis=1).astype(bf16)
    r2 = lambda a: a.reshape(1, -1)
    z = lambda n: jnp.zeros((n,), f32)
    tail = C_PAD_DIM - hq
    head_groups = [(hd * C_PAD_DIM + o, n) for hd in range(C_HEADS) for o, n in ((0, C_NOPE_DIM), (C_NOPE_DIM, C_ROPE_DIM))]
    place = np.zeros((C_HEADS * C_NOPE_DIM + C_ROPE_DIM, C_QW), np.float32)
    for hd in range(C_HEADS):
        place[hd * C_NOPE_DIM + np.arange(C_NOPE_DIM), hd * C_PAD_DIM + np.arange(C_NOPE_DIM)] = 1.0
        place[C_HEADS * C_NOPE_DIM + np.arange(C_ROPE_DIM), hd * C_PAD_DIM + C_NOPE_DIM + np.arange(C_ROPE_DIM)] = 1.0
    rw = p['router_w'][l]
    rw_hi = rw.astype(bf16)
    return {
        'ln1_g': r2(p['ln1_g'][l]), 'ln2_g': r2(p['ln2_g'][l]), 'w_in': w_in,
        'g_a': _group_matrix(A_W, [(hd * A_HEAD_DIM, A_HEAD_DIM) for hd in range(A_HEADS)]),
        'a_q_g': r2(jnp.tile(p['a_q_g'][l], A_HEADS)), 'a_k_g': r2(jnp.tile(p['a_k_g'][l], A_HEADS)),
        'bias_p': _band_bias(p['a_rel_bias'][l], BAND_ROWS), 'bias_s': _band_bias(p['a_rel_bias'][l], CHUNK),
        'b_gate_w2': p['b_gate_w2'][l], 'b_gate_b': r2(p['b_gate_b'][l]), 'b_out_g': r2(p['b_out_g'][l]),
        'c_qa_g': r2(p['c_qa_g'][l]), 'c_w_uq': wuq, 'c_kva_g': r2(p['c_kva_g'][l]), 'c_w_ukv': wukv,
        'g_c': _group_matrix(C_QW, head_groups),
        'c_q_gain': r2(jnp.tile(jnp.concatenate([p['c_qn_g'][l], p['c_qr_g'][l], z(tail)]), C_HEADS)),
        'c_k_gain': r2(jnp.tile(p['c_kn_g'][l], C_HEADS)),
        'g_k': _group_matrix(C_HEADS * C_NOPE_DIM, [(hd * C_NOPE_DIM, C_NOPE_DIM) for hd in range(C_HEADS)]),
        'c_kr_gain': r2(jnp.concatenate([z(C_NOPE_DIM), p['c_kr_g'][l], z(tail)])),
        'k_place': jnp.asarray(place, bf16),
        'w_out': p['w_out'][l].astype(bf16),
        'router_w_hi': rw_hi, 'router_w_lo': (rw - rw_hi.astype(f32)).astype(bf16), 'router_b': r2(p['router_b'][l]),
    }


def _token_mix(h_all, row0, b_, t, lw, cos, sin, past):
    aq, ak, av, akb, avb, slab, cq, ckv, ckr = _proj(h_all, row0, b_ * t, lw, cos, sin)
    r3 = lambda a: a.reshape(b_, t, a.shape[-1])
    ck, cv = _mla_keys(ckv, ckr, lw)
    if past is None:
        a_o = _band(r3(aq), r3(akb), r3(avb), None, None, lw['bias_p'])
        s0_t = jnp.zeros((b_, B_HEADS, B_VAL_DIM, B_KEY_DIM), f32)
        c_o = _mla_causal(r3(cq), r3(ck), r3(cv))
    else:
        a_k, a_v, b_s, c_kv, c_kr, layer = past
        lp = a_k.shape[1]
        v_past = jnp.concatenate([a_v.astype(bf16), jnp.ones(a_v.shape[:-1] + (1,), bf16),
                                  jnp.zeros(a_v.shape[:-1] + (V_EXT - A_HEAD_DIM - 1,), bf16)], axis=-1)
        a_o = _band(r3(aq), r3(akb), r3(avb), a_k.reshape(b_, lp, A_W).astype(bf16),
                    v_past.reshape(b_, lp, A_VW), lw['bias_s'])
        s0_t = jnp.swapaxes(b_s, -1, -2)
        c_o = _mla_sample(r3(cq), c_kv, c_kr, layer, r3(ck), r3(cv), lw)
    b_o, s_t = _gla(r3(slab), s0_t, lw['b_out_g'])
    flat = lambda a: a.reshape(b_ * t, a.shape[-1])
    la = min(A_PAST, t)
    state = (r3(ak)[:, t - la:].reshape(b_, la, A_HEADS, A_HEAD_DIM),
             r3(av)[:, t - la:].reshape(b_, la, A_HEADS, A_HEAD_DIM),
             jnp.swapaxes(s_t, -1, -2), r3(ckv), r3(ckr))
    return (flat(a_o), flat(b_o), flat(c_o)), state


def kernel(x_prompt, x_sample, cache_a_k, cache_a_v, state_b_s, cache_c_kv, cache_c_kr, ln1_g, ln2_g, w_in, a_q_g, a_k_g, a_rel_bias, b_gate_w2, b_gate_b, b_out_g, c_qa_g, c_w_uq, c_kva_g, c_w_ukv, c_qn_g, c_qr_g, c_kn_g, c_kr_g, w_out, router_w, router_b, e_w_gate, e_b_gate, e_w_up, e_b_up, e_w_down, e_b_down):
    params = dict(ln1_g=ln1_g, ln2_g=ln2_g, w_in=w_in, a_q_g=a_q_g, a_k_g=a_k_g, a_rel_bias=a_rel_bias,
                  b_gate_w2=b_gate_w2, b_gate_b=b_gate_b, b_out_g=b_out_g, c_qa_g=c_qa_g, c_w_uq=c_w_uq,
                  c_kva_g=c_kva_g, c_w_ukv=c_w_ukv, c_qn_g=c_qn_g, c_qr_g=c_qr_g, c_kn_g=c_kn_g,
                  c_kr_g=c_kr_g, w_out=w_out, router_w=router_w, router_b=router_b)
    ew = dict(e_w_gate=e_w_gate, e_b_gate=e_b_gate[:, :, None, :], e_w_up=e_w_up, e_b_up=e_b_up[:, :, None, :],
              e_w_down=e_w_down, e_b_down=e_b_down[:, :, None, :])
    bp, tp, _ = x_prompt.shape
    bs, ts, _ = x_sample.shape
    depth = w_in.shape[0]
    n_p = bp * tp
    cos_p, sin_p = _rope_tables(jnp.arange(tp), ROW_TILE)
    cos_s, sin_s = _rope_tables(PAST_LEN + jnp.arange(ts), ROW_TILE)
    x_p = x_prompt.reshape(n_p, D_MODEL)
    x_s = x_sample.reshape(bs * ts, D_MODEL)
    res_p, res_s = (x_p, 0), (x_s, 0)
    st_p, st_s = [], []
    for l in range(depth):
        lw = _layer_weights(l, params)
        mix_p, s_p = _token_mix(res_p[0], res_p[1], bp, tp, lw, cos_p, sin_p, None)
        mix_s, s_s = _token_mix(res_s[0], res_s[1], bs, ts, lw, cos_s, sin_s,
                                (cache_a_k[l], cache_a_v[l], state_b_s[l], cache_c_kv, cache_c_kr, l))
        h, xn, logits = _outproj(mix_p, mix_s, res_p, res_s, lw)
        h = _moe(h, xn, logits, l, ew, split_rows=n_p if l == depth - 1 else None)
        res_p, res_s = (h, 0), (h, n_p)
        st_p.append(s_p)
        st_s.append(s_s)
    h_p, h_s = h
    stack = lambda sts, i: jnp.stack([s[i] for s in sts])
    return (h_p.reshape(bp, tp, D_MODEL), h_s.reshape(bs, ts, D_MODEL),
            stack(st_p, 0), stack(st_p, 1), stack(st_p, 2), stack(st_p, 3), stack(st_p, 4),
            stack(st_s, 0), stack(st_s, 1), stack(st_s, 2), stack(st_s, 3), stack(st_s, 4))
```

```python
import functools
import math

import jax
import jax.numpy as jnp
import numpy as np
from jax import lax
from jax.experimental import pallas as pl
from jax.experimental.pallas import tpu as pltpu

f32 = jnp.float32
bf16 = jnp.bfloat16
i32 = jnp.int32

D_MODEL = 1024
PAST_LEN = 2048
CHUNK = 64
A_HEADS = 6
A_HEAD_DIM = 64
A_BAND_CHUNKS = 8
A_PAST = A_BAND_CHUNKS * CHUNK
A_BAND = A_PAST + CHUNK
A_REL_CLIP = 128
B_HEADS = 4
B_KEY_DIM = 64
B_VAL_DIM = 96
B_GATE_RANK = 16
B_GATE_TAU = 16.0
C_HEADS = 4
C_Q_RANK = 192
C_KV_RANK = 128
C_NOPE_DIM = 64
C_ROPE_DIM = 32
C_V_DIM = 64
ROPE_THETA = 10000.0
N_EXPERTS = 32
TOP_K = 4
D_EXPERT = 1024
SWIGLU_LIMIT = 7.0
SWIGLU_ALPHA = 1.702
EPS = 1e-6
NEG = -1e30
LOG2E = 1.0 / math.log(2.0)

LANES = 128
ROW_TILES = D_MODEL // LANES
C_PAD_DIM = LANES
V_EXT = LANES
A_QSCALE = A_HEAD_DIM ** -0.5 * LOG2E
C_QSCALE = (C_NOPE_DIM + C_ROPE_DIM) ** -0.5 * LOG2E

A_W = A_HEADS * A_HEAD_DIM
B_W = B_HEADS * B_VAL_DIM
C_W = C_HEADS * C_V_DIM
B_KW = B_HEADS * B_KEY_DIM
C_QW = C_HEADS * C_PAD_DIM
A_VW = A_HEADS * V_EXT
C_VW = C_HEADS * V_EXT

P_AQ, P_AK, P_AV = 0, 384, 768
P_BQ, P_BK, P_BV, P_BG = 1152, 1408, 1664, 2048
P_CQ, P_CKV, P_CKR, P_BLR = 2432, 2688, 2816, 2912
IN_P = 2944
S_BQ, S_BK, S_BV, S_LA, S_BG, S_W = 0, 256, 512, 896, 1152, 1536

ROW_TILE = 512
BAND_ROWS = 256
GLA_CHUNKS = 4
MLA_TILE = 512
MOE_BLOCK = 256
COMBINE_TILE = 128
ROW_TOK_CHUNK = 4096
VMEM_LIMIT = 56 * 1024 * 1024


def _cparams(*sem):
    return pltpu.CompilerParams(dimension_semantics=sem, vmem_limit_bytes=VMEM_LIMIT)


def _rms(x, g):
    return x * lax.rsqrt(jnp.mean(x * x, axis=-1, keepdims=True) + EPS) * g


def _rope_lanes(x, cos_t, sin_t):
    w = x.shape[1]
    lane = lax.broadcasted_iota(i32, x.shape, 1) % LANES
    first_half = jnp.logical_and(lane >= C_NOPE_DIM, lane < C_NOPE_DIM + C_ROPE_DIM // 2)
    swapped = jnp.where(first_half, pltpu.roll(x, w - C_ROPE_DIM // 2, 1), pltpu.roll(x, C_ROPE_DIM // 2, 1))
    return x * cos_t + swapped * sin_t


def _dot(a, b):
    return jnp.dot(a, b, preferred_element_type=f32)


def _dot_nt(a, b):
    return lax.dot_general(a, b, (((1,), (1,)), ((), ())), preferred_element_type=f32)


def _dot_tn(a, b):
    return lax.dot_general(a, b, (((0,), (0,)), ((), ())), preferred_element_type=f32)


def _sigmoid(x):
    return 1.0 / (1.0 + jnp.exp(-x))


def _ones_column(rows):
    lane = lax.broadcasted_iota(i32, (rows, V_EXT - C_V_DIM), 1)
    return jnp.where(lane == 0, 1.0, 0.0).astype(bf16)


def _group_mean_sq(x, g):
    sq = x * x
    hi = sq.astype(bf16)
    lo = (sq - hi.astype(f32)).astype(bf16)
    return _dot(hi, g) + _dot(lo, g)


def _group_rms(x, g, gain):
    return x * lax.rsqrt(_group_mean_sq(x, g) + EPS) * gain


def _softmax_pv(s, v_blocks, widths):
    p = jnp.exp2(s - jnp.max(s, axis=-1, keepdims=True)).astype(bf16)
    acc, lo = None, 0
    for v, w in zip(v_blocks, widths):
        part = _dot(p[:, lo:lo + w], v)
        acc = part if acc is None else acc + part
        lo += w
    return acc[:, :C_V_DIM] / acc[:, C_V_DIM:C_V_DIM + 1]


def _proj_body(x_ref, g1_ref, w_ref, ga_ref, aqg_ref, akg_ref, w2_ref, gb_ref, qag_ref, wuq_ref,
               gc_ref, cqg_ref, kvag_ref, krg_ref, cos_ref, sin_ref,
               aq_o, ak_o, av_o, akb_o, avb_o, b_o, cq_o, ckv_o, ckr_o, h_sc):
    tm = x_ref.shape[0]
    xn = _rms(x_ref[...], g1_ref[...])
    h_sc[...] = _dot(xn.astype(bf16), w_ref[...])
    cos_t = cos_ref[...]
    sin_t = sin_ref[...]
    q = _group_rms(h_sc[:, P_AQ:P_AQ + A_W], ga_ref[...], aqg_ref[...])
    aq_o[...] = (q * A_QSCALE).astype(bf16)
    k = _group_rms(h_sc[:, P_AK:P_AK + A_W], ga_ref[...], akg_ref[...])
    ak_o[...] = k
    akb_o[...] = k.astype(bf16)
    av_o[...] = h_sc[:, P_AV:P_AV + A_W]
    ones = _ones_column(tm)
    for hd in range(A_HEADS):
        lo = P_AV + hd * A_HEAD_DIM
        avb_o[:, hd * V_EXT:hd * V_EXT + A_HEAD_DIM] = h_sc[:, lo:lo + A_HEAD_DIM].astype(bf16)
        avb_o[:, hd * V_EXT + A_HEAD_DIM:(hd + 1) * V_EXT] = ones
    b_o[:, S_BQ:S_LA] = h_sc[:, P_BQ:P_BG]
    z = jnp.dot(h_sc[:, P_BLR:P_BLR + B_GATE_RANK], w2_ref[...], preferred_element_type=f32,
                precision=lax.Precision.HIGHEST) + gb_ref[...]
    log_sig = jnp.minimum(z, 0.0) - jnp.log1p(jnp.exp(-jnp.abs(z)))
    b_o[:, S_LA:S_BG] = log_sig * (1.0 / B_GATE_TAU)
    b_o[:, S_BG:S_W] = h_sc[:, P_BG:P_BG + B_W]
    cql = _rms(h_sc[:, P_CQ:P_CQ + C_Q_RANK], qag_ref[...])
    cq = _group_rms(_dot(cql.astype(bf16), wuq_ref[...]), gc_ref[...], cqg_ref[...])
    cos4 = jnp.concatenate([cos_t] * C_HEADS, axis=1)
    sin4 = jnp.concatenate([sin_t] * C_HEADS, axis=1)
    cq_o[...] = (_rope_lanes(cq, cos4, sin4) * C_QSCALE).astype(bf16)
    ckv_o[...] = _rms(h_sc[:, P_CKV:P_CKV + C_KV_RANK], kvag_ref[...])
    kr = _group_rms(h_sc[:, P_CKR:P_CKR + LANES], gc_ref[0:LANES, 0:LANES], krg_ref[...])
    ckr_o[...] = _rope_lanes(kr, cos_t, sin_t)[:, C_NOPE_DIM:C_NOPE_DIM + C_ROPE_DIM]


def _proj(h_all, row0, n, lw, cos, sin):
    tm = ROW_TILE
    blk0 = row0 // tm
    npos = cos.shape[0] // tm
    row = lambda w: pl.BlockSpec((tm, w), lambda i: (i, 0))
    const = lambda a: pl.BlockSpec(a.shape, lambda i: (0,) * a.ndim)
    pos = pl.BlockSpec((tm, LANES), lambda i: (i % npos, 0))
    consts = (lw['ln1_g'], lw['w_in'], lw['g_a'], lw['a_q_g'], lw['a_k_g'], lw['b_gate_w2'], lw['b_gate_b'],
              lw['c_qa_g'], lw['c_w_uq'], lw['g_c'], lw['c_q_gain'], lw['c_kva_g'], lw['c_kr_gain'])
    out_w = ((A_W, bf16), (A_W, f32), (A_W, f32), (A_W, bf16), (A_VW, bf16), (S_W, f32),
             (C_QW, bf16), (C_KV_RANK, f32), (C_ROPE_DIM, f32))
    return pl.pallas_call(
        _proj_body,
        grid=(n // tm,),
        in_specs=[pl.BlockSpec((tm, D_MODEL), lambda i: (i + blk0, 0))] + [const(a) for a in consts] + [pos, pos],
        out_specs=[row(w) for w, _ in out_w],
        out_shape=[jax.ShapeDtypeStruct((n, w), dt) for w, dt in out_w],
        scratch_shapes=[pltpu.VMEM((tm, IN_P), f32)],
        compiler_params=_cparams("parallel"),
        name="proj",
    )(h_all, *consts, cos, sin)


def _expand_keys(kv, kr, g_k, k_gain, place, k_dst, v_dst, rows):
    kn = _group_rms(kv[:, :C_HEADS * C_NOPE_DIM], g_k, k_gain).astype(bf16)
    k_dst[rows, :] = _dot(jnp.concatenate([kn, kr], axis=1), place).astype(bf16)
    v = kv[:, C_HEADS * C_NOPE_DIM:]
    lane = lax.broadcasted_iota(i32, v.shape, 1) % LANES
    v_dst[rows, :] = jnp.where(lane == C_V_DIM, 1.0, v).astype(bf16)


def _mla_keys_body(ckv_ref, ckr_ref, w_ref, gk_ref, kg_ref, pl_ref, ck_o, cv_o):
    kv = _dot(ckv_ref[...].astype(bf16), w_ref[...])
    _expand_keys(kv, ckr_ref[...].astype(bf16), gk_ref[...], kg_ref[...], pl_ref[...], ck_o, cv_o, slice(None))


def _mla_keys(ckv, ckr, lw):
    n = ckv.shape[0]
    tm = ROW_TILE
    row = lambda w: pl.BlockSpec((tm, w), lambda i: (i, 0))
    const = lambda a: pl.BlockSpec(a.shape, lambda i: (0,) * a.ndim)
    consts = (lw['c_w_ukv'], lw['g_k'], lw['c_k_gain'], lw['k_place'])
    return pl.pallas_call(
        _mla_keys_body,
        grid=(n // tm,),
        in_specs=[row(C_KV_RANK), row(C_ROPE_DIM)] + [const(a) for a in consts],
        out_specs=[row(C_QW), row(C_VW)],
        out_shape=[jax.ShapeDtypeStruct((n, C_QW), bf16), jax.ShapeDtypeStruct((n, C_VW), bf16)],
        compiler_params=_cparams("parallel"),
        name="mla_keys",
    )(ckv, ckr, *consts)


def _band_body(q_ref, *refs, n_kb, past_from_seq):
    k_refs, v_refs = refs[:n_kb], refs[n_kb:2 * n_kb]
    bias_ref, o_ref = refs[2 * n_kb], refs[2 * n_kb + 1]
    rows = q_ref.shape[0]
    widths = [k.shape[0] for k in k_refs]
    if past_from_seq:
        col = lax.broadcasted_iota(i32, (rows, sum(widths)), 1)
        exists = col + (pl.program_id(1) * rows - A_PAST) >= 0
    for hd in range(A_HEADS):
        qs = slice(hd * A_HEAD_DIM, (hd + 1) * A_HEAD_DIM)
        vs = slice(hd * V_EXT, (hd + 1) * V_EXT)
        q = q_ref[:, qs]
        s = jnp.concatenate([_dot_nt(q, k[:, qs]) for k in k_refs], axis=1) + bias_ref[hd]
        if past_from_seq:
            s = jnp.where(exists, s, NEG)
        o_ref[:, qs] = _softmax_pv(s, [v[:, vs] for v in v_refs], widths).astype(bf16)


def _band(q, k, v, k_past, v_past, bias):
    b_, t, _ = q.shape
    from_seq = k_past is None
    rows = BAND_ROWS if from_seq else t
    n_back = A_PAST // rows if from_seq else 1
    cur = lambda w: pl.BlockSpec((None, rows, w), lambda b, i: (b, i, 0))
    if from_seq:
        back = lambda w, d: pl.BlockSpec((None, rows, w), lambda b, i: (b, jnp.maximum(i - d, 0), 0))
        k_specs = [back(A_W, d) for d in range(n_back, 0, -1)] + [cur(A_W)]
        v_specs = [back(A_VW, d) for d in range(n_back, 0, -1)] + [cur(A_VW)]
        k_args, v_args = [k] * (n_back + 1), [v] * (n_back + 1)
    else:
        past = lambda w: pl.BlockSpec((None, A_PAST, w), lambda b, i: (b, 0, 0))
        k_specs, v_specs = [past(A_W), cur(A_W)], [past(A_VW), cur(A_VW)]
        k_args, v_args = [k_past, k], [v_past, v]
    return pl.pallas_call(
        functools.partial(_band_body, n_kb=len(k_specs), past_from_seq=from_seq),
        grid=(b_, t // rows),
        in_specs=[cur(A_W)] + k_specs + v_specs + [pl.BlockSpec(bias.shape, lambda b, i: (0, 0, 0))],
        out_specs=cur(A_W),
        out_shape=jax.ShapeDtypeStruct((b_, t, A_W), bf16),
        compiler_params=_cparams("parallel", "parallel"),
        name="band",
    )(q, *k_args, *v_args, bias)


def _gla_body(b_ref, s0_ref, gain_ref, o_ref, st_ref, *, chunks):
    @pl.when(pl.program_id(1) == 0)
    def _():
        st_ref[...] = s0_ref[...]

    rows = chunks * CHUNK
    r = lax.broadcasted_iota(i32, (rows, rows), 0)
    c = lax.broadcasted_iota(i32, (rows, rows), 1)
    causal = jnp.logical_and(r >= c, r // CHUNK == c // CHUNK)
    tril = (lax.broadcasted_iota(i32, (CHUNK, CHUNK), 0) >= lax.broadcasted_iota(i32, (CHUNK, CHUNK), 1)).astype(f32)
    bcum, btot = [], []
    for ch in range(chunks):
        la = b_ref[ch * CHUNK:(ch + 1) * CHUNK, S_LA:S_LA + B_KW]
        cum = jnp.dot(tril, la, preferred_element_type=f32, precision=lax.Precision.HIGHEST)
        bcum.append(cum)
        btot.append(jnp.broadcast_to(cum[CHUNK - 1:CHUNK, :], cum.shape))
    bcum = jnp.concatenate(bcum, axis=0)
    btot = jnp.concatenate(btot, axis=0)
    k = b_ref[:, S_BK:S_BK + B_KW]
    qt = b_ref[:, S_BQ:S_BQ + B_KW] * (B_KEY_DIM ** -0.5) * jnp.exp(bcum)
    kt = k * jnp.exp(-bcum)
    kd = k * jnp.exp(btot - bcum)
    decay = jnp.exp(btot)
    for hd in range(B_HEADS):
        ks = slice(hd * B_KEY_DIM, (hd + 1) * B_KEY_DIM)
        vs = slice(hd * B_VAL_DIM, (hd + 1) * B_VAL_DIM)
        v = b_ref[:, S_BV + vs.start:S_BV + vs.stop]
        att = jnp.where(causal, _dot_nt(qt[:, ks], kt[:, ks]), 0.0)
        o_intra = _dot(att, v)
        st = st_ref[hd]
        outs = []
        for ch in range(chunks):
            rs = slice(ch * CHUNK, (ch + 1) * CHUNK)
            outs.append(o_intra[rs] + _dot_nt(qt[rs, ks], st))
            st = st * decay[ch * CHUNK:ch * CHUNK + 1, ks] + _dot_tn(v[rs], kd[rs, ks])
        st_ref[hd] = st
        g = b_ref[:, S_BG + vs.start:S_BG + vs.stop]
        o = _rms(jnp.concatenate(outs, axis=0), gain_ref[...]) * (g * _sigmoid(g))
        o_ref[:, vs] = o.astype(bf16)


def _gla(slab, s0_t, gain):
    b_, t, _ = slab.shape
    chunks = min(GLA_CHUNKS, t // CHUNK)
    rows = chunks * CHUNK
    st_spec = pl.BlockSpec((None, B_HEADS, B_VAL_DIM, B_KEY_DIM), lambda b, i: (b, 0, 0, 0))
    return pl.pallas_call(
        functools.partial(_gla_body, chunks=chunks),
        grid=(b_, t // rows),
        in_specs=[pl.BlockSpec((None, rows, S_W), lambda b, i: (b, i, 0)), st_spec,
                  pl.BlockSpec(gain.shape, lambda b, i: (0, 0))],
        out_specs=[pl.BlockSpec((None, rows, B_W), lambda b, i: (b, i, 0)), st_spec],
        out_shape=[jax.ShapeDtypeStruct((b_, t, B_W), bf16),
                   jax.ShapeDtypeStruct((b_, B_HEADS, B_VAL_DIM, B_KEY_DIM), f32)],
        compiler_params=_cparams("parallel", "arbitrary"),
        name="gla",
    )(slab, s0_t, gain)


def _mla_causal_body(q_ref, k_ref, v_ref, o_ref, m_sc, acc_sc):
    i = pl.program_id(1)
    t = q_ref.shape[0]
    m_sc[...] = jnp.full(m_sc.shape, NEG, f32)
    acc_sc[...] = jnp.zeros(acc_sc.shape, f32)

    def tile(j, diagonal):
        rows = pl.ds(pl.multiple_of(j * t, t), t)
        if diagonal:
            visible = (lax.broadcasted_iota(i32, (t, t), 1) // CHUNK) <= (lax.broadcasted_iota(i32, (t, t), 0) // CHUNK)
        for hd in range(C_HEADS):
            hs = slice(hd * C_PAD_DIM, (hd + 1) * C_PAD_DIM)
            s = _dot_nt(q_ref[:, hs], k_ref[rows, hs])
            if diagonal:
                s = jnp.where(visible, s, NEG)
            m_old = m_sc[hd]
            m_new = jnp.maximum(m_old, jnp.max(s, axis=-1, keepdims=True))
            p = jnp.exp2(s - jnp.concatenate([m_new] * (t // LANES), axis=1))
            acc_sc[hd] = jnp.exp2(m_old - m_new) * acc_sc[hd] + _dot(p.astype(bf16), v_ref[rows, hs])
            m_sc[hd] = m_new

    def earlier(j, carry):
        tile(j, False)
        return carry

    lax.fori_loop(0, i, earlier, 0)
    tile(i, True)
    for hd in range(C_HEADS):
        acc = acc_sc[hd]
        o_ref[:, hd * C_V_DIM:(hd + 1) * C_V_DIM] = (acc[:, :C_V_DIM] / acc[:, C_V_DIM:C_V_DIM + 1]).astype(bf16)


def _mla_causal(q, k, v):
    b_, t, _ = q.shape
    tq = MLA_TILE
    whole = lambda w: pl.BlockSpec((None, t, w), lambda b, i: (b, 0, 0))
    return pl.pallas_call(
        _mla_causal_body,
        grid=(b_, t // tq),
        in_specs=[pl.BlockSpec((None, tq, C_QW), lambda b, i: (b, i, 0)), whole(C_QW), whole(C_VW)],
        out_specs=pl.BlockSpec((None, tq, C_W), lambda b, i: (b, i, 0)),
        out_shape=jax.ShapeDtypeStruct((b_, t, C_W), bf16),
        scratch_shapes=[pltpu.VMEM((C_HEADS, tq, LANES), f32), pltpu.VMEM((C_HEADS, tq, V_EXT), f32)],
        compiler_params=_cparams("parallel", "arbitrary"),
        name="mla_causal",
    )(q, k, v)


def _mla_sample_body(q_ref, ckv_ref, ckr_ref, kn_ref, vn_ref, w_ref, gk_ref, kg_ref, pl_ref, o_ref, k_sc, v_sc):
    past = ckv_ref.shape[0]
    kv = _dot(ckv_ref[...].astype(bf16), w_ref[...])
    _expand_keys(kv, ckr_ref[...].astype(bf16), gk_ref[...], kg_ref[...], pl_ref[...], k_sc, v_sc, slice(0, past))
    k_sc[past:, :] = kn_ref[...]
    v_sc[past:, :] = vn_ref[...]
    total = k_sc.shape[0]
    for hd in range(C_HEADS):
        hs = slice(hd * C_PAD_DIM, (hd + 1) * C_PAD_DIM)
        s = _dot_nt(q_ref[:, hs], k_sc[:, hs])
        o_ref[:, hd * C_V_DIM:(hd + 1) * C_V_DIM] = _softmax_pv(s, [v_sc[:, hs]], [total]).astype(bf16)


def _mla_sample(q, c_kv, c_kr, layer, k_new, v_new, lw):
    b_, s_len, _ = q.shape
    past = c_kv.shape[2]
    per_b = lambda r, w: pl.BlockSpec((None, r, w), lambda b: (b, 0, 0))
    cache = lambda w: pl.BlockSpec((None, None, past, w), lambda b: (layer, b, 0, 0))
    const = lambda a: pl.BlockSpec(a.shape, lambda b: (0,) * a.ndim)
    consts = (lw['c_w_ukv'], lw['g_k'], lw['c_k_gain'], lw['k_place'])
    return pl.pallas_call(
        _mla_sample_body,
        grid=(b_,),
        in_specs=[per_b(s_len, C_QW), cache(C_KV_RANK), cache(C_ROPE_DIM), per_b(s_len, C_QW),
                  per_b(s_len, C_VW)] + [const(a) for a in consts],
        out_specs=per_b(s_len, C_W),
        out_shape=jax.ShapeDtypeStruct((b_, s_len, C_W), bf16),
        scratch_shapes=[pltpu.VMEM((past + s_len, C_QW), bf16), pltpu.VMEM((past + s_len, C_VW), bf16)],
        compiler_params=_cparams("parallel"),
        name="mla_sample",
    )(q, c_kv, c_kr, k_new, v_new, *consts)


def _outproj_body(ap_ref, bp_ref, cp_ref, hp_ref, as_ref, bs_ref, cs_ref, hs_ref, w_ref, g2_ref, rwh_ref, rwl_ref,
                  rb_ref, h_o, xn_o, lg_o, *, prompt_tiles):
    def run(a_ref, b_ref, c_ref, h_ref):
        y = _dot(a_ref[...], w_ref[0:A_W, :])
        y += _dot(b_ref[...], w_ref[A_W:A_W + B_W, :])
        y += _dot(c_ref[...], w_ref[A_W + B_W:, :])
        h = h_ref[...] + y
        h_o[...] = h
        xn = _rms(h, g2_ref[...])
        for c in range(ROW_TILES):
            xn_o[pl.ds(c, h.shape[0], stride=ROW_TILES), :] = xn[:, c * LANES:(c + 1) * LANES]
        xh = xn.astype(bf16)
        xl = (xn - xh.astype(f32)).astype(bf16)
        lg_o[...] = _dot(xh, rwh_ref[...]) + _dot(xl, rwh_ref[...]) + _dot(xh, rwl_ref[...]) + rb_ref[...]

    i = pl.program_id(0)
    pl.when(i < prompt_tiles)(functools.partial(run, ap_ref, bp_ref, cp_ref, hp_ref))
    pl.when(i >= prompt_tiles)(functools.partial(run, as_ref, bs_ref, cs_ref, hs_ref))


def _outproj(mix_p, mix_s, res_p, res_s, lw):
    tm = ROW_TILE
    pt = mix_p[0].shape[0] // tm
    n = mix_p[0].shape[0] + mix_s[0].shape[0]
    row = lambda w: pl.BlockSpec((tm, w), lambda i: (i, 0))
    first = lambda w, o=0: pl.BlockSpec((tm, w), lambda i: (jnp.minimum(i, pt - 1) + o, 0))
    second = lambda w, o=0: pl.BlockSpec((tm, w), lambda i: (jnp.maximum(i - pt, 0) + o, 0))
    const = lambda a: pl.BlockSpec(a.shape, lambda i: (0,) * a.ndim)
    consts = (lw['w_out'], lw['ln2_g'], lw['router_w_hi'], lw['router_w_lo'], lw['router_b'])
    return pl.pallas_call(
        functools.partial(_outproj_body, prompt_tiles=pt),
        grid=(n // tm,),
        in_specs=[first(A_W), first(B_W), first(C_W), first(D_MODEL, res_p[1] // tm),
                  second(A_W), second(B_W), second(C_W), second(D_MODEL, res_s[1] // tm)]
        + [const(a) for a in consts],
        out_specs=[row(D_MODEL), pl.BlockSpec((tm * ROW_TILES, LANES), lambda i: (i, 0)), row(N_EXPERTS)],
        out_shape=[jax.ShapeDtypeStruct((n, D_MODEL), f32), jax.ShapeDtypeStruct((n * ROW_TILES, LANES), f32),
                   jax.ShapeDtypeStruct((n, N_EXPERTS), f32)],
        compiler_params=_cparams("parallel"),
        name="outproj",
    )(*mix_p, res_p[0], *mix_s, res_s[0], *consts)


def _route_body(lg_ref, e_o, g_o, r_o, cnt_o):
    @pl.when(pl.program_id(0) == 0)
    def _():
        cnt_o[...] = jnp.zeros(cnt_o.shape, f32)

    tm = lg_ref.shape[0]
    lane = lax.broadcasted_iota(i32, (tm, N_EXPERTS), 1)
    slot = lax.broadcasted_iota(i32, (tm, TOP_K), 1)
    work = lg_ref[...]
    onehots, vals = [], []
    e_out = jnp.zeros((tm, TOP_K), i32)
    for k in range(TOP_K):
        m = jnp.max(work, axis=-1, keepdims=True)
        idx = jnp.min(jnp.where(work == m, lane, N_EXPERTS), axis=-1, keepdims=True)
        oh = lane == idx
        work = jnp.where(oh, -jnp.inf, work)
        onehots.append(oh)
        vals.append(m)
        e_out = jnp.where(slot == k, idx, e_out)
    ex = [jnp.exp(v - vals[0]) for v in vals]
    denom = ex[0] + ex[1] + ex[2] + ex[3]
    g_out = jnp.zeros((tm, TOP_K), f32)
    sel = jnp.zeros((tm, N_EXPERTS), f32)
    for k in range(TOP_K):
        g_out = jnp.where(slot == k, ex[k] / denom, g_out)
        sel = sel + onehots[k].astype(f32)
    r = lax.broadcasted_iota(i32, (tm, tm), 0)
    c = lax.broadcasted_iota(i32, (tm, tm), 1)
    before = (c < r).astype(bf16)
    rank = _dot(before, sel.astype(bf16)) + cnt_o[...]
    r_out = jnp.zeros((tm, TOP_K), f32)
    for k in range(TOP_K):
        rk = jnp.sum(jnp.where(onehots[k], rank, 0.0), axis=-1, keepdims=True)
        r_out = jnp.where(slot == k, rk, r_out)
    e_o[...] = e_out
    g_o[...] = g_out
    r_o[...] = r_out.astype(i32)
    cnt_o[...] += jnp.sum(sel, axis=0, keepdims=True)


def _route(logits):
    n = logits.shape[0]
    tm = ROW_TILE
    row = lambda w: pl.BlockSpec((tm, w), lambda i: (i, 0))
    return pl.pallas_call(
        _route_body,
        grid=(n // tm,),
        in_specs=[row(N_EXPERTS)],
        out_specs=[row(TOP_K), row(TOP_K), row(TOP_K), pl.BlockSpec((1, N_EXPERTS), lambda i: (0, 0))],
        out_shape=[jax.ShapeDtypeStruct((n, TOP_K), i32), jax.ShapeDtypeStruct((n, TOP_K), f32),
                   jax.ShapeDtypeStruct((n, TOP_K), i32), jax.ShapeDtypeStruct((1, N_EXPERTS), f32)],
        compiler_params=_cparams("arbitrary"),
        name="route",
    )(logits)


def _row_copy(src_hbm, tok, dst, slot, j, sem):
    return pltpu.make_async_copy(src_hbm.at[pl.ds(pl.multiple_of(tok * ROW_TILES, ROW_TILES), ROW_TILES), :],
                                 dst.at[slot, pl.ds(j * ROW_TILES, ROW_TILES), :], sem.at[slot])


def _experts_body(blk_e_ref, row_tok_ref, used_ref, x_hbm, wg_ref, bg_ref, wu_ref, bu_ref, wd_ref, bd_ref, y_o,
                  xbuf, wbf, sem):
    b = pl.program_id(0)
    bm = MOE_BLOCK
    last = used_ref[0] - 1
    slot = b % 2

    def wait_block(s):
        pltpu.make_async_copy(x_hbm.at[pl.ds(0, bm * ROW_TILES), :], xbuf.at[s], sem.at[s]).wait()

    @pl.when(b == 0)
    def _():
        def issue(j, carry):
            _row_copy(x_hbm, row_tok_ref[j], xbuf, 0, j, sem).start()
            return carry
        lax.fori_loop(0, bm, issue, 0)

    @pl.when(b <= last)
    def _():
        wait_block(slot)
        nxt = jnp.minimum(b + 1, last) * bm
        for s in range(2):
            @pl.when(slot != s)
            def _(s=s):
                for j in range(bm):
                    _row_copy(x_hbm, row_tok_ref[nxt + j], xbuf, s, j, sem).start(priority=j % 2)

        @pl.when(jnp.logical_or(b == 0, blk_e_ref[b] != blk_e_ref[jnp.maximum(b - 1, 0)]))
        def _():
            wbf[0] = wg_ref[...].astype(bf16)
            wbf[1] = wu_ref[...].astype(bf16)
            wbf[2] = wd_ref[...].astype(bf16)

        x = jnp.concatenate([xbuf[slot, pl.ds(c, bm, stride=ROW_TILES), :] for c in range(ROW_TILES)],
                            axis=1).astype(bf16)
        g = _dot(x, wbf[0]) + bg_ref[...]
        u = _dot(x, wbf[1]) + bu_ref[...]
        g = jnp.minimum(g, SWIGLU_LIMIT)
        u = jnp.clip(u, -SWIGLU_LIMIT, SWIGLU_LIMIT)
        hdn = (u + 1.0) * (g * _sigmoid(g * SWIGLU_ALPHA))
        y = _dot(hdn.astype(bf16), wbf[2]) + bd_ref[...]
        for c in range(ROW_TILES):
            y_o[pl.ds(c, bm, stride=ROW_TILES), :] = y[:, c * LANES:(c + 1) * LANES]

    @pl.when(b == last)
    def _():
        wait_block(1 - slot)

    @pl.when(b > last)
    def _():
        y_o[...] = jnp.zeros(y_o.shape, f32)


def _experts(x, blk_e, row_tok, n_used, layer, ew):
    nblk = blk_e.shape[0]
    bm = MOE_BLOCK
    wspec = lambda: pl.BlockSpec((None, None, D_MODEL, D_EXPERT), lambda b, be, rt, nu: (layer, be[b], 0, 0))
    bspec = lambda: pl.BlockSpec((None, None, 1, D_EXPERT), lambda b, be, rt, nu: (layer, be[b], 0, 0))
    grid_spec = pltpu.PrefetchScalarGridSpec(
        num_scalar_prefetch=3,
        grid=(nblk,),
        in_specs=[pl.BlockSpec(memory_space=pl.ANY), wspec(), bspec(), wspec(), bspec(), wspec(), bspec()],
        out_specs=pl.BlockSpec((bm * ROW_TILES, LANES), lambda b, be, rt, nu: (b, 0)),
        scratch_shapes=[pltpu.VMEM((2, bm * ROW_TILES, LANES), f32),
                        pltpu.VMEM((3, D_MODEL, D_EXPERT), bf16), pltpu.SemaphoreType.DMA((2,))],
    )
    return pl.pallas_call(
        _experts_body,
        grid_spec=grid_spec,
        out_shape=jax.ShapeDtypeStruct((nblk * bm * ROW_TILES, LANES), f32),
        compiler_params=_cparams("arbitrary"),
        name="experts",
    )(blk_e, row_tok, n_used, x, ew['e_w_gate'], ew['e_b_gate'], ew['e_w_up'], ew['e_b_up'],
      ew['e_w_down'], ew['e_b_down'])


def _combine_body(dest_ref, y_hbm, h_ref, g_ref, *refs, first_tiles):
    o_refs, (ybuf, sem) = refs[:-2], refs[-2:]
    i = pl.program_id(0)
    nt = pl.num_programs(0)
    tt = COMBINE_TILE

    def row_copy(tile, slot, j, k):
        return pltpu.make_async_copy(y_hbm.at[pl.ds(pl.multiple_of(dest_ref[(tile * tt + j) * TOP_K + k] * ROW_TILES, ROW_TILES), ROW_TILES), :],
                                     ybuf.at[slot, k, pl.ds(j * ROW_TILES, ROW_TILES), :], sem.at[slot])

    def wait_tile(slot):
        for k in range(TOP_K):
            pltpu.make_async_copy(y_hbm.at[pl.ds(0, tt * ROW_TILES), :], ybuf.at[slot, k], sem.at[slot]).wait()

    slot = i % 2

    @pl.when(i == 0)
    def _():
        def issue(j, carry):
            for k in range(TOP_K):
                row_copy(0, 0, j, k).start(priority=k % 2)
            return carry
        lax.fori_loop(0, tt, issue, 0)

    for s in range(2):
        @pl.when(jnp.logical_and(i + 1 < nt, slot != s))
        def _(s=s):
            for j in range(tt):
                for k in range(TOP_K):
                    row_copy(i + 1, s, j, k).start(priority=k % 2)

    wait_tile(slot)
    g = g_ref[...]
    acc = h_ref[...]
    for k in range(TOP_K):
        yk = jnp.concatenate([ybuf[slot, k, pl.ds(c, tt, stride=ROW_TILES), :] for c in range(ROW_TILES)], axis=1)
        acc = acc + yk * g[:, k:k + 1]
    if len(o_refs) == 1:
        o_refs[0][...] = acc
    else:
        @pl.when(i < first_tiles)
        def _():
            o_refs[0][...] = acc

        @pl.when(i >= first_tiles)
        def _():
            o_refs[1][...] = acc


def _combine(y_rows, dest, h, gates, split_rows=None):
    n = h.shape[0]
    tt = COMBINE_TILE
    blk = lambda f: pl.BlockSpec((tt, D_MODEL), f)
    if split_rows is None:
        ft = n // tt
        out_specs = blk(lambda i, d: (i, 0))
        out_shape = jax.ShapeDtypeStruct((n, D_MODEL), f32)
    else:
        ft = split_rows // tt
        out_specs = [blk(lambda i, d: (jnp.minimum(i, ft - 1), 0)), blk(lambda i, d: (jnp.maximum(i - ft, 0), 0))]
        out_shape = [jax.ShapeDtypeStruct((split_rows, D_MODEL), f32), jax.ShapeDtypeStruct((n - split_rows, D_MODEL), f32)]
    grid_spec = pltpu.PrefetchScalarGridSpec(
        num_scalar_prefetch=1,
        grid=(n // tt,),
        in_specs=[pl.BlockSpec(memory_space=pl.ANY), blk(lambda i, d: (i, 0)),
                  pl.BlockSpec((tt, TOP_K), lambda i, d: (i, 0))],
        out_specs=out_specs,
        scratch_shapes=[pltpu.VMEM((2, TOP_K, tt * ROW_TILES, LANES), f32), pltpu.SemaphoreType.DMA((2,))],
    )
    return pl.pallas_call(
        functools.partial(_combine_body, first_tiles=ft),
        grid_spec=grid_spec,
        out_shape=out_shape,
        compiler_params=_cparams("arbitrary"),
        name="combine",
    )(dest, y_rows, h, gates)


def _row_tok_body(dest_hbm, zeros_hbm, out_ref, buf0, buf1, sem, zsem):
    ch = buf0.shape[0]
    nchunks = dest_hbm.shape[0] // ch
    bufs = (buf0, buf1)
    fill = pltpu.make_async_copy(zeros_hbm, out_ref, zsem)
    fill.start()
    fill.wait()

    def copy(c, slot):
        return pltpu.make_async_copy(dest_hbm.at[pl.ds(pl.multiple_of(c * ch, ch), ch)], bufs[slot], sem.at[slot])

    copy(0, 0).start()

    def per_pair(p, carry):
        for slot in range(2):
            c = 2 * p + slot

            @pl.when(c + 1 < nchunks)
            def _():
                copy(c + 1, 1 - slot).start()

            copy(c, slot).wait()
            base = c * (ch // TOP_K)

            def scatter(t, cc):
                for k in range(TOP_K):
                    out_ref[bufs[slot][TOP_K * t + k]] = base + t
                return cc

            lax.fori_loop(0, ch // TOP_K, scatter, 0, unroll=8)
        return carry

    lax.fori_loop(0, nchunks // 2, per_pair, 0)


def _row_tok(dest, rows):
    ch = min(ROW_TOK_CHUNK, dest.shape[0] // 2)
    assert dest.shape[0] % (2 * ch) == 0
    return pl.pallas_call(
        _row_tok_body,
        in_specs=[pl.BlockSpec(memory_space=pl.ANY), pl.BlockSpec(memory_space=pl.ANY)],
        out_specs=pl.BlockSpec(memory_space=pltpu.SMEM),
        out_shape=jax.ShapeDtypeStruct((rows,), i32),
        scratch_shapes=[pltpu.SMEM((ch,), i32), pltpu.SMEM((ch,), i32), pltpu.SemaphoreType.DMA((2,)),
                        pltpu.SemaphoreType.DMA(())],
        name="row_tok",
    )(dest, jnp.zeros((rows,), i32))


def _moe(h, xn, logits, layer, ew, split_rows=None):
    n = h.shape[0]
    bm = MOE_BLOCK
    top_e, gates, rank, counts = _route(logits)
    counts = counts[0].astype(i32)
    padded = ((counts + bm - 1) // bm) * bm
    pend = jnp.cumsum(padded)
    pstart = pend - padded
    expert = jnp.arange(N_EXPERTS, dtype=i32)
    dest = jnp.sum(jnp.where(top_e[:, :, None] == expert, pstart, 0), axis=-1) + rank
    nblk = (n * TOP_K) // bm + N_EXPERTS
    row_tok = _row_tok(dest.reshape(-1), nblk * bm)
    blk_start = jnp.arange(nblk, dtype=i32) * bm
    blk_e = jnp.minimum(jnp.sum((pend[None, :] <= blk_start[:, None]).astype(i32), axis=1), N_EXPERTS - 1)
    y_rows = _experts(xn, blk_e, row_tok, pend[-1:] // bm, layer, ew)
    return _combine(y_rows, dest.reshape(-1), h, gates, split_rows)


def _rope_tables(pos, tile_rows):
    half = C_ROPE_DIM // 2
    inv = ROPE_THETA ** (-jnp.arange(half, dtype=f32) / half)
    ang = pos.astype(f32)[:, None] * inv[None, :]
    cos, sin = jnp.cos(ang), jnp.sin(ang)
    n = ang.shape[0]
    tail = jnp.zeros((n, LANES - C_NOPE_DIM - C_ROPE_DIM), f32)
    cos_t = jnp.concatenate([jnp.ones((n, C_NOPE_DIM), f32), cos, cos, tail], axis=1)
    sin_t = jnp.concatenate([jnp.zeros((n, C_NOPE_DIM), f32), -sin, sin, tail], axis=1)
    reps = max(1, tile_rows // n)
    return jnp.tile(cos_t, (reps, 1)), jnp.tile(sin_t, (reps, 1))


def _band_bias(rel_bias, rows):
    cols = A_PAST + rows
    period = cols + rows
    x = jnp.arange(period)
    x = jnp.where(x < cols, x, x - period)
    idx = jnp.clip(A_PAST - x, -A_REL_CLIP, A_REL_CLIP) + A_REL_CLIP
    v = rel_bias[:, idx].astype(f32) * LOG2E
    skew = jnp.tile(v, (1, rows))[:, :rows * (period - 1)].reshape(-1, rows, period - 1)
    r = jnp.arange(rows)[:, None]
    c = jnp.arange(cols)[None, :]
    lo = (r // CHUNK) * CHUNK
    in_band = jnp.logical_and(c >= lo, c < lo + A_BAND)
    return jnp.where(in_band[None], skew[:, :, :cols], NEG)


def _group_matrix(width, groups):
    g = np.zeros((width, width), np.float32)
    for start, size in groups:
        g[start:start + size, start:start + size] = 1.0 / size
    return jnp.asarray(g, bf16)


def _layer_weights(l, p):
    w = p['w_in'][l]
    seg = lambda o, n: w[:, o:o + n]
    zc = lambda n: jnp.zeros((D_MODEL, n), f32)
    o_aq, o_ak, o_av, o_bq, o_bk, o_bv = 0, 384, 768, 1152, 1408, 1664
    o_blr, o_bg, o_cq, o_ckv, o_ckr = 2048, 2064, 2448, 2640, 2768
    w_in = jnp.concatenate([
        seg(o_aq, 384), seg(o_ak, 384), seg(o_av, 384), seg(o_bq, 256), seg(o_bk, 256), seg(o_bv, 384),
        seg(o_bg, 384), seg(o_cq, C_Q_RANK), zc(P_CKV - P_CQ - C_Q_RANK), seg(o_ckv, 128),
        zc(C_NOPE_DIM), seg(o_ckr, 32), seg(o_blr, 16), zc(IN_P - P_BLR - B_GATE_RANK)], axis=1).astype(bf16)
    hq = C_NOPE_DIM + C_ROPE_DIM
    pad_heads = lambda a, n: jnp.pad(a.reshape(a.shape[0], C_HEADS, n), ((0, 0), (0, 0), (0, C_PAD_DIM - n))
                                     ).reshape(a.shape[0], C_QW)
    wuq = pad_heads(p['c_w_uq'][l], hq).astype(bf16)
    wukv = p['c_w_ukv'][l].reshape(C_KV_RANK, C_HEADS, C_NOPE_DIM + C_V_DIM)
    wukv = jnp.concatenate([wukv[:, :, :C_NOPE_DIM].reshape(C_KV_RANK, -1),
                            pad_heads(wukv[:, :, C_NOPE_DIM:].reshape(C_KV_RANK, -1), C_V_DIM)], axis=1).astype(bf16)
    r2 = lambda a: a.reshape(1, -1)
    z = lambda n: jnp.zeros((n,), f32)
    tail = C_PAD_DIM - hq
    head_groups = [(hd * C_PAD_DIM + o, n) for hd in range(C_HEADS) for o, n in ((0, C_NOPE_DIM), (C_NOPE_DIM, C_ROPE_DIM))]
    place = np.zeros((C_HEADS * C_NOPE_DIM + C_ROPE_DIM, C_QW), np.float32)
    for hd in range(C_HEADS):
        place[hd * C_NOPE_DIM + np.arange(C_NOPE_DIM), hd * C_PAD_DIM + np.arange(C_NOPE_DIM)] = 1.0
        place[C_HEADS * C_NOPE_DIM + np.arange(C_ROPE_DIM), hd * C_PAD_DIM + C_NOPE_DIM + np.arange(C_ROPE_DIM)] = 1.0
    rw = p['router_w'][l]
    rw_hi = rw.astype(bf16)
    return {
        'ln1_g': r2(p['ln1_g'][l]), 'ln2_g': r2(p['ln2_g'][l]), 'w_in': w_in,
        'g_a': _group_matrix(A_W, [(hd * A_HEAD_DIM, A_HEAD_DIM) for hd in range(A_HEADS)]),
        'a_q_g': r2(jnp.tile(p['a_q_g'][l], A_HEADS)), 'a_k_g': r2(jnp.tile(p['a_k_g'][l], A_HEADS)),
        'bias_p': _band_bias(p['a_rel_bias'][l], BAND_ROWS), 'bias_s': _band_bias(p['a_rel_bias'][l], CHUNK),
        'b_gate_w2': p['b_gate_w2'][l], 'b_gate_b': r2(p['b_gate_b'][l]), 'b_out_g': r2(p['b_out_g'][l]),
        'c_qa_g': r2(p['c_qa_g'][l]), 'c_w_uq': wuq, 'c_kva_g': r2(p['c_kva_g'][l]), 'c_w_ukv': wukv,
        'g_c': _group_matrix(C_QW, head_groups),
        'c_q_gain': r2(jnp.tile(jnp.concatenate([p['c_qn_g'][l], p['c_qr_g'][l], z(tail)]), C_HEADS)),
        'c_k_gain': r2(jnp.tile(p['c_kn_g'][l], C_HEADS)),
        'g_k': _group_matrix(C_HEADS * C_NOPE_DIM, [(hd * C_NOPE_DIM, C_NOPE_DIM) for hd in range(C_HEADS)]),
        'c_kr_gain': r2(jnp.concatenate([z(C_NOPE_DIM), p['c_kr_g'][l], z(tail)])),
        'k_place': jnp.asarray(place, bf16),
        'w_out': p['w_out'][l].astype(bf16),
        'router_w_hi': rw_hi, 'router_w_lo': (rw - rw_hi.astype(f32)).astype(bf16), 'router_b': r2(p['router_b'][l]),
    }


def _token_mix(h_all, row0, b_, t, lw, cos, sin, past):
    aq, ak, av, akb, avb, slab, cq, ckv, ckr = _proj(h_all, row0, b_ * t, lw, cos, sin)
    r3 = lambda a: a.reshape(b_, t, a.shape[-1])
    ck, cv = _mla_keys(ckv, ckr, lw)
    if past is None:
        a_o = _band(r3(aq), r3(akb), r3(avb), None, None, lw['bias_p'])
        s0_t = jnp.zeros((b_, B_HEADS, B_VAL_DIM, B_KEY_DIM), f32)
        c_o = _mla_causal(r3(cq), r3(ck), r3(cv))
    else:
        a_k, a_v, b_s, c_kv, c_kr, layer = past
        lp = a_k.shape[1]
        v_past = jnp.concatenate([a_v.astype(bf16), jnp.ones(a_v.shape[:-1] + (1,), bf16),
                                  jnp.zeros(a_v.shape[:-1] + (V_EXT - A_HEAD_DIM - 1,), bf16)], axis=-1)
        a_o = _band(r3(aq), r3(akb), r3(avb), a_k.reshape(b_, lp, A_W).astype(bf16),
                    v_past.reshape(b_, lp, A_VW), lw['bias_s'])
        s0_t = jnp.swapaxes(b_s, -1, -2)
        c_o = _mla_sample(r3(cq), c_kv, c_kr, layer, r3(ck), r3(cv), lw)
    b_o, s_t = _gla(r3(slab), s0_t, lw['b_out_g'])
    flat = lambda a: a.reshape(b_ * t, a.shape[-1])
    la = min(A_PAST, t)
    state = (r3(ak)[:, t - la:].reshape(b_, la, A_HEADS, A_HEAD_DIM),
             r3(av)[:, t - la:].reshape(b_, la, A_HEADS, A_HEAD_DIM),
             jnp.swapaxes(s_t, -1, -2), r3(ckv), r3(ckr))
    return (flat(a_o), flat(b_o), flat(c_o)), state


def kernel(x_prompt, x_sample, cache_a_k, cache_a_v, state_b_s, cache_c_kv, cache_c_kr, ln1_g, ln2_g, w_in, a_q_g, a_k_g, a_rel_bias, b_gate_w2, b_gate_b, b_out_g, c_qa_g, c_w_uq, c_kva_g, c_w_ukv, c_qn_g, c_qr_g, c_kn_g, c_kr_g, w_out, router_w, router_b, e_w_gate, e_b_gate, e_w_up, e_b_up, e_w_down, e_b_down):
    params = dict(ln1_g=ln1_g, ln2_g=ln2_g, w_in=w_in, a_q_g=a_q_g, a_k_g=a_k_g, a_rel_bias=a_rel_bias,
                  b_gate_w2=b_gate_w2, b_gate_b=b_gate_b, b_out_g=b_out_g, c_qa_g=c_qa_g, c_w_uq=c_w_uq,
                  c_kva_g=c_kva_g, c_w_ukv=c_w_ukv, c_qn_g=c_qn_g, c_qr_g=c_qr_g, c_kn_g=c_kn_g,
                  c_kr_g=c_kr_g, w_out=w_out, router_w=router_w, router_b=router_b)
    ew = dict(e_w_gate=e_w_gate, e_b_gate=e_b_gate[:, :, None, :], e_w_up=e_w_up, e_b_up=e_b_up[:, :, None, :],
              e_w_down=e_w_down, e_b_down=e_b_down[:, :, None, :])
    bp, tp, _ = x_prompt.shape
    bs, ts, _ = x_sample.shape
    depth = w_in.shape[0]
    n_p = bp * tp
    cos_p, sin_p = _rope_tables(jnp.arange(tp), ROW_TILE)
    cos_s, sin_s = _rope_tables(PAST_LEN + jnp.arange(ts), ROW_TILE)
    x_p = x_prompt.reshape(n_p, D_MODEL)
    x_s = x_sample.reshape(bs * ts, D_MODEL)
    res_p, res_s = (x_p, 0), (x_s, 0)
    st_p, st_s = [], []
    for l in range(depth):
        lw = _layer_weights(l, params)
        mix_p, s_p = _token_mix(res_p[0], res_p[1], bp, tp, lw, cos_p, sin_p, None)
        mix_s, s_s = _token_mix(res_s[0], res_s[1], bs, ts, lw, cos_s, sin_s,
                                (cache_a_k[l], cache_a_v[l], state_b_s[l], cache_c_kv, cache_c_kr, l))
        h, xn, logits = _outproj(mix_p, mix_s, res_p, res_s, lw)
        h = _moe(h, xn, logits, l, ew, split_rows=n_p if l == depth - 1 else None)
        res_p, res_s = (h, 0), (h, n_p)
        st_p.append(s_p)
        st_s.append(s_s)
    h_p, h_s = h
    stack = lambda sts, i: jnp.stack([s[i] for s in sts])
    return (h_p.reshape(bp, tp, D_MODEL), h_s.reshape(bs, ts, D_MODEL),
            stack(st_p, 0), stack(st_p, 1), stack(st_p, 2), stack(st_p, 3), stack(st_p, 4),
            stack(st_s, 0), stack(st_s, 1), stack(st_s, 2), stack(st_s, 3), stack(st_s, 4))
```
